```python
import jax, jax.numpy as jnp
from jax import lax
import numpy as np

D_MODEL = 1024
BATCH = 1
SEQ = 16384
DEPTH = 2

GRID_W = 64
CTX_LEN = 256
N_MIXERS = 2
GLA_HEADS = 4
GLA_DK = D_MODEL // 2
GLA_DV = D_MODEL
GLA_HEAD_K = GLA_DK // GLA_HEADS
GLA_HEAD_V = GLA_DV // GLA_HEADS
GLA_GATE_RANK = 16
GLA_TAU = 16.0
GLA_CHUNK = 64
LRU_WIDTH = D_MODEL
LRU_BLOCKS = 4
LRU_BLOCK = LRU_WIDTH // LRU_BLOCKS
LRU_C = 8.0
CONV_W = 4
CONV_LEFT = 2
FFN_HIDDEN = -(-8 * D_MODEL // (3 * 256)) * 256
EPS = 1e-6

kernel_name = "hybrid_gla_rglru_prefix_dit"


def rmsnorm(x, g):
    xf = x.astype(jnp.float32)
    y = xf * lax.rsqrt(jnp.mean(xf * xf, axis=-1, keepdims=True) + EPS)
    return (y * g.astype(jnp.float32)).astype(x.dtype)


def swiglu(h, w1, w3, w2):
    return (jax.nn.silu(h @ w1) * (h @ w3)) @ w2


def _heads(t, n):
    B, L, F = t.shape
    return t.reshape(B, L, n, F // n).transpose(0, 2, 1, 3)


def _flip_if(t, rev, axis):
    return jnp.flip(t, axis=axis) if rev else t


def gla_chunked(q, k, v, log_a, s0):
    B, H, L, dk = q.shape
    dv = v.shape[-1]
    C = GLA_CHUNK
    N = L // C
    q = q.reshape(B, H, N, C, dk)
    k = k.reshape(B, H, N, C, dk)
    v = v.reshape(B, H, N, C, dv)
    b = jnp.cumsum(log_a.reshape(B, H, N, C, dk), axis=3)
    b_last = b[:, :, :, -1:, :]
    q_dec = q * jnp.exp(b)
    k_inv = k * jnp.exp(-b)
    k_to_end = k * jnp.exp(b_last - b)
    mask = jnp.tril(jnp.ones((C, C), dtype=bool))
    att = jnp.where(mask, jnp.einsum('bhnik,bhnjk->bhnij', q_dec, k_inv), 0.0)
    o_intra = jnp.einsum('bhnij,bhnjv->bhniv', att, v)
    u = jnp.einsum('bhnck,bhncv->bhnkv', k_to_end, v)
    decay = jnp.exp(b_last[:, :, :, 0, :])

    def step(s, inp):
        d, uu = inp
        return d[..., None] * s + uu, s

    s_final, s_prev = lax.scan(step, s0, (jnp.moveaxis(decay, 2, 0), jnp.moveaxis(u, 2, 0)))
    s_prev = jnp.moveaxis(s_prev, 0, 2)
    o_inter = jnp.einsum('bhnck,bhnkv->bhncv', q_dec, s_prev)
    return (o_intra + o_inter).reshape(B, H, L, dv), s_final


def gla_mixer(h_c, h_l, w_in, gw1, gw2, gb, hn_g, w_out, need_ctx):
    f32 = jnp.float32
    B = h_l.shape[0]

    def project(h):
        z = h @ w_in
        q, k, v, g = jnp.split(z, [GLA_DK, 2 * GLA_DK, 2 * GLA_DK + GLA_DV], axis=-1)
        q = _heads(q, GLA_HEADS).astype(f32) * (GLA_HEAD_K ** -0.5)
        return q, _heads(k, GLA_HEADS).astype(f32), _heads(v, GLA_HEADS).astype(f32), g

    def log_gate(h, d):
        z = (h @ gw1[d]) @ gw2[d] + gb[d]
        return _heads(jax.nn.log_sigmoid(z.astype(f32)) / GLA_TAU, GLA_HEADS)

    qc, kc, vc, gc = project(h_c)
    ql, kl, vl, gl = project(h_l)
    s0 = jnp.zeros((B, GLA_HEADS, GLA_HEAD_K, GLA_HEAD_V), f32)
    outs_c, outs_l = [], []
    for d in range(2):
        rev = d == 1
        oc, s_ctx = gla_chunked(_flip_if(qc, rev, 2), _flip_if(kc, rev, 2), _flip_if(vc, rev, 2),
                                _flip_if(log_gate(h_c, d), rev, 2), s0)
        ol, _ = gla_chunked(_flip_if(ql, rev, 2), _flip_if(kl, rev, 2), _flip_if(vl, rev, 2),
                            _flip_if(log_gate(h_l, d), rev, 2), s_ctx)
        outs_c.append(_flip_if(oc, rev, 2))
        outs_l.append(_flip_if(ol, rev, 2))

    def finish(o, g, h):
        Bh, H, L, dv = o.shape
        o = rmsnorm(o, hn_g).transpose(0, 2, 1, 3).reshape(Bh, L, H * dv)
        return ((o * jax.nn.silu(g.astype(f32))) @ w_out).astype(h.dtype)

    y_l = finish(outs_l[0] + outs_l[1], gl, h_l)
    y_c = finish(outs_c[0] + outs_c[1], gc, h_c) if need_ctx else None
    return y_c, y_l


def conv_centred(u, w, b):
    L = u.shape[1]
    up = jnp.pad(u, ((0, 0), (CONV_LEFT, CONV_W - 1 - CONV_LEFT), (0, 0)))
    y = b
    for j in range(CONV_W):
        y = y + up[:, j:j + L, :] * w[j]
    return y


def rglru_coeffs(u, wa, ba, wx, bx, lam):
    B, L, W = u.shape
    ub = u.reshape(B, L, LRU_BLOCKS, LRU_BLOCK)
    r = jax.nn.sigmoid(jnp.einsum('blgi,gij->blgj', ub, wa).reshape(B, L, W) + ba)
    i = jax.nn.sigmoid(jnp.einsum('blgi,gij->blgj', ub, wx).reshape(B, L, W) + bx)
    log_a = -LRU_C * r * jax.nn.softplus(-lam.astype(jnp.float32))
    a = jnp.exp(log_a)
    return a, jnp.sqrt(-jnp.expm1(2.0 * log_a)) * (i * u)


def linear_scan(a, b, h0):
    b = b.at[:, 0].add(a[:, 0] * h0)

    def comb(left, right):
        a1, b1 = left
        a2, b2 = right
        return a1 * a2, a2 * b1 + b2

    _, h = lax.associative_scan(comb, (a, b), axis=1)
    return h


def rglru_mixer(h_c, h_l, w_in, conv_w, conv_b, ga_w, ga_b, gx_w, gx_b, lam, w_out, need_ctx):
    f32 = jnp.float32
    B = h_l.shape[0]

    def branches(h):
        z = h @ w_in
        y, u = jnp.split(z, 2, axis=-1)
        return jax.nn.gelu(y.astype(f32)), conv_centred(u, conv_w, conv_b).astype(f32)

    yc, uc = branches(h_c)
    yl, ul = branches(h_l)
    h0 = jnp.zeros((B, LRU_WIDTH), f32)
    hs_c, hs_l = [], []
    for d in range(2):
        rev = d == 1
        a, b = rglru_coeffs(_flip_if(uc, rev, 1), ga_w[d], ga_b[d], gx_w[d], gx_b[d], lam[d])
        hc = linear_scan(a, b, h0)
        a, b = rglru_coeffs(_flip_if(ul, rev, 1), ga_w[d], ga_b[d], gx_w[d], gx_b[d], lam[d])
        hl = linear_scan(a, b, hc[:, -1])
        hs_c.append(_flip_if(hc, rev, 1))
        hs_l.append(_flip_if(hl, rev, 1))
    y_l = (((hs_l[0] + hs_l[1]) * yl) @ w_out).astype(h_l.dtype)
    y_c = (((hs_c[0] + hs_c[1]) * yc) @ w_out).astype(h_c.dtype) if need_ctx else None
    return y_c, y_l


def setup_inputs(seed: int = 0) -> dict:
    key = jax.random.key(seed)
    ks = list(jax.random.split(key, 32))
    n_gla = (DEPTH + 1) // 2
    n_lru = DEPTH // 2
    D = D_MODEL
    nrm = lambda k, shape, s: jax.random.normal(k, shape, jnp.float32) * s
    u_a = jax.random.uniform(ks[21], (n_lru, 2, LRU_WIDTH), jnp.float32, 0.9, 0.999)
    s_root = u_a ** (1.0 / LRU_C)
    return {
        'x': nrm(ks[0], (BATCH, SEQ, D), 1.0),
        'c': nrm(ks[1], (BATCH, D), 1.0),
        'ctx': nrm(ks[2], (BATCH, CTX_LEN, D), 1.0),
        'c_ctx': nrm(ks[3], (D,), 1.0),
        'norm_mix_g': 1.0 + nrm(ks[4], (DEPTH, D), 0.1),
        'norm_ffn_g': 1.0 + nrm(ks[5], (DEPTH, D), 0.1),
        'w_mod': nrm(ks[6], (DEPTH, D, 6 * D), D ** -0.5),
        'b_mod': nrm(ks[7], (DEPTH, 6 * D), 0.02),
        'gla_w_in': nrm(ks[8], (n_gla, D, 2 * GLA_DK + 2 * GLA_DV), D ** -0.5),
        'gla_gate_w1': nrm(ks[9], (n_gla, 2, D, GLA_GATE_RANK), D ** -0.5),
        'gla_gate_w2': nrm(ks[10], (n_gla, 2, GLA_GATE_RANK, GLA_DK), GLA_GATE_RANK ** -0.5),
        'gla_gate_b': nrm(ks[11], (n_gla, 2, GLA_DK), 0.5),
        'gla_head_norm_g': 1.0 + nrm(ks[12], (n_gla, GLA_HEAD_V), 0.1),
        'gla_w_out': nrm(ks[13], (n_gla, GLA_DV, D), GLA_DV ** -0.5),
        'lru_w_in': nrm(ks[14], (n_lru, D, 2 * LRU_WIDTH), D ** -0.5),
        'lru_conv_w': nrm(ks[15], (n_lru, CONV_W, LRU_WIDTH), CONV_W ** -0.5),
        'lru_conv_b': nrm(ks[16], (n_lru, LRU_WIDTH), 0.02),
        'lru_gate_a_w': nrm(ks[17], (n_lru, 2, LRU_BLOCKS, LRU_BLOCK, LRU_BLOCK), LRU_BLOCK ** -0.5),
        'lru_gate_a_b': nrm(ks[18], (n_lru, 2, LRU_WIDTH), 0.1),
        'lru_gate_x_w': nrm(ks[19], (n_lru, 2, LRU_BLOCKS, LRU_BLOCK, LRU_BLOCK), LRU_BLOCK ** -0.5),
        'lru_gate_x_b': nrm(ks[20], (n_lru, 2, LRU_WIDTH), 0.1),
        'lru_lambda': jnp.log(s_root) - jnp.log1p(-s_root),
        'lru_w_out': nrm(ks[22], (n_lru, LRU_WIDTH, D), LRU_WIDTH ** -0.5),
        'ffn_w1': nrm(ks[23], (DEPTH, D, FFN_HIDDEN), D ** -0.5),
        'ffn_w3': nrm(ks[24], (DEPTH, D, FFN_HIDDEN), D ** -0.5),
        'ffn_w2': nrm(ks[25], (DEPTH, FFN_HIDDEN, D), FFN_HIDDEN ** -0.5),
        'final_norm_g': 1.0 + nrm(ks[26], (D,), 0.1),
    }


def reference(x, c, ctx, c_ctx, norm_mix_g, norm_ffn_g, w_mod, b_mod,
              gla_w_in, gla_gate_w1, gla_gate_w2, gla_gate_b, gla_head_norm_g, gla_w_out,
              lru_w_in, lru_conv_w, lru_conv_b, lru_gate_a_w, lru_gate_a_b, lru_gate_x_w,
              lru_gate_x_b, lru_lambda, lru_w_out, ffn_w1, ffn_w3, ffn_w2, final_norm_g):
    B, S, D = x.shape
    rows = S // GRID_W

    def to_col(t):
        return t.reshape(B, rows, GRID_W, D).transpose(0, 2, 1, 3).reshape(B, S, D)

    def from_col(t):
        return t.reshape(B, GRID_W, rows, D).transpose(0, 2, 1, 3).reshape(B, S, D)

    h_ctx = ctx
    for i in range(DEPTH):
        last = i == DEPTH - 1
        j = i // N_MIXERS
        m_l = (jax.nn.silu(c) @ w_mod[i] + b_mod[i])[:, None, :]
        m_c = (jax.nn.silu(c_ctx) @ w_mod[i] + b_mod[i])[None, None, :]
        sh1, sc1, g1, sh2, sc2, g2 = jnp.split(m_l, 6, axis=-1)
        csh1, csc1, cg1, csh2, csc2, cg2 = jnp.split(m_c, 6, axis=-1)
        a_l = rmsnorm(x, norm_mix_g[i]) * (1.0 + sc1) + sh1
        a_c = rmsnorm(h_ctx, norm_mix_g[i]) * (1.0 + csc1) + csh1
        if i % N_MIXERS == 0:
            y_c, y_l = gla_mixer(a_c, a_l, gla_w_in[j], gla_gate_w1[j], gla_gate_w2[j], gla_gate_b[j],
                                 gla_head_norm_g[j], gla_w_out[j], not last)
        else:
            y_c, y_lc = rglru_mixer(a_c, to_col(a_l), lru_w_in[j], lru_conv_w[j], lru_conv_b[j],
                                    lru_gate_a_w[j], lru_gate_a_b[j], lru_gate_x_w[j], lru_gate_x_b[j],
                                    lru_lambda[j], lru_w_out[j], not last)
            y_l = from_col(y_lc)
        x = x + g1 * y_l
        f_l = rmsnorm(x, norm_ffn_g[i]) * (1.0 + sc2) + sh2
        x = x + g2 * swiglu(f_l, ffn_w1[i], ffn_w3[i], ffn_w2[i])
        if not last:
            h_ctx = h_ctx + cg1 * y_c
            f_c = rmsnorm(h_ctx, norm_ffn_g[i]) * (1.0 + csc2) + csh2
            h_ctx = h_ctx + cg2 * swiglu(f_c, ffn_w1[i], ffn_w3[i], ffn_w2[i])
    return rmsnorm(x, final_norm_g)
```

```python
import functools

import jax
import jax.numpy as jnp
from jax import lax
from jax.experimental import pallas as pl
from jax.experimental.pallas import tpu as pltpu

D = 1024
SEQ = 16384
GRID_W = 64
ROWS = SEQ // GRID_W
CTX = 256
TM = 256
NT = SEQ // TM
HEADS = 4
DK = 512
DV = 1024
HK = DK // HEADS
HV = DV // HEADS
RANK = 16
TAU = 16.0
CHUNK = 64
NCH = TM // CHUNK
LRU_BLOCKS = 4
LRU_BLOCK = D // LRU_BLOCKS
LRU_C = 8.0
FFN = 2816
EPS = 1e-6
SUB = 8
HALO = 8

F32 = jnp.float32
BF16 = jnp.bfloat16

V7X_VMEM_LIMIT = 56 * 1024 * 1024


def _dot(a, b):
    return jnp.dot(a, b, preferred_element_type=F32)


def _dot_nt(a, b):
    return lax.dot_general(a, b, (((1,), (1,)), ((), ())), preferred_element_type=F32)


def _dot_tn(a, b):
    return lax.dot_general(a, b, (((0,), (0,)), ((), ())), preferred_element_type=F32)


def _rms(x, g):
    ms = jnp.mean(x * x, axis=-1, keepdims=True)
    return x * lax.rsqrt(ms + EPS) * g


def _rms_mod(x, g, scale, shift):
    return _rms(x, g) * (1.0 + scale) + shift


def _silu(x):
    return x * jax.nn.sigmoid(x)


def _split_bf16(x):
    hi = x.astype(BF16)
    lo = (x - hi.astype(F32)).astype(BF16)
    return hi, lo


def _mod_kernel(c_ref, w_ref, b_ref, o_ref):
    s = _silu(c_ref[...])
    o_ref[...] = jnp.dot(s, w_ref[...], preferred_element_type=F32,
                         precision=lax.Precision.HIGHEST) + b_ref[...]


def _modulation(cvec, w_mod, b_mod):
    depth = w_mod.shape[0]
    tn = 1536
    return pl.pallas_call(
        _mod_kernel,
        grid=(depth, 6 * D // tn),
        in_specs=[
            pl.BlockSpec((SUB, D), lambda l, n: (0, 0)),
            pl.BlockSpec((None, D, tn), lambda l, n: (l, 0, n)),
            pl.BlockSpec((None, 1, tn), lambda l, n: (l, 0, n)),
        ],
        out_specs=pl.BlockSpec((None, SUB, tn), lambda l, n: (l, 0, n)),
        out_shape=jax.ShapeDtypeStruct((depth, SUB, 6 * D), F32),
        compiler_params=pltpu.CompilerParams(
            dimension_semantics=("arbitrary", "arbitrary"),
            vmem_limit_bytes=V7X_VMEM_LIMIT),
        name="mod",
    )(cvec, w_mod, b_mod.reshape(depth, 1, 6 * D))


def _gla_tile(q, k, v, la, s_ref, o_ref, rev):
    row = lax.broadcasted_iota(jnp.int32, (TM, TM), 0)
    col = lax.broadcasted_iota(jnp.int32, (TM, TM), 1)
    shift = CHUNK.bit_length() - 1
    same = jnp.right_shift(row, shift) == jnp.right_shift(col, shift)
    tri = same & ((col >= row) if rev else (col <= row))
    tri_b = tri.astype(BF16)
    same_b = same.astype(BF16)
    la_hi, la_lo = _split_bf16(la)
    b = _dot(tri_b, la_hi) + _dot(tri_b, la_lo)
    bl = _dot(same_b, la_hi) + _dot(same_b, la_lo)
    q_dec = (q * jnp.exp(b)).astype(BF16)
    k_inv = (k * jnp.exp(-b)).astype(BF16)
    k_end = (k * jnp.exp(bl - b)).astype(BF16)
    decay = jnp.exp(bl)
    order = range(NCH - 1, -1, -1) if rev else range(NCH)
    for h in range(HEADS):
        ks = slice(h * HK, (h + 1) * HK)
        vs = slice(h * HV, (h + 1) * HV)
        qd = q_dec[:, ks]
        vh = v[:, vs]
        att = jnp.where(tri, _dot_nt(qd, k_inv[:, ks]), 0.0).astype(BF16)
        o_intra = _dot(att, vh)
        st = s_ref[h]
        for c in order:
            rs = slice(c * CHUNK, (c + 1) * CHUNK)
            o_inter = _dot_nt(qd[rs], st.astype(BF16))
            o_ref[rs, vs] = o_intra[rs] + o_inter
            u_t = _dot_tn(vh[rs], k_end[rs, ks])
            st = decay[c * CHUNK:c * CHUNK + 1, ks] * st + u_t
        s_ref[h] = st


def _gla_fwd_kernel(x_ref, ctx_ref, mod_ref, gmix_ref, win_ref, gw1_ref, gw2_ref, gb_ref,
                    q_ref, k_ref, v_ref, g_ref, la1_ref, of_ref, s_ref):
    i = pl.program_id(0)

    @pl.when(i == 0)
    def _():
        s_ref[...] = jnp.zeros_like(s_ref)

    xt = jnp.where(i == 0, ctx_ref[...], x_ref[...])
    mod = mod_ref[...]
    a = _rms_mod(xt, gmix_ref[...], mod[1:2], mod[0:1]).astype(BF16)
    z = _dot(a, win_ref[...])
    q = z[:, :DK] * (HK ** -0.5)
    k = z[:, DK:2 * DK]
    v = z[:, 2 * DK:2 * DK + DV].astype(BF16)
    t = _dot(a, gw1_ref[...]).astype(BF16)
    zg = _dot(t, gw2_ref[...]) + gb_ref[...]
    la = jax.nn.log_sigmoid(zg) * (1.0 / TAU)
    q_ref[...] = q.astype(BF16)
    k_ref[...] = k.astype(BF16)
    v_ref[...] = v
    g_ref[...] = z[:, 2 * DK + DV:].astype(BF16)
    la1_ref[...] = la[:, DK:]
    _gla_tile(q, k, v, la[:, :DK], s_ref, of_ref, rev=False)


def _ffn(x1, mod, gffn, w13_ref, w2_ref):
    f = _rms_mod(x1, gffn, mod[4:5], mod[3:4]).astype(BF16)
    h13 = _dot(f, w13_ref[...])
    p = (_silu(h13[:, :FFN]) * h13[:, FFN:]).astype(BF16)
    return x1 + mod[5:6] * _dot(p, w2_ref[...])


def _gla_bwd_kernel(x_ref, ctx_ref, mod_ref, q_ref, k_ref, v_ref, g_ref, la1_ref, of_ref,
                    hn_ref, wout_ref, gffn_ref, w13_ref, w2_ref,
                    xo_ref, co_ref, s_ref, ob_ref):
    i = pl.program_id(0)

    @pl.when(i == 0)
    def _():
        s_ref[...] = jnp.zeros_like(s_ref)

    _gla_tile(q_ref[...].astype(F32), k_ref[...].astype(F32), v_ref[...], la1_ref[...],
              s_ref, ob_ref, rev=True)
    o = of_ref[...] + ob_ref[...]
    hn = hn_ref[...]
    parts = [_rms(o[:, h * HV:(h + 1) * HV], hn) for h in range(HEADS)]
    on = jnp.concatenate(parts, axis=-1)
    gated = (on * _silu(g_ref[...].astype(F32))).astype(BF16)
    y = _dot(gated, wout_ref[...])
    mod = mod_ref[...]
    xt = jnp.where(i == 0, ctx_ref[...], x_ref[...])
    x1 = xt + mod[2:3] * y
    x2 = _ffn(x1, mod, gffn_ref[...], w13_ref, w2_ref)

    @pl.when(i == 0)
    def _():
        co_ref[...] = x2

    @pl.when(i > 0)
    def _():
        xo_ref[...] = x2


def _lru_in_kernel(x_ref, ctx_ref, mod_ref, gmix_ref, win_ref, y_ref, u_ref):
    i = pl.program_id(0)
    xt = jnp.where(i == 0, ctx_ref[...], x_ref[...])
    mod = mod_ref[...]
    a = _rms_mod(xt, gmix_ref[...], mod[1:2], mod[0:1]).astype(BF16)
    z = _dot(a, win_ref[...])
    y_ref[...] = jax.nn.gelu(z[:, :D]).astype(BF16)
    u_ref[...] = z[:, D:]


def _conv_gates(u_ref, up_ref, un_ref, prev_ok, next_ok, cw_ref, cb_ref, wg_ref, bg_ref,
                lam_ref, cs_ref):
    cs_ref[0:HALO] = jnp.where(prev_ok, up_ref[...], 0.0)
    cs_ref[HALO:HALO + TM] = u_ref[...]
    cs_ref[HALO + TM:] = jnp.where(next_ok, un_ref[...], 0.0)
    cw = cw_ref[...]
    uc = cb_ref[...] + cw[0:1] * cs_ref[HALO - 2:HALO - 2 + TM]
    uc = uc + cw[1:2] * cs_ref[HALO - 1:HALO - 1 + TM]
    uc = uc + cw[2:3] * cs_ref[HALO:HALO + TM]
    uc = uc + cw[3:4] * cs_ref[HALO + 1:HALO + 1 + TM]
    ub = uc.astype(BF16)
    bg = bg_ref[...]
    rs, xs = [], []
    for g in range(LRU_BLOCKS):
        sl = slice(g * LRU_BLOCK, (g + 1) * LRU_BLOCK)
        zz = _dot(ub[:, sl], wg_ref[g])
        rs.append(zz[:, :LRU_BLOCK])
        xs.append(zz[:, LRU_BLOCK:])
    r = jax.nn.sigmoid(jnp.concatenate(rs, axis=-1) + bg[0:1])
    ig = jax.nn.sigmoid(jnp.concatenate(xs, axis=-1) + bg[1:2])
    log_a = (-LRU_C) * r * jax.nn.softplus(-lam_ref[...])
    a = jnp.exp(log_a)
    b = jnp.sqrt(1.0 - a * a) * (ig * uc)
    return a, b


def _scan_tile(a, b, h_ref, out_ref, rev):
    groups = TM // SUB
    a3 = a.reshape(groups, SUB, D)
    b3 = b.reshape(groups, SUB, D)
    sub = lax.broadcasted_iota(jnp.int32, (groups, SUB, D), 1)
    s = 1
    while s < SUB:
        shift = (SUB - s) if rev else s
        a_sh = pltpu.roll(a3, shift, 1)
        b_sh = pltpu.roll(b3, shift, 1)
        m = (sub < SUB - s) if rev else (sub >= s)
        b3 = jnp.where(m, a3 * b_sh + b3, b3)
        a3 = jnp.where(m, a3 * a_sh, a3)
        s *= 2
    h = h_ref[...]
    last = 0 if rev else SUB - 1
    order = range(groups - 1, -1, -1) if rev else range(groups)
    for g in order:
        hg = a3[g] * h + b3[g]
        out_ref[g * SUB:(g + 1) * SUB, :] = hg
        h = hg[last:last + 1]
    h_ref[...] = h


def _lru_fwd_kernel(u_ref, up_ref, un_ref, cw_ref, cb_ref, wg_ref, bg_ref, lam_ref,
                    hf_ref, h_ref, cs_ref):
    i = pl.program_id(0)

    @pl.when(i == 0)
    def _():
        h_ref[...] = jnp.zeros_like(h_ref)

    a, b = _conv_gates(u_ref, up_ref, un_ref, i >= 2, (i >= 1) & (i <= NT - 1),
                       cw_ref, cb_ref, wg_ref, bg_ref, lam_ref, cs_ref)
    _scan_tile(a, b, h_ref, hf_ref, rev=False)


def _lru_bwd_kernel(x_ref, mod_ref, u_ref, up_ref, un_ref, y_ref, hf_ref,
                    cw_ref, cb_ref, wg_ref, bg_ref, lam_ref, wout_ref, gffn_ref, w13_ref, w2_ref,
                    gfin_ref, o_ref, h_ref, cs_ref, hb_ref):
    i = pl.program_id(0)

    @pl.when(i == 0)
    def _():
        h_ref[...] = jnp.zeros_like(h_ref)

    a, b = _conv_gates(u_ref, up_ref, un_ref, (i >= 1) & (i <= NT - 1), i >= 2,
                       cw_ref, cb_ref, wg_ref, bg_ref, lam_ref, cs_ref)
    _scan_tile(a, b, h_ref, hb_ref, rev=True)

    @pl.when(i > 0)
    def _():
        hs = hf_ref[...] + hb_ref[...]
        ym = (hs * y_ref[...].astype(F32)).astype(BF16)
        y = _dot(ym, wout_ref[...])
        mod = mod_ref[...]
        x1 = x_ref[...] + mod[2:3] * y
        x2 = _ffn(x1, mod, gffn_ref[...], w13_ref, w2_ref)
        o_ref[...] = _rms(x2, gfin_ref[...])


def _const(shape):
    nd = len(shape)
    return pl.BlockSpec(shape, lambda i: (0,) * nd, pipeline_mode=pl.Buffered(1))


def _params():
    return pltpu.CompilerParams(dimension_semantics=("arbitrary",),
                                vmem_limit_bytes=V7X_VMEM_LIMIT)


def kernel(x, c, ctx, c_ctx, norm_mix_g, norm_ffn_g, w_mod, b_mod, gla_w_in, gla_gate_w1, gla_gate_w2, gla_gate_b, gla_head_norm_g, gla_w_out, lru_w_in, lru_conv_w, lru_conv_b, lru_gate_a_w, lru_gate_a_b, lru_gate_x_w, lru_gate_x_b, lru_lambda, lru_w_out, ffn_w1, ffn_w3, ffn_w2, final_norm_g):
    assert x.shape == (1, SEQ, D) and ctx.shape == (1, CTX, D) and CTX == TM
    x2d = x.reshape(SEQ, D)
    ctx2d = ctx.reshape(CTX, D)
    nseq = (NT + 1) * TM

    cvec = jnp.zeros((SUB, D), F32).at[0].set(c[0]).at[1].set(c_ctx)
    m = _modulation(cvec, w_mod, b_mod)
    mods = m[:, :2].reshape(2, 2, 6, D)
    mods = jnp.pad(mods, ((0, 0), (0, 0), (0, SUB - 6), (0, 0)))

    def mod_spec():
        return pl.BlockSpec((None, SUB, D), lambda i: (jnp.where(i == 0, 1, 0), 0, 0))

    vec = lambda t: t.reshape(1, -1)
    tile = lambda w, f: pl.BlockSpec((TM, w), f)

    win = gla_w_in[0].astype(BF16)
    gw1 = jnp.concatenate([gla_gate_w1[0, 0], gla_gate_w1[0, 1]], axis=1)
    gw1 = jnp.pad(gw1, ((0, 0), (0, 128 - 2 * RANK))).astype(BF16)
    gw2 = jnp.zeros((128, 2 * DK), F32)
    gw2 = gw2.at[:RANK, :DK].set(gla_gate_w2[0, 0]).at[RANK:2 * RANK, DK:].set(gla_gate_w2[0, 1])
    gw2 = gw2.astype(BF16)
    gb = gla_gate_b[0].reshape(1, 2 * DK)

    fwd_lat = lambda i: (jnp.maximum(i - 1, 0), 0)
    seq_f = lambda i: (i, 0)
    q, k, v, g, la1, o_f = pl.pallas_call(
        _gla_fwd_kernel,
        grid=(NT + 1,),
        in_specs=[tile(D, fwd_lat), _const((CTX, D)), mod_spec(), _const((1, D)),
                  _const((D, 2 * DK + 2 * DV)), _const((D, 128)), _const((128, 2 * DK)),
                  _const((1, 2 * DK))],
        out_specs=[tile(DK, seq_f), tile(DK, seq_f), tile(DV, seq_f), tile(DV, seq_f),
                   tile(DK, seq_f), tile(DV, seq_f)],
        out_shape=[jax.ShapeDtypeStruct((nseq, DK), BF16), jax.ShapeDtypeStruct((nseq, DK), BF16),
                   jax.ShapeDtypeStruct((nseq, DV), BF16), jax.ShapeDtypeStruct((nseq, DV), BF16),
                   jax.ShapeDtypeStruct((nseq, DK), F32), jax.ShapeDtypeStruct((nseq, DV), F32)],
        scratch_shapes=[pltpu.VMEM((HEADS, HV, HK), F32)],
        compiler_params=_params(),
        name="gla_fwd",
    )(x2d, ctx2d, mods[0], vec(norm_mix_g[0]), win, gw1, gw2, gb)

    bwd_lat = lambda i: (jnp.where(i == 0, NT - 1, NT - i), 0)
    seq_b = lambda i: (jnp.where(i == 0, 0, NT + 1 - i), 0)
    w13_0 = jnp.concatenate([ffn_w1[0], ffn_w3[0]], axis=1).astype(BF16)
    x_mid, ctx_mid = pl.pallas_call(
        _gla_bwd_kernel,
        grid=(NT + 1,),
        in_specs=[tile(D, bwd_lat), _const((CTX, D)), mod_spec(),
                  tile(DK, seq_b), tile(DK, seq_b), tile(DV, seq_b), tile(DV, seq_b),
                  tile(DK, seq_b), tile(DV, seq_b),
                  _const((1, HV)), _const((DV, D)), _const((1, D)),
                  _const((D, 2 * FFN)), _const((FFN, D))],
        out_specs=[tile(D, bwd_lat), pl.BlockSpec((CTX, D), lambda i: (0, 0))],
        out_shape=[jax.ShapeDtypeStruct((SEQ, D), F32), jax.ShapeDtypeStruct((CTX, D), F32)],
        scratch_shapes=[pltpu.VMEM((HEADS, HV, HK), F32), pltpu.VMEM((TM, DV), F32)],
        compiler_params=_params(),
        name="gla_bwd",
    )(x2d, ctx2d, mods[0], q, k, v, g, la1, o_f,
      vec(gla_head_norm_g[0]), gla_w_out[0].astype(BF16), vec(norm_ffn_g[0]),
      w13_0, ffn_w2[0].astype(BF16))

    x_cols = x_mid.reshape(ROWS, GRID_W * D)
    col = lambda f: pl.BlockSpec((ROWS, D), f)
    fwd_col = lambda i: (0, jnp.maximum(i - 1, 0))
    bwd_col = lambda i: (0, jnp.where(i == 0, NT - 1, NT - i))
    y_br, u_pre = pl.pallas_call(
        _lru_in_kernel,
        grid=(NT + 1,),
        in_specs=[col(fwd_col), _const((CTX, D)), mod_spec(), _const((1, D)), _const((D, 2 * D))],
        out_specs=[tile(D, seq_f), tile(D, seq_f)],
        out_shape=[jax.ShapeDtypeStruct((nseq, D), BF16), jax.ShapeDtypeStruct((nseq, D), F32)],
        compiler_params=_params(),
        name="lru_in",
    )(x_cols, ctx_mid, mods[1], vec(norm_mix_g[1]), lru_w_in[0].astype(BF16))

    per_tile = TM // HALO
    last_halo = nseq // HALO - 1

    def halo_specs(seq):
        prev = lambda i: (jnp.maximum(seq(i)[0] * per_tile - 1, 0), 0)
        nxt = lambda i: (jnp.minimum((seq(i)[0] + 1) * per_tile, last_halo), 0)
        return [tile(D, seq), pl.BlockSpec((HALO, D), prev), pl.BlockSpec((HALO, D), nxt)]

    def gate_w(d):
        wg = jnp.concatenate([lru_gate_a_w[0, d], lru_gate_x_w[0, d]], axis=-1).astype(BF16)
        bg = jnp.stack([lru_gate_a_b[0, d], lru_gate_x_b[0, d]])
        return wg, bg, lru_lambda[0, d].reshape(1, D)

    gate_specs = [_const((CONV_W, D)), _const((1, D)), _const((LRU_BLOCKS, LRU_BLOCK, 2 * LRU_BLOCK)),
                  _const((2, D)), _const((1, D))]
    wg0, bg0, lam0 = gate_w(0)
    h_f = pl.pallas_call(
        _lru_fwd_kernel,
        grid=(NT + 1,),
        in_specs=halo_specs(seq_f) + gate_specs,
        out_specs=tile(D, seq_f),
        out_shape=jax.ShapeDtypeStruct((nseq, D), F32),
        scratch_shapes=[pltpu.VMEM((1, D), F32), pltpu.VMEM((TM + 2 * HALO, D), F32)],
        compiler_params=_params(),
        name="lru_fwd",
    )(u_pre, u_pre, u_pre, lru_conv_w[0], vec(lru_conv_b[0]), wg0, bg0, lam0)

    wg1, bg1, lam1 = gate_w(1)
    w13_1 = jnp.concatenate([ffn_w1[1], ffn_w3[1]], axis=1).astype(BF16)
    out = pl.pallas_call(
        _lru_bwd_kernel,
        grid=(NT + 1,),
        in_specs=[col(bwd_col), mod_spec()] + halo_specs(seq_b) + [tile(D, seq_b), tile(D, seq_b)]
                 + gate_specs + [_const((D, D)), _const((1, D)), _const((D, 2 * FFN)),
                                 _const((FFN, D)), _const((1, D))],
        out_specs=col(bwd_col),
        out_shape=jax.ShapeDtypeStruct((ROWS, GRID_W * D), F32),
        scratch_shapes=[pltpu.VMEM((1, D), F32), pltpu.VMEM((TM + 2 * HALO, D), F32),
                        pltpu.VMEM((TM, D), F32)],
        compiler_params=_params(),
        name="lru_bwd",
    )(x_cols, mods[1], u_pre, u_pre, u_pre, y_br, h_f,
      lru_conv_w[0], vec(lru_conv_b[0]), wg1, bg1, lam1,
      lru_w_out[0].astype(BF16), vec(norm_ffn_g[1]), w13_1, ffn_w2[1].astype(BF16),
      vec(final_norm_g))
    return out.reshape(1, SEQ, D)


CONV_W = 4
```

```python
import jax
import jax.numpy as jnp
from jax import lax
from jax.experimental import pallas as pl
from jax.experimental.pallas import tpu as pltpu

D = 1024
SEQ = 16384
GRID_W = 64
ROWS = SEQ // GRID_W
CTX = 256
TM = 256
NT = SEQ // TM
HEADS = 4
DK = 512
DV = 1024
HK = DK // HEADS
HV = DV // HEADS
RANK = 16
TAU = 16.0
CHUNK = 64
NCH = TM // CHUNK
LRU_BLOCKS = 4
LRU_BLOCK = D // LRU_BLOCKS
LRU_C = 8.0
FFN = 2816
EPS = 1e-6
SUB = 8
CONV_W = 4
CONV_LEFT = 2
NGRP = GRID_W // SUB

F32 = jnp.float32
BF16 = jnp.bfloat16

V7X_VMEM_LIMIT = 56 * 1024 * 1024


def _dot(a, b):
    return jnp.dot(a, b, preferred_element_type=F32)


def _dot_nt(a, b):
    return lax.dot_general(a, b, (((1,), (1,)), ((), ())), preferred_element_type=F32)


def _dot_tn(a, b):
    return lax.dot_general(a, b, (((0,), (0,)), ((), ())), preferred_element_type=F32)


def _rms(x, g):
    ms = jnp.mean(x * x, axis=-1, keepdims=True)
    return x * lax.rsqrt(ms + EPS) * g


def _rms_mod(x, g, scale, shift):
    return _rms(x, g) * (1.0 + scale) + shift


def _silu(x):
    return x * jax.nn.sigmoid(x)


def _split_bf16(x):
    hi = x.astype(BF16)
    lo = (x - hi.astype(F32)).astype(BF16)
    return hi, lo


def _mod_kernel(c_ref, w_ref, b_ref, o_ref):
    s = _silu(c_ref[...])
    o_ref[...] = jnp.dot(s, w_ref[...], preferred_element_type=F32,
                         precision=lax.Precision.HIGHEST) + b_ref[...]


def _modulation(cvec, w_mod, b_mod):
    depth = w_mod.shape[0]
    tn = 1536
    return pl.pallas_call(
        _mod_kernel,
        grid=(depth, 6 * D // tn),
        in_specs=[
            pl.BlockSpec((SUB, D), lambda l, n: (0, 0)),
            pl.BlockSpec((None, D, tn), lambda l, n: (l, 0, n)),
            pl.BlockSpec((None, 1, tn), lambda l, n: (l, 0, n)),
        ],
        out_specs=pl.BlockSpec((None, SUB, tn), lambda l, n: (l, 0, n)),
        out_shape=jax.ShapeDtypeStruct((depth, SUB, 6 * D), F32),
        compiler_params=pltpu.CompilerParams(
            dimension_semantics=("arbitrary", "arbitrary"),
            vmem_limit_bytes=V7X_VMEM_LIMIT),
        name="mod",
    )(cvec, w_mod, b_mod.reshape(depth, 1, 6 * D))


def _gla_tile(q, k, v, la, s_ref, o_ref, rev):
    row = lax.broadcasted_iota(jnp.int32, (TM, TM), 0)
    col = lax.broadcasted_iota(jnp.int32, (TM, TM), 1)
    shift = CHUNK.bit_length() - 1
    same = jnp.right_shift(row, shift) == jnp.right_shift(col, shift)
    tri = same & ((col >= row) if rev else (col <= row))
    tri_b = tri.astype(BF16)
    same_b = same.astype(BF16)
    la_hi, la_lo = _split_bf16(la)
    b = _dot(tri_b, la_hi) + _dot(tri_b, la_lo)
    bl = _dot(same_b, la_hi) + _dot(same_b, la_lo)
    q_dec = (q * jnp.exp(b)).astype(BF16)
    k_inv = (k * jnp.exp(-b)).astype(BF16)
    k_end = (k * jnp.exp(bl - b)).astype(BF16)
    decay = jnp.exp(bl)
    order = range(NCH - 1, -1, -1) if rev else range(NCH)
    for h in range(HEADS):
        ks = slice(h * HK, (h + 1) * HK)
        vs = slice(h * HV, (h + 1) * HV)
        qd = q_dec[:, ks]
        vh = v[:, vs]
        att = jnp.where(tri, _dot_nt(qd, k_inv[:, ks]), 0.0).astype(BF16)
        o_intra = _dot(att, vh)
        st = s_ref[h]
        for c in order:
            rs = slice(c * CHUNK, (c + 1) * CHUNK)
            o_inter = _dot_nt(qd[rs], st.astype(BF16))
            o_ref[rs, vs] = o_intra[rs] + o_inter
            u_t = _dot_tn(vh[rs], k_end[rs, ks])
            st = decay[c * CHUNK:c * CHUNK + 1, ks] * st + u_t
        s_ref[h] = st


def _gla_fwd_kernel(x_ref, ctx_ref, mod_ref, gmix_ref, win_ref, gw1_ref, gw2_ref, gb_ref,
                    q_ref, k_ref, v_ref, g_ref, la1_ref, of_ref, s_ref):
    i = pl.program_id(0)

    @pl.when(i == 0)
    def _():
        s_ref[...] = jnp.zeros_like(s_ref)

    xt = jnp.where(i == 0, ctx_ref[...], x_ref[...])
    mod = mod_ref[...]
    a = _rms_mod(xt, gmix_ref[...], mod[1:2], mod[0:1]).astype(BF16)
    z = _dot(a, win_ref[...])
    q = z[:, :DK] * (HK ** -0.5)
    k = z[:, DK:2 * DK]
    v = z[:, 2 * DK:2 * DK + DV].astype(BF16)
    t = _dot(a, gw1_ref[...]).astype(BF16)
    zg = _dot(t, gw2_ref[...]) + gb_ref[...]
    la = jax.nn.log_sigmoid(zg) * (1.0 / TAU)
    q_ref[...] = q.astype(BF16)
    k_ref[...] = k.astype(BF16)
    v_ref[...] = v
    g_ref[...] = z[:, 2 * DK + DV:].astype(BF16)
    la1_ref[...] = la[:, DK:]
    _gla_tile(q, k, v, la[:, :DK], s_ref, of_ref, rev=False)


def _ffn(x1, mod, gffn, w1_ref, w3_ref, w2_ref):
    f = _rms_mod(x1, gffn, mod[4:5], mod[3:4]).astype(BF16)
    p = (_silu(_dot(f, w1_ref[...])) * _dot(f, w3_ref[...])).astype(BF16)
    return x1 + mod[5:6] * _dot(p, w2_ref[...])


def _gla_bwd_kernel(x_ref, ctx_ref, mod_ref, q_ref, k_ref, v_ref, g_ref, la1_ref, of_ref,
                    hn_ref, wout_ref, gffn_ref, w1_ref, w3_ref, w2_ref,
                    xo_ref, co_ref, s_ref, ob_ref):
    i = pl.program_id(0)

    @pl.when(i == 0)
    def _():
        s_ref[...] = jnp.zeros_like(s_ref)

    _gla_tile(q_ref[...].astype(F32), k_ref[...].astype(F32), v_ref[...], la1_ref[...],
              s_ref, ob_ref, rev=True)
    o = of_ref[...] + ob_ref[...]
    hn = hn_ref[...]
    parts = [_rms(o[:, h * HV:(h + 1) * HV], hn) for h in range(HEADS)]
    on = jnp.concatenate(parts, axis=-1)
    gated = (on * _silu(g_ref[...].astype(F32))).astype(BF16)
    y = _dot(gated, wout_ref[...])
    mod = mod_ref[...]
    xt = jnp.where(i == 0, ctx_ref[...], x_ref[...])
    x1 = xt + mod[2:3] * y
    x2 = _ffn(x1, mod, gffn_ref[...], w1_ref, w3_ref, w2_ref)

    @pl.when(i == 0)
    def _():
        co_ref[...] = x2

    @pl.when(i > 0)
    def _():
        xo_ref[...] = x2


def _lru_in_kernel(x_ref, ctx_ref, mod_ref, gmix_ref, win_ref, y_ref, u_ref, uc_ref):
    i = pl.program_id(0)
    xt = jnp.where(i == 0, ctx_ref[...], x_ref[...])
    mod = mod_ref[...]
    a = _rms_mod(xt, gmix_ref[...], mod[1:2], mod[0:1]).astype(BF16)
    z = _dot(a, win_ref[...])
    y_ref[...] = jax.nn.gelu(z[:, :D]).astype(BF16)
    u_ref[...] = z[:, D:]

    @pl.when(i == 0)
    def _():
        uc_ref[...] = z[:, D:]


def _lru_coeffs(ub, uc2, wg_ref, bg_ref, lam_row):
    zz = _dot(ub, wg_ref[...])
    bg = bg_ref[...]
    r = jax.nn.sigmoid(zz[:, :LRU_BLOCK] + bg[0:1])
    ig = jax.nn.sigmoid(zz[:, LRU_BLOCK:] + bg[1:2])
    a = jnp.exp((-LRU_C) * r * jax.nn.softplus(-lam_row))
    b = jnp.sqrt(1.0 - a * a) * (ig * uc2)
    shape = (ROWS, SUB, LRU_BLOCK)
    return a.reshape(shape), b.reshape(shape)


def _lru_scan_kernel(u_ref, uctx_ref, hp_ref, hn_ref, cw_ref, cb_ref, wg0_ref, wg1_ref,
                     bg0_ref, bg1_ref, lam_ref,
                     s0_ref, pf_ref, pb_ref, sum_ref,
                     e_ref, af_ref, bf_ref, ab_ref, bb_ref):
    wg = pl.program_id(1)
    u = jnp.where(wg == 0, uctx_ref[...], u_ref[...])
    prev_ok = wg >= 2
    next_ok = (wg >= 1) & (wg <= NGRP - 1)
    sub = lax.broadcasted_iota(jnp.int32, (SUB, LRU_BLOCK), 0)
    hp = hp_ref[...]
    hn = hn_ref[...]

    def from_prev_column(cur, halo):
        edge = jnp.where(prev_ok, pltpu.roll(halo, 1, 0), 0.0)
        return jnp.where(sub == 0, edge, pltpu.roll(cur, 1, 0))

    e_ref[CONV_LEFT:CONV_LEFT + ROWS] = u
    e_ref[0] = from_prev_column(u[ROWS - 2], hp[0])
    e_ref[1] = from_prev_column(u[ROWS - 1], hp[1])
    edge = jnp.where(next_ok, pltpu.roll(hn[0], SUB - 1, 0), 0.0)
    e_ref[CONV_LEFT + ROWS] = jnp.where(sub == SUB - 1, edge, pltpu.roll(u[0], SUB - 1, 0))
    cw = cw_ref[...]
    uc = cb_ref[...] + cw[0:1] * e_ref[0:ROWS]
    for j in range(1, CONV_W):
        uc = uc + cw[j:j + 1] * e_ref[j:j + ROWS]
    uc2 = uc.reshape(ROWS * SUB, LRU_BLOCK)
    ub = uc2.astype(BF16)
    lam = lam_ref[...]
    a0, b0 = _lru_coeffs(ub, uc2, wg0_ref, bg0_ref, lam[0:1])
    af_ref[...] = a0
    bf_ref[...] = b0
    a1, b1 = _lru_coeffs(ub, uc2, wg1_ref, bg1_ref, lam[1:2])
    ab_ref[...] = a1
    bb_ref[...] = b1

    zero = jnp.zeros((SUB, LRU_BLOCK), F32)
    one = jnp.ones((SUB, LRU_BLOCK), F32)

    def fwd(r, carry):
        h, p = carry
        a = af_ref[r]
        h = a * h + bf_ref[r]
        p = a * p
        s0_ref[r] = h
        pf_ref[r] = p
        return h, p

    h, p = lax.fori_loop(0, ROWS, fwd, (zero, one), unroll=8)
    sum_ref[0] = p
    sum_ref[1] = h

    def bwd(t, carry):
        h, p = carry
        r = ROWS - 1 - t
        a = ab_ref[r]
        h = a * h + bb_ref[r]
        p = a * p
        s0_ref[r] = s0_ref[r] + h
        pb_ref[r] = p
        return h, p

    h, p = lax.fori_loop(0, ROWS, bwd, (zero, one), unroll=8)
    sum_ref[2] = p
    sum_ref[3] = h


def _lru_chain_kernel(sum_ref, hf_ref, hb_ref):
    s = sum_ref[1, 0:1, :]
    for w in range(GRID_W):
        hf_ref[w:w + 1, :] = s
        s = sum_ref[0, SUB + w:SUB + w + 1, :] * s + sum_ref[1, SUB + w:SUB + w + 1, :]
    s = sum_ref[3, 0:1, :]
    for w in range(GRID_W - 1, -1, -1):
        hb_ref[w:w + 1, :] = s
        s = sum_ref[2, SUB + w:SUB + w + 1, :] * s + sum_ref[3, SUB + w:SUB + w + 1, :]


def _lru_out_kernel(x_ref, mod_ref, y_ref, s0_ref, pf_ref, pb_ref, hf_ref, hb_ref,
                    wout_ref, gffn_ref, w1_ref, w3_ref, w2_ref, gfin_ref, o_ref):
    hs = s0_ref[...] + pf_ref[...] * hf_ref[...] + pb_ref[...] * hb_ref[...]
    ym = (hs * y_ref[...].astype(F32)).astype(BF16)
    mod = mod_ref[...]
    x1 = x_ref[...] + mod[2:3] * _dot(ym, wout_ref[...])
    x2 = _ffn(x1, mod, gffn_ref[...], w1_ref, w3_ref, w2_ref)
    o_ref[...] = _rms(x2, gfin_ref[...])


def _const(shape, index=None):
    nd = len(shape)
    index = (0,) * nd if index is None else index
    return pl.BlockSpec(shape, lambda *_: index, pipeline_mode=pl.Buffered(1))


def _params(n_axes=1):
    return pltpu.CompilerParams(dimension_semantics=("arbitrary",) * n_axes,
                                vmem_limit_bytes=V7X_VMEM_LIMIT)


def kernel(x, c, ctx, c_ctx, norm_mix_g, norm_ffn_g, w_mod, b_mod, gla_w_in, gla_gate_w1, gla_gate_w2, gla_gate_b, gla_head_norm_g, gla_w_out, lru_w_in, lru_conv_w, lru_conv_b, lru_gate_a_w, lru_gate_a_b, lru_gate_x_w, lru_gate_x_b, lru_lambda, lru_w_out, ffn_w1, ffn_w3, ffn_w2, final_norm_g):
    assert x.shape == (1, SEQ, D) and ctx.shape == (1, CTX, D) and CTX == TM == ROWS
    x2d = x.reshape(SEQ, D)
    ctx2d = ctx.reshape(CTX, D)
    nseq = (NT + 1) * TM

    cvec = jnp.zeros((SUB, D), F32).at[0].set(c[0]).at[1].set(c_ctx)
    m = _modulation(cvec, w_mod, b_mod)
    mods = m[:, :2].reshape(2, 2, 6, D)
    mods = jnp.pad(mods, ((0, 0), (0, 0), (0, SUB - 6), (0, 0)))

    def mod_spec():
        return pl.BlockSpec((None, SUB, D), lambda i: (jnp.where(i == 0, 1, 0), 0, 0))

    vec = lambda t: t.reshape(1, -1)
    tile = lambda w, f: pl.BlockSpec((TM, w), f)
    w1b, w3b, w2b = ffn_w1.astype(BF16), ffn_w3.astype(BF16), ffn_w2.astype(BF16)

    def ffn_specs(layer):
        return [_const((1, D)), _const((None, D, FFN), (layer, 0, 0)),
                _const((None, D, FFN), (layer, 0, 0)), _const((None, FFN, D), (layer, 0, 0))]

    win = gla_w_in[0].astype(BF16)
    gw1 = jnp.concatenate([gla_gate_w1[0, 0], gla_gate_w1[0, 1]], axis=1)
    gw1 = jnp.pad(gw1, ((0, 0), (0, 128 - 2 * RANK))).astype(BF16)
    gw2 = jnp.zeros((128, 2 * DK), F32)
    gw2 = gw2.at[:RANK, :DK].set(gla_gate_w2[0, 0]).at[RANK:2 * RANK, DK:].set(gla_gate_w2[0, 1])
    gw2 = gw2.astype(BF16)
    gb = gla_gate_b[0].reshape(1, 2 * DK)

    fwd_lat = lambda i: (jnp.maximum(i - 1, 0), 0)
    seq_f = lambda i: (i, 0)
    q, k, v, g, la1, o_f = pl.pallas_call(
        _gla_fwd_kernel,
        grid=(NT + 1,),
        in_specs=[tile(D, fwd_lat), _const((CTX, D)), mod_spec(), _const((1, D)),
                  _const((D, 2 * DK + 2 * DV)), _const((D, 128)), _const((128, 2 * DK)),
                  _const((1, 2 * DK))],
        out_specs=[tile(DK, seq_f), tile(DK, seq_f), tile(DV, seq_f), tile(DV, seq_f),
                   tile(DK, seq_f), tile(DV, seq_f)],
        out_shape=[jax.ShapeDtypeStruct((nseq, DK), BF16), jax.ShapeDtypeStruct((nseq, DK), BF16),
                   jax.ShapeDtypeStruct((nseq, DV), BF16), jax.ShapeDtypeStruct((nseq, DV), BF16),
                   jax.ShapeDtypeStruct((nseq, DK), F32), jax.ShapeDtypeStruct((nseq, DV), F32)],
        scratch_shapes=[pltpu.VMEM((HEADS, HV, HK), F32)],
        compiler_params=_params(),
        name="gla_fwd",
    )(x2d, ctx2d, mods[0], vec(norm_mix_g[0]), win, gw1, gw2, gb)

    bwd_lat = lambda i: (jnp.where(i == 0, NT - 1, NT - i), 0)
    seq_b = lambda i: (jnp.where(i == 0, 0, NT + 1 - i), 0)
    x_mid, ctx_mid = pl.pallas_call(
        _gla_bwd_kernel,
        grid=(NT + 1,),
        in_specs=[tile(D, bwd_lat), _const((CTX, D)), mod_spec(),
                  tile(DK, seq_b), tile(DK, seq_b), tile(DV, seq_b), tile(DV, seq_b),
                  tile(DK, seq_b), tile(DV, seq_b),
                  _const((1, HV)), _const((DV, D))] + ffn_specs(0),
        out_specs=[tile(D, bwd_lat), pl.BlockSpec((CTX, D), lambda i: (0, 0))],
        out_shape=[jax.ShapeDtypeStruct((SEQ, D), F32), jax.ShapeDtypeStruct((CTX, D), F32)],
        scratch_shapes=[pltpu.VMEM((HEADS, HV, HK), F32), pltpu.VMEM((TM, DV), F32)],
        compiler_params=_params(),
        name="gla_bwd",
    )(x2d, ctx2d, mods[0], q, k, v, g, la1, o_f,
      vec(gla_head_norm_g[0]), gla_w_out[0].astype(BF16), vec(norm_ffn_g[0]), w1b, w3b, w2b)

    y_br, u_lat, u_ctx = pl.pallas_call(
        _lru_in_kernel,
        grid=(NT + 1,),
        in_specs=[tile(D, fwd_lat), _const((CTX, D)), mod_spec(), _const((1, D)), _const((D, 2 * D))],
        out_specs=[tile(D, fwd_lat), tile(D, fwd_lat), pl.BlockSpec((CTX, D), lambda i: (0, 0))],
        out_shape=[jax.ShapeDtypeStruct((SEQ, D), BF16), jax.ShapeDtypeStruct((SEQ, D), F32),
                   jax.ShapeDtypeStruct((CTX, D), F32)],
        compiler_params=_params(),
        name="lru_in",
    )(x_mid, ctx_mid, mods[1], vec(norm_mix_g[1]), lru_w_in[0].astype(BF16))

    u3 = u_lat.reshape(ROWS, GRID_W, D)
    uctx3 = jnp.pad(u_ctx[:, None, :], ((0, 0), (0, SUB - 1), (0, 0)))
    grp = lambda wg: jnp.maximum(wg - 1, 0)
    blk = (ROWS, SUB, LRU_BLOCK)
    gate_w = lambda d: jnp.concatenate([lru_gate_a_w[0, d], lru_gate_x_w[0, d]], axis=-1).astype(BF16)
    gate_b = lambda d: jnp.stack([lru_gate_a_b[0, d], lru_gate_x_b[0, d]])
    chan = lambda rows: pl.BlockSpec((rows, LRU_BLOCK), lambda cb, wg: (0, cb))
    gate_spec = pl.BlockSpec((None, LRU_BLOCK, 2 * LRU_BLOCK), lambda cb, wg: (cb, 0, 0))
    last2 = ROWS // 2 - 1
    s0, pf, pb, sums = pl.pallas_call(
        _lru_scan_kernel,
        grid=(LRU_BLOCKS, NGRP + 1),
        in_specs=[pl.BlockSpec(blk, lambda cb, wg: (0, grp(wg), cb)),
                  pl.BlockSpec(blk, lambda cb, wg: (0, 0, cb)),
                  pl.BlockSpec((2, SUB, LRU_BLOCK), lambda cb, wg: (last2, jnp.maximum(wg - 2, 0), cb)),
                  pl.BlockSpec((2, SUB, LRU_BLOCK), lambda cb, wg: (0, jnp.minimum(wg, NGRP - 1), cb)),
                  chan(CONV_W), chan(1), gate_spec, gate_spec, chan(2), chan(2), chan(2)],
        out_specs=[pl.BlockSpec(blk, lambda cb, wg: (0, grp(wg), cb)),
                   pl.BlockSpec(blk, lambda cb, wg: (0, grp(wg), cb)),
                   pl.BlockSpec(blk, lambda cb, wg: (0, grp(wg), cb)),
                   pl.BlockSpec((4, SUB, LRU_BLOCK), lambda cb, wg: (0, wg, cb))],
        out_shape=[jax.ShapeDtypeStruct((ROWS, GRID_W, D), F32)] * 3
                  + [jax.ShapeDtypeStruct((4, (NGRP + 1) * SUB, D), F32)],
        scratch_shapes=[pltpu.VMEM((ROWS + CONV_W - 1, SUB, LRU_BLOCK), F32)]
                       + [pltpu.VMEM(blk, F32)] * 4,
        compiler_params=_params(2),
        name="lru_scan",
    )(u3, uctx3, u3, u3, lru_conv_w[0], vec(lru_conv_b[0]), gate_w(0), gate_w(1),
      gate_b(0), gate_b(1), lru_lambda[0])

    h_f, h_b = pl.pallas_call(
        _lru_chain_kernel,
        out_shape=[jax.ShapeDtypeStruct((GRID_W, D), F32)] * 2,
        name="lru_chain",
    )(sums)

    reps = TM // GRID_W
    lat = lambda i: (i, 0)
    out = pl.pallas_call(
        _lru_out_kernel,
        grid=(NT,),
        in_specs=[tile(D, lat), _const((None, SUB, D), (0, 0, 0)),
                  tile(D, lat), tile(D, lat), tile(D, lat), tile(D, lat),
                  _const((TM, D)), _const((TM, D)), _const((D, D))] + ffn_specs(1) + [_const((1, D))],
        out_specs=tile(D, lat),
        out_shape=jax.ShapeDtypeStruct((SEQ, D), F32),
        compiler_params=_params(),
        name="lru_out",
    )(x_mid, mods[1], y_br, s0.reshape(SEQ, D), pf.reshape(SEQ, D), pb.reshape(SEQ, D),
      jnp.tile(h_f, (reps, 1)), jnp.tile(h_b, (reps, 1)),
      lru_w_out[0].astype(BF16), vec(norm_ffn_g[1]), w1b, w3b, w2b, vec(final_norm_g))
    return out.reshape(1, SEQ, D)
```

```python
import jax
import jax.numpy as jnp
from jax import lax
from jax.experimental import pallas as pl
from jax.experimental.pallas import tpu as pltpu

D = 1024
SEQ = 16384
GRID_W = 64
ROWS = SEQ // GRID_W
CTX = 256
TM = 256
NT = SEQ // TM
HEADS = 4
DK = 512
DV = 1024
HK = DK // HEADS
HV = DV // HEADS
RANK = 16
TAU = 16.0
CHUNK = 64
NCH = TM // CHUNK
LRU_BLOCKS = 4
LRU_BLOCK = D // LRU_BLOCKS
LRU_C = 8.0
FFN = 2816
EPS = 1e-6
SUB = 8
CONV_W = 4
CONV_LEFT = 2
NGRP = GRID_W // SUB

F32 = jnp.float32
BF16 = jnp.bfloat16

V7X_VMEM_LIMIT = 56 * 1024 * 1024
LOG2E = 1.4426950408889634
TINY = 1e-30


def _dot(a, b):
    return jnp.dot(a, b, preferred_element_type=F32)


def _dot_nt(a, b):
    return lax.dot_general(a, b, (((1,), (1,)), ((), ())), preferred_element_type=F32)


def _dot_tn(a, b):
    return lax.dot_general(a, b, (((0,), (0,)), ((), ())), preferred_element_type=F32)


def _rms(x, g):
    ms = jnp.mean(x * x, axis=-1, keepdims=True)
    return x * lax.rsqrt(ms + EPS) * g


def _rms_mod(x, g, scale, shift):
    return _rms(x, g) * (1.0 + scale) + shift


def _silu(x):
    return x * jax.nn.sigmoid(x)


def _split_bf16(x):
    hi = x.astype(BF16)
    lo = (x - hi.astype(F32)).astype(BF16)
    return hi, lo


def _mod_kernel(c_ref, w_ref, b_ref, o_ref):
    s = _silu(c_ref[...])
    o_ref[...] = jnp.dot(s, w_ref[...], preferred_element_type=F32,
                         precision=lax.Precision.HIGHEST) + b_ref[...]


def _modulation(cvec, w_mod, b_mod):
    depth = w_mod.shape[0]
    tn = 1536
    return pl.pallas_call(
        _mod_kernel,
        grid=(depth, 6 * D // tn),
        in_specs=[
            pl.BlockSpec((SUB, D), lambda l, n: (0, 0)),
            pl.BlockSpec((None, D, tn), lambda l, n: (l, 0, n)),
            pl.BlockSpec((None, 1, tn), lambda l, n: (l, 0, n)),
        ],
        out_specs=pl.BlockSpec((None, SUB, tn), lambda l, n: (l, 0, n)),
        out_shape=jax.ShapeDtypeStruct((depth, SUB, 6 * D), F32),
        compiler_params=pltpu.CompilerParams(
            dimension_semantics=("arbitrary", "arbitrary"),
            vmem_limit_bytes=V7X_VMEM_LIMIT),
        name="mod",
    )(cvec, w_mod, b_mod.reshape(depth, 1, 6 * D))


def _gla_tile(q, k, v, la, s_ref, o_ref, rev):
    row = lax.broadcasted_iota(jnp.int32, (TM, TM), 0)
    col = lax.broadcasted_iota(jnp.int32, (TM, TM), 1)
    shift = CHUNK.bit_length() - 1
    same = jnp.right_shift(row, shift) == jnp.right_shift(col, shift)
    tri = same & ((col >= row) if rev else (col <= row))
    tri_b = tri.astype(BF16)
    same_b = same.astype(BF16)
    la_hi, la_lo = _split_bf16(la)
    b = _dot(tri_b, la_hi) + _dot(tri_b, la_lo)
    bl = _dot(same_b, la_hi) + _dot(same_b, la_lo)
    q_dec = (q * jnp.exp(b)).astype(BF16)
    k_inv = (k * jnp.exp(-b)).astype(BF16)
    k_end = (k * jnp.exp(bl - b)).astype(BF16)
    decay = jnp.exp(bl)
    order = range(NCH - 1, -1, -1) if rev else range(NCH)
    for h in range(HEADS):
        ks = slice(h * HK, (h + 1) * HK)
        vs = slice(h * HV, (h + 1) * HV)
        qd = q_dec[:, ks]
        vh = v[:, vs]
        att = jnp.where(tri, _dot_nt(qd, k_inv[:, ks]), 0.0).astype(BF16)
        o_intra = _dot(att, vh)
        st = s_ref[h]
        for c in order:
            rs = slice(c * CHUNK, (c + 1) * CHUNK)
            o_inter = _dot_nt(qd[rs], st.astype(BF16))
            o_ref[rs, vs] = o_intra[rs] + o_inter
            u_t = _dot_tn(vh[rs], k_end[rs, ks])
            st = decay[c * CHUNK:c * CHUNK + 1, ks] * st + u_t
        s_ref[h] = st


def _gla_fwd_kernel(x_ref, ctx_ref, mod_ref, gmix_ref, win_ref, gw1_ref, gw2_ref, gb_ref,
                    q_ref, k_ref, v_ref, g_ref, la1_ref, of_ref, s_ref):
    i = pl.program_id(0)

    @pl.when(i == 0)
    def _():
        s_ref[...] = jnp.zeros_like(s_ref)

    xt = jnp.where(i == 0, ctx_ref[...], x_ref[...])
    mod = mod_ref[...]
    a = _rms_mod(xt, gmix_ref[...], mod[1:2], mod[0:1]).astype(BF16)
    z = _dot(a, win_ref[...])
    q = z[:, :DK] * (HK ** -0.5)
    k = z[:, DK:2 * DK]
    v = z[:, 2 * DK:2 * DK + DV].astype(BF16)
    t = _dot(a, gw1_ref[...]).astype(BF16)
    zg = _dot(t, gw2_ref[...]) + gb_ref[...]
    la = jax.nn.log_sigmoid(zg) * (1.0 / TAU)
    q_ref[...] = q.astype(BF16)
    k_ref[...] = k.astype(BF16)
    v_ref[...] = v
    g_ref[...] = z[:, 2 * DK + DV:].astype(BF16)
    la1_ref[...] = la[:, DK:]
    _gla_tile(q, k, v, la[:, :DK], s_ref, of_ref, rev=False)


def _ffn(x1, mod, gffn, w1_ref, w3_ref, w2_ref):
    f = _rms_mod(x1, gffn, mod[4:5], mod[3:4]).astype(BF16)
    p = (_silu(_dot(f, w1_ref[...])) * _dot(f, w3_ref[...])).astype(BF16)
    return x1 + mod[5:6] * _dot(p, w2_ref[...])


def _gla_bwd_kernel(x_ref, ctx_ref, mod_ref, q_ref, k_ref, v_ref, g_ref, la1_ref, of_ref,
                    hn_ref, wout_ref, gffn_ref, w1_ref, w3_ref, w2_ref,
                    xo_ref, co_ref, s_ref, ob_ref):
    i = pl.program_id(0)

    @pl.when(i == 0)
    def _():
        s_ref[...] = jnp.zeros_like(s_ref)

    _gla_tile(q_ref[...].astype(F32), k_ref[...].astype(F32), v_ref[...], la1_ref[...],
              s_ref, ob_ref, rev=True)
    o = of_ref[...] + ob_ref[...]
    hn = hn_ref[...]
    parts = [_rms(o[:, h * HV:(h + 1) * HV], hn) for h in range(HEADS)]
    on = jnp.concatenate(parts, axis=-1)
    gated = (on * _silu(g_ref[...].astype(F32))).astype(BF16)
    y = _dot(gated, wout_ref[...])
    mod = mod_ref[...]
    xt = jnp.where(i == 0, ctx_ref[...], x_ref[...])
    x1 = xt + mod[2:3] * y
    x2 = _ffn(x1, mod, gffn_ref[...], w1_ref, w3_ref, w2_ref)

    @pl.when(i == 0)
    def _():
        co_ref[...] = x2

    @pl.when(i > 0)
    def _():
        xo_ref[...] = x2


def _lru_in_kernel(x_ref, ctx_ref, mod_ref, gmix_ref, win_ref, y_ref, u_ref, uc_ref):
    i = pl.program_id(0)
    xt = jnp.where(i == 0, ctx_ref[...], x_ref[...])
    mod = mod_ref[...]
    a = _rms_mod(xt, gmix_ref[...], mod[1:2], mod[0:1]).astype(BF16)
    z = _dot(a, win_ref[...])
    y_ref[...] = jax.nn.gelu(z[:, :D]).astype(BF16)
    u_ref[...] = z[:, D:]

    @pl.when(i == 0)
    def _():
        uc_ref[...] = z[:, D:]


def _lru_coeffs(ub, uch, wg_ref, bg_ref, lam_row):
    zz = _dot(ub, wg_ref[...])
    bg = bg_ref[...]
    ta = jnp.tanh(zz[:, :LRU_BLOCK] + bg[0:1])
    tx = jnp.tanh(zz[:, LRU_BLOCK:] + bg[1:2])
    c2 = (-0.5 * LRU_C * LOG2E) * jax.nn.softplus(-lam_row)
    a = jnp.exp2(c2 * ta + c2)
    v = 1.0 - a * a
    s = v * lax.rsqrt(jnp.maximum(v, TINY))
    b = s * ((tx + 1.0) * uch)
    shape = (ROWS, SUB, LRU_BLOCK)
    return a.reshape(shape), b.reshape(shape)


def _lru_scan_kernel(u_ref, uctx_ref, hp_ref, hn_ref, cw_ref, cb_ref, wg0_ref, wg1_ref,
                     bg0_ref, bg1_ref, lam_ref,
                     s0_ref, pf_ref, pb_ref, sum_ref,
                     e_ref):
    wg = pl.program_id(1)
    u = jnp.where(wg == 0, uctx_ref[...], u_ref[...])
    prev_ok = wg >= 2
    next_ok = (wg >= 1) & (wg <= NGRP - 1)
    sub = lax.broadcasted_iota(jnp.int32, (SUB, LRU_BLOCK), 0)
    hp = hp_ref[...]
    hn = hn_ref[...]

    def from_prev_column(cur, halo):
        edge = jnp.where(prev_ok, pltpu.roll(halo, 1, 0), 0.0)
        return jnp.where(sub == 0, edge, pltpu.roll(cur, 1, 0))

    e_ref[CONV_LEFT:CONV_LEFT + ROWS] = u
    e_ref[0] = from_prev_column(u[ROWS - 2], hp[0])
    e_ref[1] = from_prev_column(u[ROWS - 1], hp[1])
    edge = jnp.where(next_ok, pltpu.roll(hn[0], SUB - 1, 0), 0.0)
    e_ref[CONV_LEFT + ROWS] = jnp.where(sub == SUB - 1, edge, pltpu.roll(u[0], SUB - 1, 0))
    cw = cw_ref[...]
    uc = cb_ref[...] + cw[0:1] * e_ref[0:ROWS]
    for j in range(1, CONV_W):
        uc = uc + cw[j:j + 1] * e_ref[j:j + ROWS]
    uc2 = uc.reshape(ROWS * SUB, LRU_BLOCK)
    ub = uc2.astype(BF16)
    lam = lam_ref[...]
    uch = 0.5 * uc2
    a0, b0 = _lru_coeffs(ub, uch, wg0_ref, bg0_ref, lam[0:1])
    a1, b1 = _lru_coeffs(ub, uch, wg1_ref, bg1_ref, lam[1:2])

    h = jnp.zeros((SUB, LRU_BLOCK), F32)
    p = jnp.ones((SUB, LRU_BLOCK), F32)
    for r in range(ROWS):
        h = a0[r] * h + b0[r]
        p = a0[r] * p
        s0_ref[r] = h
        pf_ref[r] = p
    sum_ref[0] = p
    sum_ref[1] = h
    h = jnp.zeros((SUB, LRU_BLOCK), F32)
    p = jnp.ones((SUB, LRU_BLOCK), F32)
    for r in range(ROWS - 1, -1, -1):
        h = a1[r] * h + b1[r]
        p = a1[r] * p
        s0_ref[r] = s0_ref[r] + h
        pb_ref[r] = p
    sum_ref[2] = p
    sum_ref[3] = h


def _lru_chain_kernel(sum_ref, hf_ref, hb_ref):
    s = sum_ref[1, 0:1, :]
    for w in range(GRID_W):
        hf_ref[w:w + 1, :] = s
        s = sum_ref[0, SUB + w:SUB + w + 1, :] * s + sum_ref[1, SUB + w:SUB + w + 1, :]
    s = sum_ref[3, 0:1, :]
    for w in range(GRID_W - 1, -1, -1):
        hb_ref[w:w + 1, :] = s
        s = sum_ref[2, SUB + w:SUB + w + 1, :] * s + sum_ref[3, SUB + w:SUB + w + 1, :]


def _lru_out_kernel(x_ref, mod_ref, y_ref, s0_ref, pf_ref, pb_ref, hf_ref, hb_ref,
                    wout_ref, gffn_ref, w1_ref, w3_ref, w2_ref, gfin_ref, o_ref):
    hs = s0_ref[...] + pf_ref[...] * hf_ref[...] + pb_ref[...] * hb_ref[...]
    ym = (hs * y_ref[...].astype(F32)).astype(BF16)
    mod = mod_ref[...]
    x1 = x_ref[...] + mod[2:3] * _dot(ym, wout_ref[...])
    x2 = _ffn(x1, mod, gffn_ref[...], w1_ref, w3_ref, w2_ref)
    o_ref[...] = _rms(x2, gfin_ref[...])


def _const(shape, index=None):
    nd = len(shape)
    index = (0,) * nd if index is None else index
    return pl.BlockSpec(shape, lambda *_: index, pipeline_mode=pl.Buffered(1))


def _params(n_axes=1):
    return pltpu.CompilerParams(dimension_semantics=("arbitrary",) * n_axes,
                                vmem_limit_bytes=V7X_VMEM_LIMIT)


def kernel(x, c, ctx, c_ctx, norm_mix_g, norm_ffn_g, w_mod, b_mod, gla_w_in, gla_gate_w1, gla_gate_w2, gla_gate_b, gla_head_norm_g, gla_w_out, lru_w_in, lru_conv_w, lru_conv_b, lru_gate_a_w, lru_gate_a_b, lru_gate_x_w, lru_gate_x_b, lru_lambda, lru_w_out, ffn_w1, ffn_w3, ffn_w2, final_norm_g):
    assert x.shape == (1, SEQ, D) and ctx.shape == (1, CTX, D) and CTX == TM == ROWS
    x2d = x.reshape(SEQ, D)
    ctx2d = ctx.reshape(CTX, D)
    nseq = (NT + 1) * TM

    cvec = jnp.zeros((SUB, D), F32).at[0].set(c[0]).at[1].set(c_ctx)
    m = _modulation(cvec, w_mod, b_mod)
    mods = m[:, :2].reshape(2, 2, 6, D)
    mods = jnp.pad(mods, ((0, 0), (0, 0), (0, SUB - 6), (0, 0)))

    def mod_spec():
        return pl.BlockSpec((None, SUB, D), lambda i: (jnp.where(i == 0, 1, 0), 0, 0))

    vec = lambda t: t.reshape(1, -1)
    tile = lambda w, f: pl.BlockSpec((TM, w), f)
    w1b, w3b, w2b = ffn_w1.astype(BF16), ffn_w3.astype(BF16), ffn_w2.astype(BF16)

    def ffn_specs(layer):
        return [_const((1, D)), _const((None, D, FFN), (layer, 0, 0)),
                _const((None, D, FFN), (layer, 0, 0)), _const((None, FFN, D), (layer, 0, 0))]

    win = gla_w_in[0].astype(BF16)
    gw1 = jnp.concatenate([gla_gate_w1[0, 0], gla_gate_w1[0, 1]], axis=1)
    gw1 = jnp.pad(gw1, ((0, 0), (0, 128 - 2 * RANK))).astype(BF16)
    gw2 = jnp.zeros((128, 2 * DK), F32)
    gw2 = gw2.at[:RANK, :DK].set(gla_gate_w2[0, 0]).at[RANK:2 * RANK, DK:].set(gla_gate_w2[0, 1])
    gw2 = gw2.astype(BF16)
    gb = gla_gate_b[0].reshape(1, 2 * DK)

    fwd_lat = lambda i: (jnp.maximum(i - 1, 0), 0)
    seq_f = lambda i: (i, 0)
    q, k, v, g, la1, o_f = pl.pallas_call(
        _gla_fwd_kernel,
        grid=(NT + 1,),
        in_specs=[tile(D, fwd_lat), _const((CTX, D)), mod_spec(), _const((1, D)),
                  _const((D, 2 * DK + 2 * DV)), _const((D, 128)), _const((128, 2 * DK)),
                  _const((1, 2 * DK))],
        out_specs=[tile(DK, seq_f), tile(DK, seq_f), tile(DV, seq_f), tile(DV, seq_f),
                   tile(DK, seq_f), tile(DV, seq_f)],
        out_shape=[jax.ShapeDtypeStruct((nseq, DK), BF16), jax.ShapeDtypeStruct((nseq, DK), BF16),
                   jax.ShapeDtypeStruct((nseq, DV), BF16), jax.ShapeDtypeStruct((nseq, DV), BF16),
                   jax.ShapeDtypeStruct((nseq, DK), F32), jax.ShapeDtypeStruct((nseq, DV), F32)],
        scratch_shapes=[pltpu.VMEM((HEADS, HV, HK), F32)],
        compiler_params=_params(),
        name="gla_fwd",
    )(x2d, ctx2d, mods[0], vec(norm_mix_g[0]), win, gw1, gw2, gb)

    bwd_lat = lambda i: (jnp.where(i == 0, NT - 1, NT - i), 0)
    seq_b = lambda i: (jnp.where(i == 0, 0, NT + 1 - i), 0)
    x_mid, ctx_mid = pl.pallas_call(
        _gla_bwd_kernel,
        grid=(NT + 1,),
        in_specs=[tile(D, bwd_lat), _const((CTX, D)), mod_spec(),
                  tile(DK, seq_b), tile(DK, seq_b), tile(DV, seq_b), tile(DV, seq_b),
                  tile(DK, seq_b), tile(DV, seq_b),
                  _const((1, HV)), _const((DV, D))] + ffn_specs(0),
        out_specs=[tile(D, bwd_lat), pl.BlockSpec((CTX, D), lambda i: (0, 0))],
        out_shape=[jax.ShapeDtypeStruct((SEQ, D), F32), jax.ShapeDtypeStruct((CTX, D), F32)],
        scratch_shapes=[pltpu.VMEM((HEADS, HV, HK), F32), pltpu.VMEM((TM, DV), F32)],
        compiler_params=_params(),
        name="gla_bwd",
    )(x2d, ctx2d, mods[0], q, k, v, g, la1, o_f,
      vec(gla_head_norm_g[0]), gla_w_out[0].astype(BF16), vec(norm_ffn_g[0]), w1b, w3b, w2b)

    y_br, u_lat, u_ctx = pl.pallas_call(
        _lru_in_kernel,
        grid=(NT + 1,),
        in_specs=[tile(D, fwd_lat), _const((CTX, D)), mod_spec(), _const((1, D)), _const((D, 2 * D))],
        out_specs=[tile(D, fwd_lat), tile(D, fwd_lat), pl.BlockSpec((CTX, D), lambda i: (0, 0))],
        out_shape=[jax.ShapeDtypeStruct((SEQ, D), BF16), jax.ShapeDtypeStruct((SEQ, D), F32),
                   jax.ShapeDtypeStruct((CTX, D), F32)],
        compiler_params=_params(),
        name="lru_in",
    )(x_mid, ctx_mid, mods[1], vec(norm_mix_g[1]), lru_w_in[0].astype(BF16))

    u3 = u_lat.reshape(ROWS, GRID_W, D)
    uctx3 = jnp.pad(u_ctx[:, None, :], ((0, 0), (0, SUB - 1), (0, 0)))
    grp = lambda wg: jnp.maximum(wg - 1, 0)
    blk = (ROWS, SUB, LRU_BLOCK)
    gate_w = lambda d: (0.5 * jnp.concatenate([lru_gate_a_w[0, d], lru_gate_x_w[0, d]], axis=-1)).astype(BF16)
    gate_b = lambda d: 0.5 * jnp.stack([lru_gate_a_b[0, d], lru_gate_x_b[0, d]])
    chan = lambda rows: pl.BlockSpec((rows, LRU_BLOCK), lambda cb, wg: (0, cb))
    gate_spec = pl.BlockSpec((None, LRU_BLOCK, 2 * LRU_BLOCK), lambda cb, wg: (cb, 0, 0))
    last2 = ROWS // 2 - 1
    s0, pf, pb, sums = pl.pallas_call(
        _lru_scan_kernel,
        grid=(LRU_BLOCKS, NGRP + 1),
        in_specs=[pl.BlockSpec(blk, lambda cb, wg: (0, grp(wg), cb)),
                  pl.BlockSpec(blk, lambda cb, wg: (0, 0, cb)),
                  pl.BlockSpec((2, SUB, LRU_BLOCK), lambda cb, wg: (last2, jnp.maximum(wg - 2, 0), cb)),
                  pl.BlockSpec((2, SUB, LRU_BLOCK), lambda cb, wg: (0, jnp.minimum(wg, NGRP - 1), cb)),
                  chan(CONV_W), chan(1), gate_spec, gate_spec, chan(2), chan(2), chan(2)],
        out_specs=[pl.BlockSpec(blk, lambda cb, wg: (0, grp(wg), cb)),
                   pl.BlockSpec(blk, lambda cb, wg: (0, grp(wg), cb)),
                   pl.BlockSpec(blk, lambda cb, wg: (0, grp(wg), cb)),
                   pl.BlockSpec((4, SUB, LRU_BLOCK), lambda cb, wg: (0, wg, cb))],
        out_shape=[jax.ShapeDtypeStruct((ROWS, GRID_W, D), F32)] * 3
                  + [jax.ShapeDtypeStruct((4, (NGRP + 1) * SUB, D), F32)],
        scratch_shapes=[pltpu.VMEM((ROWS + CONV_W - 1, SUB, LRU_BLOCK), F32)],
        compiler_params=_params(2),
        name="lru_scan",
    )(u3, uctx3, u3, u3, lru_conv_w[0], vec(lru_conv_b[0]), gate_w(0), gate_w(1),
      gate_b(0), gate_b(1), lru_lambda[0])

    h_f, h_b = pl.pallas_call(
        _lru_chain_kernel,
        out_shape=[jax.ShapeDtypeStruct((GRID_W, D), F32)] * 2,
        name="lru_chain",
    )(sums)

    reps = TM // GRID_W
    lat = lambda i: (i, 0)
    out = pl.pallas_call(
        _lru_out_kernel,
        grid=(NT,),
        in_specs=[tile(D, lat), _const((None, SUB, D), (0, 0, 0)),
                  tile(D, lat), tile(D, lat), tile(D, lat), tile(D, lat),
                  _const((TM, D)), _const((TM, D)), _const((D, D))] + ffn_specs(1) + [_const((1, D))],
        out_specs=tile(D, lat),
        out_shape=jax.ShapeDtypeStruct((SEQ, D), F32),
        compiler_params=_params(),
        name="lru_out",
    )(x_mid, mods[1], y_br, s0.reshape(SEQ, D), pf.reshape(SEQ, D), pb.reshape(SEQ, D),
      jnp.tile(h_f, (reps, 1)), jnp.tile(h_b, (reps, 1)),
      lru_w_out[0].astype(BF16), vec(norm_ffn_g[1]), w1b, w3b, w2b, vec(final_norm_g))
    return out.reshape(1, SEQ, D)
```

```python
import jax
import jax.numpy as jnp
from jax import lax
from jax.experimental import pallas as pl
from jax.experimental.pallas import tpu as pltpu

D = 1024
SEQ = 16384
GRID_W = 64
ROWS = SEQ // GRID_W
CTX = 256
TM = 256
NT = SEQ // TM
HEADS = 4
DK = 512
DV = 1024
HK = DK // HEADS
HV = DV // HEADS
RANK = 16
TAU = 16.0
CHUNK = 64
NCH = TM // CHUNK
LRU_BLOCKS = 4
LRU_BLOCK = D // LRU_BLOCKS
LRU_C = 8.0
FFN = 2816
EPS = 1e-6
SUB = 8
CONV_W = 4
CONV_LEFT = 2
NGRP = GRID_W // SUB
OUT_TM = 256
OUT_SUBTILES = 2

F32 = jnp.float32
BF16 = jnp.bfloat16

V7X_VMEM_LIMIT = 56 * 1024 * 1024
LOG2E = 1.4426950408889634
TINY = 1e-30


def _dot(a, b):
    return jnp.dot(a, b, preferred_element_type=F32)


def _dot_nt(a, b):
    return lax.dot_general(a, b, (((1,), (1,)), ((), ())), preferred_element_type=F32)


def _dot_tn(a, b):
    return lax.dot_general(a, b, (((0,), (0,)), ((), ())), preferred_element_type=F32)


def _rms(x, g):
    ms = jnp.mean(x * x, axis=-1, keepdims=True)
    return x * lax.rsqrt(ms + EPS) * g


def _rms_mod(x, g, scale, shift):
    return _rms(x, g) * (1.0 + scale) + shift


def _silu(x):
    return x * jax.nn.sigmoid(x)


def _split_bf16(x):
    hi = x.astype(BF16)
    lo = (x - hi.astype(F32)).astype(BF16)
    return hi, lo


def _mod_kernel(c_ref, w_ref, b_ref, o_ref):
    s = _silu(c_ref[...])
    o_ref[...] = jnp.dot(s, w_ref[...], preferred_element_type=F32,
                         precision=lax.Precision.HIGHEST) + b_ref[...]


def _modulation(cvec, w_mod, b_mod):
    depth = w_mod.shape[0]
    tn = 1536
    return pl.pallas_call(
        _mod_kernel,
        grid=(depth, 6 * D // tn),
        in_specs=[
            pl.BlockSpec((SUB, D), lambda l, n: (0, 0)),
            pl.BlockSpec((None, D, tn), lambda l, n: (l, 0, n)),
            pl.BlockSpec((None, 1, tn), lambda l, n: (l, 0, n)),
        ],
        out_specs=pl.BlockSpec((None, SUB, tn), lambda l, n: (l, 0, n)),
        out_shape=jax.ShapeDtypeStruct((depth, SUB, 6 * D), F32),
        compiler_params=pltpu.CompilerParams(
            dimension_semantics=("arbitrary", "arbitrary"),
            vmem_limit_bytes=V7X_VMEM_LIMIT),
        name="mod",
    )(cvec, w_mod, b_mod.reshape(depth, 1, 6 * D))


def _gla_tile(q, k, v, la, s_ref, o_ref, rev):
    row = lax.broadcasted_iota(jnp.int32, (TM, TM), 0)
    col = lax.broadcasted_iota(jnp.int32, (TM, TM), 1)
    shift = CHUNK.bit_length() - 1
    same = jnp.right_shift(row, shift) == jnp.right_shift(col, shift)
    tri = same & ((col >= row) if rev else (col <= row))
    tri_b = tri.astype(BF16)
    la_hi, la_lo = _split_bf16(la)
    b = _dot(tri_b, la_hi) + _dot(tri_b, la_lo)
    ends = [c * CHUNK if rev else (c + 1) * CHUNK - 1 for c in range(NCH)]
    bl = jnp.concatenate([jnp.broadcast_to(b[r:r + 1], (CHUNK, DK)) for r in ends], axis=0)
    q_dec = (q * jnp.exp(b)).astype(BF16)
    k_inv = (k * jnp.exp(-b)).astype(BF16)
    k_end = (k * jnp.exp(bl - b)).astype(BF16)
    decay = jnp.exp(bl)
    order = range(NCH - 1, -1, -1) if rev else range(NCH)
    for h in range(HEADS):
        ks = slice(h * HK, (h + 1) * HK)
        vs = slice(h * HV, (h + 1) * HV)
        qd = q_dec[:, ks]
        vh = v[:, vs]
        att = jnp.where(tri, _dot_nt(qd, k_inv[:, ks]), 0.0).astype(BF16)
        o_intra = _dot(att, vh)
        st = s_ref[h]
        for c in order:
            rs = slice(c * CHUNK, (c + 1) * CHUNK)
            o_inter = _dot_nt(qd[rs], st.astype(BF16))
            o_ref[rs, vs] = o_intra[rs] + o_inter
            u_t = _dot_tn(vh[rs], k_end[rs, ks])
            st = decay[c * CHUNK:c * CHUNK + 1, ks] * st + u_t
        s_ref[h] = st


def _gla_proj_kernel(x_ref, ctx_ref, mod_ref, gmix_ref, win_ref, gw1_ref, gw2_ref, gb_ref,
                     q_ref, k_ref, v_ref, g_ref, la_ref):
    i = pl.program_id(0)
    xt = jnp.where(i == 0, ctx_ref[...], x_ref[...])
    mod = mod_ref[...]
    a = _rms_mod(xt, gmix_ref[...], mod[1:2], mod[0:1]).astype(BF16)
    t = _dot(a, gw1_ref[...]).astype(BF16)
    zg = _dot(t, gw2_ref[...]) + gb_ref[...]
    la_ref[...] = (jnp.minimum(zg, 0.0) - jnp.log(1.0 + jnp.exp(-jnp.abs(zg)))) * (1.0 / TAU)
    z = _dot(a, win_ref[...])
    q_ref[...] = (z[:, :DK] * (HK ** -0.5)).astype(BF16)
    k_ref[...] = z[:, DK:2 * DK].astype(BF16)
    v_ref[...] = z[:, 2 * DK:2 * DK + DV].astype(BF16)
    g_ref[...] = z[:, 2 * DK + DV:].astype(BF16)


def _gla_scan_kernel(qf_ref, kf_ref, vf_ref, laf_ref, qb_ref, kb_ref, vb_ref, lab_ref,
                     of_ref, ob_ref, sf_ref, sb_ref):
    @pl.when(pl.program_id(0) == 0)
    def _():
        sf_ref[...] = jnp.zeros_like(sf_ref)
        sb_ref[...] = jnp.zeros_like(sb_ref)

    _gla_tile(qf_ref[...].astype(F32), kf_ref[...].astype(F32), vf_ref[...], laf_ref[...],
              sf_ref, of_ref, rev=False)
    _gla_tile(qb_ref[...].astype(F32), kb_ref[...].astype(F32), vb_ref[...], lab_ref[...],
              sb_ref, ob_ref, rev=True)


def _ffn(x1, mod, gffn, w1_ref, w3_ref, w2_ref):
    f = _rms_mod(x1, gffn, mod[4:5], mod[3:4]).astype(BF16)
    p = (_silu(_dot(f, w1_ref[...])) * _dot(f, w3_ref[...])).astype(BF16)
    return x1 + mod[5:6] * _dot(p, w2_ref[...])


def _gla_out_kernel(x_ref, ctx_ref, mod0_ref, mod1_ref, of_ref, ob_ref, g_ref,
                    hn_ref, wout_ref, gffn_ref, w1_ref, w3_ref, w2_ref, gmix1_ref, lwin_ref,
                    xo_ref, y_ref, u_ref, uc_ref):
    i = pl.program_id(0)
    o = of_ref[...] + ob_ref[...]
    hn = hn_ref[...]
    parts = [_rms(o[:, h * HV:(h + 1) * HV], hn) for h in range(HEADS)]
    on = jnp.concatenate(parts, axis=-1)
    gated = (on * _silu(g_ref[...].astype(F32))).astype(BF16)
    mod = mod0_ref[...]
    xt = jnp.where(i == 0, ctx_ref[...], x_ref[...])
    x1 = xt + mod[2:3] * _dot(gated, wout_ref[...])
    x2 = _ffn(x1, mod, gffn_ref[...], w1_ref, w3_ref, w2_ref)
    xo_ref[...] = x2
    mod = mod1_ref[...]
    a = _rms_mod(x2, gmix1_ref[...], mod[1:2], mod[0:1]).astype(BF16)
    z = _dot(a, lwin_ref[...])
    y_ref[...] = jax.nn.gelu(z[:, :D]).astype(BF16)
    u_ref[...] = z[:, D:]

    @pl.when(i == 0)
    def _():
        uc_ref[...] = z[:, D:]


def _lru_coeffs(ub, uch, wg_ref, bg_ref, lam_row):
    zz = _dot(ub, wg_ref[...])
    bg = bg_ref[...]
    ta = jnp.tanh(zz[:, :LRU_BLOCK] + bg[0:1])
    tx = jnp.tanh(zz[:, LRU_BLOCK:] + bg[1:2])
    c2 = (-0.5 * LRU_C * LOG2E) * jax.nn.softplus(-lam_row)
    a = jnp.exp2(c2 * ta + c2)
    v = 1.0 - a * a
    s = v * lax.rsqrt(jnp.maximum(v, TINY))
    b = s * ((tx + 1.0) * uch)
    shape = (ROWS, SUB, LRU_BLOCK)
    return a.reshape(shape), b.reshape(shape)


def _lru_scan_kernel(u_ref, uctx_ref, hp_ref, hn_ref, cw_ref, cb_ref, wg0_ref, wg1_ref,
                     bg0_ref, bg1_ref, lam_ref,
                     s0_ref, pf_ref, pb_ref, sum_ref,
                     e_ref):
    wg = pl.program_id(1)
    u = jnp.where(wg == 0, uctx_ref[...], u_ref[...])
    prev_ok = wg >= 2
    next_ok = (wg >= 1) & (wg <= NGRP - 1)
    sub = lax.broadcasted_iota(jnp.int32, (SUB, LRU_BLOCK), 0)
    hp = hp_ref[...]
    hn = hn_ref[...]

    def from_prev_column(cur, halo):
        edge = jnp.where(prev_ok, pltpu.roll(halo, 1, 0), 0.0)
        return jnp.where(sub == 0, edge, pltpu.roll(cur, 1, 0))

    e_ref[CONV_LEFT:CONV_LEFT + ROWS] = u
    e_ref[0] = from_prev_column(u[ROWS - 2], hp[0])
    e_ref[1] = from_prev_column(u[ROWS - 1], hp[1])
    edge = jnp.where(next_ok, pltpu.roll(hn[0], SUB - 1, 0), 0.0)
    e_ref[CONV_LEFT + ROWS] = jnp.where(sub == SUB - 1, edge, pltpu.roll(u[0], SUB - 1, 0))
    cw = cw_ref[...]
    uc = cb_ref[...] + cw[0:1] * e_ref[0:ROWS]
    for j in range(1, CONV_W):
        uc = uc + cw[j:j + 1] * e_ref[j:j + ROWS]
    uc2 = uc.reshape(ROWS * SUB, LRU_BLOCK)
    ub = uc2.astype(BF16)
    lam = lam_ref[...]
    uch = 0.5 * uc2
    a0, b0 = _lru_coeffs(ub, uch, wg0_ref, bg0_ref, lam[0:1])
    a1, b1 = _lru_coeffs(ub, uch, wg1_ref, bg1_ref, lam[1:2])

    h = jnp.zeros((SUB, LRU_BLOCK), F32)
    p = jnp.ones((SUB, LRU_BLOCK), F32)
    for r in range(ROWS):
        h = a0[r] * h + b0[r]
        p = a0[r] * p
        s0_ref[r] = h
        pf_ref[r] = p
    sum_ref[0] = p
    sum_ref[1] = h
    h = jnp.zeros((SUB, LRU_BLOCK), F32)
    p = jnp.ones((SUB, LRU_BLOCK), F32)
    for r in range(ROWS - 1, -1, -1):
        h = a1[r] * h + b1[r]
        p = a1[r] * p
        s0_ref[r] = s0_ref[r] + h
        pb_ref[r] = p
    sum_ref[2] = p
    sum_ref[3] = h


def _lru_chain_kernel(sum_ref, hf_ref, hb_ref):
    s = sum_ref[1, 0:1, :]
    for w in range(GRID_W):
        hf_ref[w:w + 1, :] = s
        s = sum_ref[0, SUB + w:SUB + w + 1, :] * s + sum_ref[1, SUB + w:SUB + w + 1, :]
    s = sum_ref[3, 0:1, :]
    for w in range(GRID_W - 1, -1, -1):
        hb_ref[w:w + 1, :] = s
        s = sum_ref[2, SUB + w:SUB + w + 1, :] * s + sum_ref[3, SUB + w:SUB + w + 1, :]


def _lru_out_kernel(x_ref, mod_ref, y_ref, s0_ref, pf_ref, pb_ref, hf_ref, hb_ref,
                    wout_ref, gffn_ref, w1_ref, w3_ref, w2_ref, gfin_ref, o_ref):
    mod = mod_ref[...]
    for j in range(OUT_SUBTILES):
        rs = slice(j * OUT_TM, (j + 1) * OUT_TM)
        hs = s0_ref[rs] + pf_ref[rs] * hf_ref[...] + pb_ref[rs] * hb_ref[...]
        ym = (hs * y_ref[rs].astype(F32)).astype(BF16)
        x1 = x_ref[rs] + mod[2:3] * _dot(ym, wout_ref[...])
        x2 = _ffn(x1, mod, gffn_ref[...], w1_ref, w3_ref, w2_ref)
        o_ref[rs] = _rms(x2, gfin_ref[...])


def _const(shape, index=None):
    nd = len(shape)
    index = (0,) * nd if index is None else index
    return pl.BlockSpec(shape, lambda *_: index, pipeline_mode=pl.Buffered(1))


def _params(n_axes=1):
    return pltpu.CompilerParams(dimension_semantics=("arbitrary",) * n_axes,
                                vmem_limit_bytes=V7X_VMEM_LIMIT)


def kernel(x, c, ctx, c_ctx, norm_mix_g, norm_ffn_g, w_mod, b_mod, gla_w_in, gla_gate_w1, gla_gate_w2, gla_gate_b, gla_head_norm_g, gla_w_out, lru_w_in, lru_conv_w, lru_conv_b, lru_gate_a_w, lru_gate_a_b, lru_gate_x_w, lru_gate_x_b, lru_lambda, lru_w_out, ffn_w1, ffn_w3, ffn_w2, final_norm_g):
    assert x.shape == (1, SEQ, D) and ctx.shape == (1, CTX, D) and CTX == TM == ROWS
    x2d = x.reshape(SEQ, D)
    ctx2d = ctx.reshape(CTX, D)
    nseq = (NT + 1) * TM

    cvec = jnp.zeros((SUB, D), F32).at[0].set(c[0]).at[1].set(c_ctx)
    m = _modulation(cvec, w_mod, b_mod)
    mods = m[:, :2].reshape(2, 2, 6, D)
    mods = jnp.pad(mods, ((0, 0), (0, 0), (0, SUB - 6), (0, 0)))

    def mod_spec(layer):
        return pl.BlockSpec((None, None, SUB, D), lambda i: (layer, jnp.where(i == 0, 1, 0), 0, 0))

    vec = lambda t: t.reshape(1, -1)
    tile = lambda w, f: pl.BlockSpec((TM, w), f)
    w1b, w3b, w2b = ffn_w1.astype(BF16), ffn_w3.astype(BF16), ffn_w2.astype(BF16)

    def ffn_specs(layer):
        return [_const((1, D)), _const((None, D, FFN), (layer, 0, 0)),
                _const((None, D, FFN), (layer, 0, 0)), _const((None, FFN, D), (layer, 0, 0))]

    win = gla_w_in[0].astype(BF16)
    gw1 = jnp.concatenate([gla_gate_w1[0, 0], gla_gate_w1[0, 1]], axis=1)
    gw1 = jnp.pad(gw1, ((0, 0), (0, 128 - 2 * RANK))).astype(BF16)
    gw2 = jnp.zeros((128, 2 * DK), F32)
    gw2 = gw2.at[:RANK, :DK].set(gla_gate_w2[0, 0]).at[RANK:2 * RANK, DK:].set(gla_gate_w2[0, 1])
    gw2 = gw2.astype(BF16)
    gb = gla_gate_b[0].reshape(1, 2 * DK)

    fwd_lat = lambda i: (jnp.maximum(i - 1, 0), 0)
    seq_f = lambda i: (i, 0)
    seq_b = lambda i: (jnp.where(i == 0, 0, NT + 1 - i), 0)
    q, k, v, g, la = pl.pallas_call(
        _gla_proj_kernel,
        grid=(NT + 1,),
        in_specs=[tile(D, fwd_lat), _const((CTX, D)), mod_spec(0), _const((1, D)),
                  _const((D, 2 * DK + 2 * DV)), _const((D, 128)), _const((128, 2 * DK)),
                  _const((1, 2 * DK))],
        out_specs=[tile(DK, seq_f), tile(DK, seq_f), tile(DV, seq_f), tile(DV, seq_f),
                   tile(2 * DK, seq_f)],
        out_shape=[jax.ShapeDtypeStruct((nseq, DK), BF16), jax.ShapeDtypeStruct((nseq, DK), BF16),
                   jax.ShapeDtypeStruct((nseq, DV), BF16), jax.ShapeDtypeStruct((nseq, DV), BF16),
                   jax.ShapeDtypeStruct((nseq, 2 * DK), F32)],
        compiler_params=_params(),
        name="gla_proj",
    )(x2d, ctx2d, mods, vec(norm_mix_g[0]), win, gw1, gw2, gb)

    la_b = lambda i: (seq_b(i)[0], 1)
    o_f, o_b = pl.pallas_call(
        _gla_scan_kernel,
        grid=(NT + 1,),
        in_specs=[tile(DK, seq_f), tile(DK, seq_f), tile(DV, seq_f), tile(DK, seq_f),
                  tile(DK, seq_b), tile(DK, seq_b), tile(DV, seq_b), tile(DK, la_b)],
        out_specs=[tile(DV, seq_f), tile(DV, seq_b)],
        out_shape=[jax.ShapeDtypeStruct((nseq, DV), F32)] * 2,
        scratch_shapes=[pltpu.VMEM((HEADS, HV, HK), F32)] * 2,
        compiler_params=_params(),
        name="gla_scan",
    )(q, k, v, la, q, k, v, la)

    x_mid, y_br, u_lat, u_ctx = pl.pallas_call(
        _gla_out_kernel,
        grid=(NT + 1,),
        in_specs=[tile(D, fwd_lat), _const((CTX, D)), mod_spec(0), mod_spec(1),
                  tile(DV, seq_f), tile(DV, seq_f), tile(DV, seq_f),
                  _const((1, HV)), _const((DV, D))] + ffn_specs(0) + [_const((1, D)), _const((D, 2 * D))],
        out_specs=[tile(D, fwd_lat), tile(D, fwd_lat), tile(D, fwd_lat),
                   pl.BlockSpec((CTX, D), lambda i: (0, 0))],
        out_shape=[jax.ShapeDtypeStruct((SEQ, D), F32), jax.ShapeDtypeStruct((SEQ, D), BF16),
                   jax.ShapeDtypeStruct((SEQ, D), F32), jax.ShapeDtypeStruct((CTX, D), F32)],
        compiler_params=_params(),
        name="gla_out",
    )(x2d, ctx2d, mods, mods, o_f, o_b, g,
      vec(gla_head_norm_g[0]), gla_w_out[0].astype(BF16), vec(norm_ffn_g[0]), w1b, w3b, w2b,
      vec(norm_mix_g[1]), lru_w_in[0].astype(BF16))

    u3 = u_lat.reshape(ROWS, GRID_W, D)
    uctx3 = jnp.pad(u_ctx[:, None, :], ((0, 0), (0, SUB - 1), (0, 0)))
    grp = lambda wg: jnp.maximum(wg - 1, 0)
    blk = (ROWS, SUB, LRU_BLOCK)
    gate_w = lambda d: (0.5 * jnp.concatenate([lru_gate_a_w[0, d], lru_gate_x_w[0, d]], axis=-1)).astype(BF16)
    gate_b = lambda d: 0.5 * jnp.stack([lru_gate_a_b[0, d], lru_gate_x_b[0, d]])
    chan = lambda rows: pl.BlockSpec((rows, LRU_BLOCK), lambda cb, wg: (0, cb))
    gate_spec = pl.BlockSpec((None, LRU_BLOCK, 2 * LRU_BLOCK), lambda cb, wg: (cb, 0, 0))
    last2 = ROWS // 2 - 1
    s0, pf, pb, sums = pl.pallas_call(
        _lru_scan_kernel,
        grid=(LRU_BLOCKS, NGRP + 1),
        in_specs=[pl.BlockSpec(blk, lambda cb, wg: (0, grp(wg), cb)),
                  pl.BlockSpec(blk, lambda cb, wg: (0, 0, cb)),
                  pl.BlockSpec((2, SUB, LRU_BLOCK), lambda cb, wg: (last2, jnp.maximum(wg - 2, 0), cb)),
                  pl.BlockSpec((2, SUB, LRU_BLOCK), lambda cb, wg: (0, jnp.minimum(wg, NGRP - 1), cb)),
                  chan(CONV_W), chan(1), gate_spec, gate_spec, chan(2), chan(2), chan(2)],
        out_specs=[pl.BlockSpec(blk, lambda cb, wg: (0, grp(wg), cb)),
                   pl.BlockSpec(blk, lambda cb, wg: (0, grp(wg), cb)),
                   pl.BlockSpec(blk, lambda cb, wg: (0, grp(wg), cb)),
                   pl.BlockSpec((4, SUB, LRU_BLOCK), lambda cb, wg: (0, wg, cb))],
        out_shape=[jax.ShapeDtypeStruct((ROWS, GRID_W, D), F32)] * 3
                  + [jax.ShapeDtypeStruct((4, (NGRP + 1) * SUB, D), F32)],
        scratch_shapes=[pltpu.VMEM((ROWS + CONV_W - 1, SUB, LRU_BLOCK), F32)],
        compiler_params=_params(2),
        name="lru_scan",
    )(u3, uctx3, u3, u3, lru_conv_w[0], vec(lru_conv_b[0]), gate_w(0), gate_w(1),
      gate_b(0), gate_b(1), lru_lambda[0])

    h_f, h_b = pl.pallas_call(
        _lru_chain_kernel,
        out_shape=[jax.ShapeDtypeStruct((GRID_W, D), F32)] * 2,
        name="lru_chain",
    )(sums)

    reps = OUT_TM // GRID_W
    lat = lambda i: (i, 0)
    big = lambda f: pl.BlockSpec((OUT_SUBTILES * OUT_TM, D), f)
    out = pl.pallas_call(
        _lru_out_kernel,
        grid=(SEQ // (OUT_SUBTILES * OUT_TM),),
        in_specs=[big(lat), _const((None, None, SUB, D), (1, 0, 0, 0)),
                  big(lat), big(lat), big(lat), big(lat),
                  _const((OUT_TM, D)), _const((OUT_TM, D)), _const((D, D))] + ffn_specs(1) + [_const((1, D))],
        out_specs=big(lat),
        out_shape=jax.ShapeDtypeStruct((SEQ, D), F32),
        compiler_params=_params(),
        name="lru_out",
    )(x_mid, mods, y_br, s0.reshape(SEQ, D), pf.reshape(SEQ, D), pb.reshape(SEQ, D),
      jnp.tile(h_f, (reps, 1)), jnp.tile(h_b, (reps, 1)),
      lru_w_out[0].astype(BF16), vec(norm_ffn_g[1]), w1b, w3b, w2b, vec(final_norm_g))
    return out.reshape(1, SEQ, D)
```

```python
import jax
import jax.numpy as jnp
from jax import lax
from jax.experimental import pallas as pl
from jax.experimental.pallas import tpu as pltpu

D = 1024
SEQ = 16384
GRID_W = 64
ROWS = SEQ // GRID_W
CTX = 256
TM = 256
NT = SEQ // TM
HEADS = 4
DK = 512
DV = 1024
HK = DK // HEADS
HV = DV // HEADS
RANK = 16
TAU = 16.0
CHUNK = 64
NCH = TM // CHUNK
LRU_BLOCKS = 4
LRU_BLOCK = D // LRU_BLOCKS
LRU_C = 8.0
FFN = 2816
EPS = 1e-6
SUB = 8
BF16_SUBLANES = 16
N_STREAMED_WEIGHTS = 6
CONV_W = 4
CONV_LEFT = 2
NGRP = GRID_W // SUB
SUBTILES = 2
CTX_STEP = NT // SUBTILES
OUT_TM = 256
OUT_SUBTILES = 2

F32 = jnp.float32
BF16 = jnp.bfloat16

V7X_VMEM_LIMIT = 56 * 1024 * 1024
LOG2E = 1.4426950408889634
TINY = 1e-30


def _dot(a, b):
    return jnp.dot(a, b, preferred_element_type=F32)


def _dot_nt(a, b):
    return lax.dot_general(a, b, (((1,), (1,)), ((), ())), preferred_element_type=F32)


def _dot_tn(a, b):
    return lax.dot_general(a, b, (((0,), (0,)), ((), ())), preferred_element_type=F32)


def _rms(x, g):
    ms = jnp.mean(x * x, axis=-1, keepdims=True)
    return x * lax.rsqrt(ms + EPS) * g


def _rms_mod(x, g, scale, shift):
    return _rms(x, g) * (1.0 + scale) + shift


def _silu(x):
    return x * jax.nn.sigmoid(x)


def _split_bf16(x):
    hi = x.astype(BF16)
    lo = (x - hi.astype(F32)).astype(BF16)
    return hi, lo


def _mod_kernel(c_ref, w_ref, b_ref, o_ref):
    s = _silu(c_ref[...])
    w = w_ref[...]
    b = b_ref[...]
    o_ref[...] = jnp.zeros_like(o_ref)
    for r in range(2):
        o_ref[r:r + 1, :] = jnp.sum(w * s[:, r:r + 1], axis=0, keepdims=True) + b


def _modulation(ccols, w_mod, b_mod):
    depth = w_mod.shape[0]
    tn = 1536
    return pl.pallas_call(
        _mod_kernel,
        grid=(depth, 6 * D // tn),
        in_specs=[
            pl.BlockSpec((D, 2), lambda l, n: (0, 0)),
            pl.BlockSpec((None, D, tn), lambda l, n: (l, 0, n)),
            pl.BlockSpec((None, 1, tn), lambda l, n: (l, 0, n)),
        ],
        out_specs=pl.BlockSpec((None, SUB, tn), lambda l, n: (l, 0, n)),
        out_shape=jax.ShapeDtypeStruct((depth, SUB, 6 * D), F32),
        compiler_params=pltpu.CompilerParams(
            dimension_semantics=("arbitrary", "arbitrary"),
            vmem_limit_bytes=V7X_VMEM_LIMIT),
        name="mod",
    )(ccols, w_mod, b_mod.reshape(depth, 1, 6 * D))


def _gla_tile(q, k, v, la, s_ref, o_ref, rev):
    row = lax.broadcasted_iota(jnp.int32, (TM, TM), 0)
    col = lax.broadcasted_iota(jnp.int32, (TM, TM), 1)
    shift = CHUNK.bit_length() - 1
    same = jnp.right_shift(row, shift) == jnp.right_shift(col, shift)
    tri = same & ((col >= row) if rev else (col <= row))
    tri_b = tri.astype(BF16)
    la_hi, la_lo = _split_bf16(la)
    b = _dot(tri_b, la_hi) + _dot(tri_b, la_lo)
    ends = [c * CHUNK if rev else (c + 1) * CHUNK - 1 for c in range(NCH)]
    bl = jnp.concatenate([jnp.broadcast_to(b[r:r + 1], (CHUNK, DK)) for r in ends], axis=0)
    q_dec = (q * jnp.exp(b)).astype(BF16)
    k_inv = (k * jnp.exp(-b)).astype(BF16)
    k_end = (k * jnp.exp(bl - b)).astype(BF16)
    decay = jnp.exp(bl)
    order = range(NCH - 1, -1, -1) if rev else range(NCH)
    for h in range(HEADS):
        ks = slice(h * HK, (h + 1) * HK)
        vs = slice(h * HV, (h + 1) * HV)
        qd = q_dec[:, ks]
        vh = v[:, vs]
        att = jnp.where(tri, _dot_nt(qd, k_inv[:, ks]), 0.0).astype(BF16)
        o_intra = _dot(att, vh)
        st = s_ref[h]
        for c in order:
            rs = slice(c * CHUNK, (c + 1) * CHUNK)
            o_inter = _dot_nt(qd[rs], st.astype(BF16))
            o_ref[rs, vs] = (o_intra[rs] + o_inter).astype(o_ref.dtype)
            u_t = _dot_tn(vh[rs], k_end[rs, ks])
            st = decay[c * CHUNK:c * CHUNK + 1, ks] * st + u_t
        s_ref[h] = st


def _proj_store(rs, z, zg, q_ref, k_ref, v_ref, g_ref, la_ref):
    q_ref[rs] = (z[:, :DK] * (HK ** -0.5)).astype(BF16)
    k_ref[rs] = z[:, DK:2 * DK].astype(BF16)
    v_ref[rs] = z[:, 2 * DK:2 * DK + DV].astype(BF16)
    g_ref[rs] = z[:, 2 * DK + DV:].astype(BF16)
    la_ref[rs] = (jnp.minimum(zg, 0.0) - jnp.log(1.0 + jnp.exp(-jnp.abs(zg)))) * (1.0 / TAU)


def _gla_proj_kernel(x_ref, ctx_ref, mod_ref, gmix_ref, win_ref, gw1_ref, gw2_ref, gb_ref,
                     q_ref, k_ref, v_ref, g_ref, la_ref):
    is_ctx = pl.program_id(0) == CTX_STEP
    mod = mod_ref[...]
    rows = [slice(j * TM, (j + 1) * TM) for j in range(SUBTILES)]
    a = [_rms_mod(jnp.where(is_ctx, ctx_ref[...], x_ref[rs]), gmix_ref[...], mod[1:2], mod[0:1]).astype(BF16)
         for rs in rows]
    pending = None
    for rs, at in zip(rows, a):
        z = _dot(at, win_ref[...])
        t = _dot(at, gw1_ref[...]).astype(BF16)
        zg = _dot(t, gw2_ref[...]) + gb_ref[...]
        if pending is not None:
            _proj_store(*pending, q_ref, k_ref, v_ref, g_ref, la_ref)
        pending = (rs, z, zg)
    _proj_store(*pending, q_ref, k_ref, v_ref, g_ref, la_ref)


def _gla_scan_kernel(qf_ref, kf_ref, vf_ref, laf_ref, qb_ref, kb_ref, vb_ref, lab_ref, *rest):
    nw = N_STREAMED_WEIGHTS
    w_in, (of_ref, ob_ref), w_out, (sf_ref, sb_ref) = (
        rest[:nw], rest[nw:nw + 2], rest[nw + 2:2 * nw + 2], rest[2 * nw + 2:])
    for src, dst in zip(w_in, w_out):
        dst[...] = src[...].astype(BF16)

    @pl.when(pl.program_id(0) == 0)
    def _():
        sf_ref[...] = jnp.zeros_like(sf_ref)
        sb_ref[...] = jnp.zeros_like(sb_ref)

    _gla_tile(qf_ref[...].astype(F32), kf_ref[...].astype(F32), vf_ref[...], laf_ref[...],
              sf_ref, of_ref, rev=False)
    _gla_tile(qb_ref[...].astype(F32), kb_ref[...].astype(F32), vb_ref[...], lab_ref[...],
              sb_ref, ob_ref, rev=True)


def _ffn_stages(x1, mod, gffn, w1_ref, w3_ref, w2_ref):
    f = [_rms_mod(t, gffn, mod[4:5], mod[3:4]).astype(BF16) for t in x1]
    p = [(_silu(_dot(t, w1_ref[...])) * _dot(t, w3_ref[...])).astype(BF16) for t in f]
    return [t + mod[5:6] * _dot(pt, w2_ref[...]) for t, pt in zip(x1, p)]


def _gla_out_kernel(x_ref, ctx_ref, mod0_ref, mod1_ref, of0_ref, of1_ref, ob0_ref, ob1_ref,
                    g0_ref, g1_ref,
                    hn_ref, wout_ref, gffn_ref, w1_ref, w3_ref, w2_ref, gmix1_ref, lwin_ref,
                    xo_ref, y_ref, u_ref):
    is_ctx = pl.program_id(0) == CTX_STEP
    hn = hn_ref[...]
    mod0 = mod0_ref[...]
    mod1 = mod1_ref[...]
    subtiles = ((of0_ref, ob0_ref, g0_ref), (of1_ref, ob1_ref, g1_ref))
    rows = [slice(j * TM, (j + 1) * TM) for j in range(SUBTILES)]
    gated = []
    for of_ref, ob_ref, g_ref in subtiles:
        o = of_ref[...].astype(F32) + ob_ref[...].astype(F32)
        parts = [_rms(o[:, h * HV:(h + 1) * HV], hn) for h in range(HEADS)]
        on = jnp.concatenate(parts, axis=-1)
        gated.append((on * _silu(g_ref[...].astype(F32))).astype(BF16))
    x1 = [jnp.where(is_ctx, ctx_ref[...], x_ref[rs]) + mod0[2:3] * _dot(gt, wout_ref[...])
          for rs, gt in zip(rows, gated)]
    x2 = _ffn_stages(x1, mod0, gffn_ref[...], w1_ref, w3_ref, w2_ref)
    a = []
    for rs, t in zip(rows, x2):
        xo_ref[rs] = t
        a.append(_rms_mod(t, gmix1_ref[...], mod1[1:2], mod1[0:1]).astype(BF16))
    z = [_dot(t, lwin_ref[...]) for t in a]
    for rs, t in zip(rows, z):
        y_ref[rs] = jax.nn.gelu(t[:, :D]).astype(BF16)
        u_ref[rs] = t[:, D:]


def _lru_coeffs(ub, uch, wg_ref, bg_ref, lam_row):
    zz = _dot(ub, wg_ref[...])
    bg = bg_ref[...]
    ta = jnp.tanh(zz[:, :LRU_BLOCK] + bg[0:1])
    tx = jnp.tanh(zz[:, LRU_BLOCK:] + bg[1:2])
    c2 = (-0.5 * LRU_C * LOG2E) * jax.nn.softplus(-lam_row)
    a = jnp.exp2(c2 * ta + c2)
    v = 1.0 - a * a
    s = v * lax.rsqrt(jnp.maximum(v, TINY))
    b = s * ((tx + 1.0) * uch)
    shape = (ROWS, SUB, LRU_BLOCK)
    return a.reshape(shape), b.reshape(shape)


def _lru_scan_kernel(u_ref, uctx_ref, hp_ref, hn_ref, cw_ref, cb_ref, wg0_ref, wg1_ref,
                     bg0_ref, bg1_ref, lam_ref,
                     s0_ref, pf_ref, pb_ref, sum_ref,
                     e_ref):
    wg = pl.program_id(1)
    u = jnp.where(wg == 0, uctx_ref[...], u_ref[...])
    prev_ok = wg >= 2
    next_ok = (wg >= 1) & (wg <= NGRP - 1)
    sub = lax.broadcasted_iota(jnp.int32, (SUB, LRU_BLOCK), 0)
    hp = hp_ref[...]
    hn = hn_ref[...]

    def from_prev_column(cur, halo):
        edge = jnp.where(prev_ok, pltpu.roll(halo, 1, 0), 0.0)
        return jnp.where(sub == 0, edge, pltpu.roll(cur, 1, 0))

    e_ref[CONV_LEFT:CONV_LEFT + ROWS] = u
    e_ref[0] = from_prev_column(u[ROWS - 2], hp[0])
    e_ref[1] = from_prev_column(u[ROWS - 1], hp[1])
    edge = jnp.where(next_ok, pltpu.roll(hn[0], SUB - 1, 0), 0.0)
    e_ref[CONV_LEFT + ROWS] = jnp.where(sub == SUB - 1, edge, pltpu.roll(u[0], SUB - 1, 0))
    cw = cw_ref[...]
    uc = cb_ref[...] + cw[0:1] * e_ref[0:ROWS]
    for j in range(1, CONV_W):
        uc = uc + cw[j:j + 1] * e_ref[j:j + ROWS]
    uc2 = uc.reshape(ROWS * SUB, LRU_BLOCK)
    ub = uc2.astype(BF16)
    lam = lam_ref[...]
    uch = 0.5 * uc2
    a0, b0 = _lru_coeffs(ub, uch, wg0_ref, bg0_ref, lam[0:1])
    a1, b1 = _lru_coeffs(ub, uch, wg1_ref, bg1_ref, lam[1:2])

    h = jnp.zeros((SUB, LRU_BLOCK), F32)
    p = jnp.ones((SUB, LRU_BLOCK), F32)
    for r in range(ROWS):
        h = a0[r] * h + b0[r]
        p = a0[r] * p
        s0_ref[r] = h
        pf_ref[r] = p
    sum_ref[0] = p
    sum_ref[1] = h
    h = jnp.zeros((SUB, LRU_BLOCK), F32)
    p = jnp.ones((SUB, LRU_BLOCK), F32)
    for r in range(ROWS - 1, -1, -1):
        h = a1[r] * h + b1[r]
        p = a1[r] * p
        s0_ref[r] = s0_ref[r] + h
        pb_ref[r] = p
    sum_ref[2] = p
    sum_ref[3] = h


def _lru_chain_kernel(sum_ref, hf_ref, hb_ref):
    s = sum_ref[1, 0:1, :]
    for w in range(GRID_W):
        hf_ref[w:w + 1, :] = s
        s = sum_ref[0, SUB + w:SUB + w + 1, :] * s + sum_ref[1, SUB + w:SUB + w + 1, :]
    s = sum_ref[3, 0:1, :]
    for w in range(GRID_W - 1, -1, -1):
        hb_ref[w:w + 1, :] = s
        s = sum_ref[2, SUB + w:SUB + w + 1, :] * s + sum_ref[3, SUB + w:SUB + w + 1, :]


def _lru_out_kernel(x_ref, mod_ref, y_ref, s0_ref, pf_ref, pb_ref, hf_ref, hb_ref,
                    wout_ref, gffn_ref, w1_ref, w3_ref, w2_ref, gfin_ref, o_ref):
    mod = mod_ref[...]
    rows = [slice(j * OUT_TM, (j + 1) * OUT_TM) for j in range(OUT_SUBTILES)]
    ym = []
    for rs in rows:
        hs = s0_ref[rs] + pf_ref[rs] * hf_ref[...] + pb_ref[rs] * hb_ref[...]
        ym.append((hs * y_ref[rs].astype(F32)).astype(BF16))
    x1 = [x_ref[rs] + mod[2:3] * _dot(t, wout_ref[...]) for rs, t in zip(rows, ym)]
    x2 = _ffn_stages(x1, mod, gffn_ref[...], w1_ref, w3_ref, w2_ref)
    for rs, t in zip(rows, x2):
        o_ref[rs] = _rms(t, gfin_ref[...])


def _const(shape, index=None):
    nd = len(shape)
    index = (0,) * nd if index is None else index
    return pl.BlockSpec(shape, lambda *_: index, pipeline_mode=pl.Buffered(1))


def _params(n_axes=1):
    return pltpu.CompilerParams(dimension_semantics=("arbitrary",) * n_axes,
                                vmem_limit_bytes=V7X_VMEM_LIMIT)


def kernel(x, c, ctx, c_ctx, norm_mix_g, norm_ffn_g, w_mod, b_mod, gla_w_in, gla_gate_w1, gla_gate_w2, gla_gate_b, gla_head_norm_g, gla_w_out, lru_w_in, lru_conv_w, lru_conv_b, lru_gate_a_w, lru_gate_a_b, lru_gate_x_w, lru_gate_x_b, lru_lambda, lru_w_out, ffn_w1, ffn_w3, ffn_w2, final_norm_g):
    assert x.shape == (1, SEQ, D) and ctx.shape == (1, CTX, D) and CTX == TM == ROWS
    x2d = x.reshape(SEQ, D)
    ctx2d = ctx.reshape(CTX, D)
    nseq = (NT + SUBTILES) * TM

    m = _modulation(jnp.stack([c[0], c_ctx], axis=1), w_mod, b_mod)
    mods = m[:, :2].reshape(2, 2, 6, D)
    mods = jnp.pad(mods, ((0, 0), (0, 0), (0, SUB - 6), (0, 0)))

    def mod_spec(layer):
        return pl.BlockSpec((None, None, SUB, D), lambda i: (layer, jnp.where(i == CTX_STEP, 1, 0), 0, 0))

    vec = lambda t: t.reshape(1, -1)
    tile = lambda w, f: pl.BlockSpec((TM, w), f)

    def ffn_specs(layer):
        return [_const((1, D)), _const((None, D, FFN), (layer, 0, 0)),
                _const((None, D, FFN), (layer, 0, 0)), _const((None, FFN, D), (layer, 0, 0))]

    win = gla_w_in[0].astype(BF16)
    gw1 = jnp.concatenate([gla_gate_w1[0, 0], gla_gate_w1[0, 1]], axis=1)
    gw1 = jnp.pad(gw1, ((0, 0), (0, 128 - 2 * RANK))).astype(BF16)
    gw2 = jnp.zeros((128, 2 * DK), F32)
    gw2 = gw2.at[:RANK, :DK].set(gla_gate_w2[0, 0]).at[RANK:2 * RANK, DK:].set(gla_gate_w2[0, 1])
    gw2 = gw2.astype(BF16)
    gb = gla_gate_b[0].reshape(1, 2 * DK)

    pair = lambda w: pl.BlockSpec((SUBTILES * TM, w), lambda i: (i, 0))
    lat_pair = pl.BlockSpec((SUBTILES * TM, D), lambda i: (jnp.minimum(i, CTX_STEP - 1), 0))
    q, k, v, g, la = pl.pallas_call(
        _gla_proj_kernel,
        grid=(CTX_STEP + 1,),
        in_specs=[lat_pair, _const((CTX, D)), mod_spec(0), _const((1, D)),
                  _const((D, 2 * DK + 2 * DV)), _const((D, 128)), _const((128, 2 * DK)),
                  _const((1, 2 * DK))],
        out_specs=[pair(DK), pair(DK), pair(DV), pair(DV), pair(2 * DK)],
        out_shape=[jax.ShapeDtypeStruct((nseq, DK), BF16), jax.ShapeDtypeStruct((nseq, DK), BF16),
                   jax.ShapeDtypeStruct((nseq, DV), BF16), jax.ShapeDtypeStruct((nseq, DV), BF16),
                   jax.ShapeDtypeStruct((nseq, 2 * DK), F32)],
        compiler_params=_params(),
        name="gla_proj",
    )(x2d, ctx2d, mods, vec(norm_mix_g[0]), win, gw1, gw2, gb)

    seq_f = lambda i: (jnp.where(i == 0, NT, i - 1), 0)
    seq_b = lambda i: (jnp.where(i == 0, NT, NT - i), 0)
    streamed = [ffn_w1.reshape(2 * D, FFN), ffn_w3.reshape(2 * D, FFN), ffn_w2.reshape(2 * FFN, D),
                gla_w_out[0], lru_w_in[0], lru_w_out[0]]
    assert len(streamed) == N_STREAMED_WEIGHTS

    def slab_spec(w):
        nblk = NT if w.shape[0] % (NT * BF16_SUBLANES) == 0 else NT // 2
        rows = w.shape[0] // nblk
        assert rows * nblk == w.shape[0] and rows % BF16_SUBLANES == 0
        return pl.BlockSpec((rows, w.shape[1]), lambda i: (jnp.minimum(i, nblk - 1), 0))

    slabs = [slab_spec(w) for w in streamed]
    la_b = lambda i: (seq_b(i)[0], 1)
    o_f, o_b, w1b, w3b, w2b, gwob, lwinb, lwob = pl.pallas_call(
        _gla_scan_kernel,
        grid=(NT + 1,),
        in_specs=[tile(DK, seq_f), tile(DK, seq_f), tile(DV, seq_f), tile(DK, seq_f),
                  tile(DK, seq_b), tile(DK, seq_b), tile(DV, seq_b), tile(DK, la_b)] + slabs,
        out_specs=[tile(DV, seq_f), tile(DV, seq_b)] + slabs,
        out_shape=[jax.ShapeDtypeStruct(((NT + 1) * TM, DV), BF16)] * 2
                  + [jax.ShapeDtypeStruct(w.shape, BF16) for w in streamed],
        scratch_shapes=[pltpu.VMEM((HEADS, HV, HK), F32)] * 2,
        compiler_params=_params(),
        name="gla_scan",
    )(q, k, v, la, q, k, v, la, *streamed)
    w1b, w3b, w2b = w1b.reshape(2, D, FFN), w3b.reshape(2, D, FFN), w2b.reshape(2, FFN, D)

    sub = lambda w, j: pl.BlockSpec((TM, w), lambda i: (jnp.minimum(SUBTILES * i + j, NT), 0))
    x_mid, y_br, u_all = pl.pallas_call(
        _gla_out_kernel,
        grid=(CTX_STEP + 1,),
        in_specs=[lat_pair, _const((CTX, D)), mod_spec(0), mod_spec(1),
                  sub(DV, 0), sub(DV, 1), sub(DV, 0), sub(DV, 1), sub(DV, 0), sub(DV, 1),
                  _const((1, HV)), _const((DV, D))] + ffn_specs(0) + [_const((1, D)), _const((D, 2 * D))],
        out_specs=[pair(D), pair(D), pair(D)],
        out_shape=[jax.ShapeDtypeStruct((nseq, D), F32), jax.ShapeDtypeStruct((nseq, D), BF16),
                   jax.ShapeDtypeStruct((nseq, D), F32)],
        compiler_params=_params(),
        name="gla_out",
    )(x2d, ctx2d, mods, mods, o_f, o_f, o_b, o_b, g, g,
      vec(gla_head_norm_g[0]), gwob, vec(norm_ffn_g[0]), w1b, w3b, w2b,
      vec(norm_mix_g[1]), lwinb)

    u3 = u_all.reshape(nseq // GRID_W, GRID_W, D)
    u_ctx = u_all[NT * TM:(NT + 1) * TM]
    uctx3 = jnp.pad(u_ctx[:, None, :], ((0, 0), (0, SUB - 1), (0, 0)))
    grp = lambda wg: jnp.maximum(wg - 1, 0)
    blk = (ROWS, SUB, LRU_BLOCK)
    gate_w = lambda d: (0.5 * jnp.concatenate([lru_gate_a_w[0, d], lru_gate_x_w[0, d]], axis=-1)).astype(BF16)
    gate_b = lambda d: 0.5 * jnp.stack([lru_gate_a_b[0, d], lru_gate_x_b[0, d]])
    chan = lambda rows: pl.BlockSpec((rows, LRU_BLOCK), lambda cb, wg: (0, cb))
    gate_spec = pl.BlockSpec((None, LRU_BLOCK, 2 * LRU_BLOCK), lambda cb, wg: (cb, 0, 0))
    last2 = ROWS // 2 - 1
    s0, pf, pb, sums = pl.pallas_call(
        _lru_scan_kernel,
        grid=(LRU_BLOCKS, NGRP + 1),
        in_specs=[pl.BlockSpec(blk, lambda cb, wg: (0, grp(wg), cb)),
                  pl.BlockSpec(blk, lambda cb, wg: (0, 0, cb)),
                  pl.BlockSpec((2, SUB, LRU_BLOCK), lambda cb, wg: (last2, jnp.maximum(wg - 2, 0), cb)),
                  pl.BlockSpec((2, SUB, LRU_BLOCK), lambda cb, wg: (0, jnp.minimum(wg, NGRP - 1), cb)),
                  chan(CONV_W), chan(1), gate_spec, gate_spec, chan(2), chan(2), chan(2)],
        out_specs=[pl.BlockSpec(blk, lambda cb, wg: (0, grp(wg), cb)),
                   pl.BlockSpec(blk, lambda cb, wg: (0, grp(wg), cb)),
                   pl.BlockSpec(blk, lambda cb, wg: (0, grp(wg), cb)),
                   pl.BlockSpec((4, SUB, LRU_BLOCK), lambda cb, wg: (0, wg, cb))],
        out_shape=[jax.ShapeDtypeStruct((ROWS, GRID_W, D), F32)] * 3
                  + [jax.ShapeDtypeStruct((4, (NGRP + 1) * SUB, D), F32)],
        scratch_shapes=[pltpu.VMEM((ROWS + CONV_W - 1, SUB, LRU_BLOCK), F32)],
        compiler_params=_params(2),
        name="lru_scan",
    )(u3, uctx3, u3, u3, lru_conv_w[0], vec(lru_conv_b[0]), gate_w(0), gate_w(1),
      gate_b(0), gate_b(1), lru_lambda[0])

    h_f, h_b = pl.pallas_call(
        _lru_chain_kernel,
        out_shape=[jax.ShapeDtypeStruct((GRID_W, D), F32)] * 2,
        name="lru_chain",
    )(sums)

    reps = OUT_TM // GRID_W
    lat = lambda i: (i, 0)
    big = lambda f: pl.BlockSpec((OUT_SUBTILES * OUT_TM, D), f)
    out = pl.pallas_call(
        _lru_out_kernel,
        grid=(SEQ // (OUT_SUBTILES * OUT_TM),),
        in_specs=[big(lat), _const((None, None, SUB, D), (1, 0, 0, 0)),
                  big(lat), big(lat), big(lat), big(lat),
                  _const((OUT_TM, D)), _const((OUT_TM, D)), _const((D, D))] + ffn_specs(1) + [_const((1, D))],
        out_specs=big(lat),
        out_shape=jax.ShapeDtypeStruct((SEQ, D), F32),
        compiler_params=_params(),
        name="lru_out",
    )(x_mid, mods, y_br, s0.reshape(SEQ, D), pf.reshape(SEQ, D), pb.reshape(SEQ, D),
      jnp.tile(h_f, (reps, 1)), jnp.tile(h_b, (reps, 1)),
      lwob, vec(norm_ffn_g[1]), w1b, w3b, w2b, vec(final_norm_g))
    return out.reshape(1, SEQ, D)
```

```python
import jax
import jax.numpy as jnp
from jax import lax
from jax.experimental import pallas as pl
from jax.experimental.pallas import tpu as pltpu

D = 1024
SEQ = 16384
GRID_W = 64
ROWS = SEQ // GRID_W
CTX = 256
TM = 256
NT = SEQ // TM
HEADS = 4
DK = 512
DV = 1024
HK = DK // HEADS
HV = DV // HEADS
RANK = 16
TAU = 16.0
CHUNK = 64
NCH = TM // CHUNK
LRU_BLOCKS = 4
LRU_BLOCK = D // LRU_BLOCKS
LRU_C = 8.0
FFN = 2816
EPS = 1e-6
SUB = 8
BF16_SUBLANES = 16
N_STREAMED_WEIGHTS = 6
CONV_W = 4
CONV_LEFT = 2
NGRP = GRID_W // SUB
SUBTILES = 2
CTX_STEP = NT // SUBTILES
OUT_TM = 256
OUT_SUBTILES = 2

F32 = jnp.float32
BF16 = jnp.bfloat16

V7X_VMEM_LIMIT = 56 * 1024 * 1024
LOG2E = 1.4426950408889634
TINY = 1e-30


def _dot(a, b):
    return jnp.dot(a, b, preferred_element_type=F32)


def _dot_nt(a, b):
    return lax.dot_general(a, b, (((1,), (1,)), ((), ())), preferred_element_type=F32)


def _dot_tn(a, b):
    return lax.dot_general(a, b, (((0,), (0,)), ((), ())), preferred_element_type=F32)


def _rms(x, g):
    ms = jnp.mean(x * x, axis=-1, keepdims=True)
    return x * lax.rsqrt(ms + EPS) * g


def _rms_mod(x, g, scale, shift):
    return _rms(x, g) * (1.0 + scale) + shift


def _silu(x):
    return x * jax.nn.sigmoid(x)


def _split_bf16(x):
    hi = x.astype(BF16)
    lo = (x - hi.astype(F32)).astype(BF16)
    return hi, lo


def _mod_kernel(c_ref, w_ref, b_ref, o_ref):
    s = _silu(c_ref[...])
    w = w_ref[...]
    b = b_ref[...]
    o_ref[...] = jnp.zeros_like(o_ref)
    for r in range(2):
        o_ref[r:r + 1, :] = jnp.sum(w * s[:, r:r + 1], axis=0, keepdims=True) + b


def _modulation(ccols, w_mod, b_mod):
    depth = w_mod.shape[0]
    tn = 1536
    return pl.pallas_call(
        _mod_kernel,
        grid=(depth, 6 * D // tn),
        in_specs=[
            pl.BlockSpec((D, 2), lambda l, n: (0, 0)),
            pl.BlockSpec((None, D, tn), lambda l, n: (l, 0, n)),
            pl.BlockSpec((None, 1, tn), lambda l, n: (l, 0, n)),
        ],
        out_specs=pl.BlockSpec((None, SUB, tn), lambda l, n: (l, 0, n)),
        out_shape=jax.ShapeDtypeStruct((depth, SUB, 6 * D), F32),
        compiler_params=pltpu.CompilerParams(
            dimension_semantics=("arbitrary", "arbitrary"),
            vmem_limit_bytes=V7X_VMEM_LIMIT),
        name="mod",
    )(ccols, w_mod, b_mod.reshape(depth, 1, 6 * D))


def _gla_tiles(jobs):
    row = lax.broadcasted_iota(jnp.int32, (TM, TM), 0)
    col = lax.broadcasted_iota(jnp.int32, (TM, TM), 1)
    shift = CHUNK.bit_length() - 1
    same = jnp.right_shift(row, shift) == jnp.right_shift(col, shift)
    prep = []
    for q, k, v, la, s_ref, o_ref, rev in jobs:
        tri = same & ((col >= row) if rev else (col <= row))
        tri_b = tri.astype(BF16)
        la_hi, la_lo = _split_bf16(la)
        b = _dot(tri_b, la_hi) + _dot(tri_b, la_lo)
        prep.append((tri, b))
    work = []
    for (q, k, v, la, s_ref, o_ref, rev), (tri, b) in zip(jobs, prep):
        ends = [c * CHUNK if rev else (c + 1) * CHUNK - 1 for c in range(NCH)]
        bl = jnp.concatenate([jnp.broadcast_to(b[r:r + 1], (CHUNK, DK)) for r in ends], axis=0)
        q_dec = (q * jnp.exp(b)).astype(BF16)
        k_inv = (k * jnp.exp(-b)).astype(BF16)
        k_end = (k * jnp.exp(bl - b)).astype(BF16)
        totals = jnp.concatenate([b[r:r + 1] for r in ends] + [jnp.zeros((SUB - NCH, DK), F32)], axis=0)
        decay_t = jnp.exp(totals).T
        order = list(range(NCH - 1, -1, -1) if rev else range(NCH))
        for h in range(HEADS):
            ks = slice(h * HK, (h + 1) * HK)
            vs = slice(h * HV, (h + 1) * HV)
            work.append(dict(tri=tri, qd=q_dec[:, ks], ki=k_inv[:, ks], ke=k_end[:, ks], vh=v[:, vs],
                             dec=decay_t[ks], order=order, s_ref=s_ref, h=h, o_ref=o_ref, vs=vs))
    for w in work:
        w["att"] = jnp.where(w["tri"], _dot_nt(w["qd"], w["ki"]), 0.0).astype(BF16)
    for w in work:
        w["o_intra"] = _dot(w["att"], w["vh"])
        w["st"] = w["s_ref"][w["h"]]
    for idx in range(NCH):
        for w in work:
            c = w["order"][idx]
            rs = slice(c * CHUNK, (c + 1) * CHUNK)
            o_inter = _dot(w["qd"][rs], w["st"].astype(BF16))
            w["o_ref"][rs, w["vs"]] = (w["o_intra"][rs] + o_inter).astype(w["o_ref"].dtype)
            u = _dot_tn(w["ke"][rs], w["vh"][rs])
            w["st"] = w["dec"][:, c:c + 1] * w["st"] + u
    for w in work:
        w["s_ref"][w["h"]] = w["st"]


def _proj_store(rs, z, zg, q_ref, k_ref, v_ref, g_ref, la_ref):
    q_ref[rs] = (z[:, :DK] * (HK ** -0.5)).astype(BF16)
    k_ref[rs] = z[:, DK:2 * DK].astype(BF16)
    v_ref[rs] = z[:, 2 * DK:2 * DK + DV].astype(BF16)
    g_ref[rs] = z[:, 2 * DK + DV:].astype(BF16)
    la_ref[rs] = (jnp.minimum(zg, 0.0) - jnp.log(1.0 + jnp.exp(-jnp.abs(zg)))) * (1.0 / TAU)


def _gla_proj_kernel(x_ref, ctx_ref, mod_ref, gmix_ref, win_ref, gw1_ref, gw2_ref, gb_ref,
                     q_ref, k_ref, v_ref, g_ref, la_ref):
    is_ctx = pl.program_id(0) == CTX_STEP
    mod = mod_ref[...]
    rows = [slice(j * TM, (j + 1) * TM) for j in range(SUBTILES)]
    a = [_rms_mod(jnp.where(is_ctx, ctx_ref[...], x_ref[rs]), gmix_ref[...], mod[1:2], mod[0:1]).astype(BF16)
         for rs in rows]
    pending = None
    for rs, at in zip(rows, a):
        z = _dot(at, win_ref[...])
        t = _dot(at, gw1_ref[...]).astype(BF16)
        zg = _dot(t, gw2_ref[...]) + gb_ref[...]
        if pending is not None:
            _proj_store(*pending, q_ref, k_ref, v_ref, g_ref, la_ref)
        pending = (rs, z, zg)
    _proj_store(*pending, q_ref, k_ref, v_ref, g_ref, la_ref)


def _gla_scan_kernel(qf_ref, kf_ref, vf_ref, laf_ref, qb_ref, kb_ref, vb_ref, lab_ref, *rest):
    nw = N_STREAMED_WEIGHTS
    w_in, (of_ref, ob_ref), w_out, (sf_ref, sb_ref) = (
        rest[:nw], rest[nw:nw + 2], rest[nw + 2:2 * nw + 2], rest[2 * nw + 2:])

    @pl.when(pl.program_id(0) == 0)
    def _():
        sf_ref[...] = jnp.zeros_like(sf_ref)
        sb_ref[...] = jnp.zeros_like(sb_ref)

    _gla_tiles([
        (qf_ref[...].astype(F32), kf_ref[...].astype(F32), vf_ref[...], laf_ref[...], sf_ref, of_ref, False),
        (qb_ref[...].astype(F32), kb_ref[...].astype(F32), vb_ref[...], lab_ref[...], sb_ref, ob_ref, True)])
    for src, dst in zip(w_in, w_out):
        dst[...] = src[...].astype(BF16)


def _ffn_stages(x1, mod, gffn, w1_ref, w3_ref, w2_ref):
    f = [_rms_mod(t, gffn, mod[4:5], mod[3:4]).astype(BF16) for t in x1]
    p = [(_silu(_dot(t, w1_ref[...])) * _dot(t, w3_ref[...])).astype(BF16) for t in f]
    return [t + mod[5:6] * _dot(pt, w2_ref[...]) for t, pt in zip(x1, p)]


def _gla_out_kernel(x_ref, ctx_ref, mod0_ref, mod1_ref, of0_ref, of1_ref, ob0_ref, ob1_ref,
                    g0_ref, g1_ref,
                    hn_ref, wout_ref, gffn_ref, w1_ref, w3_ref, w2_ref, gmix1_ref, lwin_ref,
                    xo_ref, y_ref, u_ref):
    is_ctx = pl.program_id(0) == CTX_STEP
    hn = hn_ref[...]
    mod0 = mod0_ref[...]
    mod1 = mod1_ref[...]
    subtiles = ((of0_ref, ob0_ref, g0_ref), (of1_ref, ob1_ref, g1_ref))
    rows = [slice(j * TM, (j + 1) * TM) for j in range(SUBTILES)]
    gated = []
    for of_ref, ob_ref, g_ref in subtiles:
        o = of_ref[...].astype(F32) + ob_ref[...].astype(F32)
        parts = [_rms(o[:, h * HV:(h + 1) * HV], hn) for h in range(HEADS)]
        on = jnp.concatenate(parts, axis=-1)
        gated.append((on * _silu(g_ref[...].astype(F32))).astype(BF16))
    x1 = [jnp.where(is_ctx, ctx_ref[...], x_ref[rs]) + mod0[2:3] * _dot(gt, wout_ref[...])
          for rs, gt in zip(rows, gated)]
    x2 = _ffn_stages(x1, mod0, gffn_ref[...], w1_ref, w3_ref, w2_ref)
    a = []
    for rs, t in zip(rows, x2):
        xo_ref[rs] = t
        a.append(_rms_mod(t, gmix1_ref[...], mod1[1:2], mod1[0:1]).astype(BF16))
    z = [_dot(t, lwin_ref[...]) for t in a]
    for rs, t in zip(rows, z):
        y_ref[rs] = jax.nn.gelu(t[:, :D]).astype(BF16)
        u_ref[rs] = t[:, D:]


def _lru_coeffs(ub, uch, wg_ref, bg_ref, lam_row):
    zz = _dot(ub, wg_ref[...])
    bg = bg_ref[...]
    ta = jnp.tanh(zz[:, :LRU_BLOCK] + bg[0:1])
    tx = jnp.tanh(zz[:, LRU_BLOCK:] + bg[1:2])
    c2 = (-0.5 * LRU_C * LOG2E) * jax.nn.softplus(-lam_row)
    a = jnp.exp2(c2 * ta + c2)
    v = 1.0 - a * a
    s = v * lax.rsqrt(jnp.maximum(v, TINY))
    b = s * ((tx + 1.0) * uch)
    shape = (ROWS, SUB, LRU_BLOCK)
    return a.reshape(shape), b.reshape(shape)


def _lru_scan_kernel(u_ref, uctx_ref, hp_ref, hn_ref, cw_ref, cb_ref, wg0_ref, wg1_ref,
                     bg0_ref, bg1_ref, lam_ref,
                     s0_ref, pf_ref, pb_ref, sum_ref,
                     e_ref):
    wg = pl.program_id(1)
    u = jnp.where(wg == 0, uctx_ref[...], u_ref[...])
    prev_ok = wg >= 2
    next_ok = (wg >= 1) & (wg <= NGRP - 1)
    sub = lax.broadcasted_iota(jnp.int32, (SUB, LRU_BLOCK), 0)
    hp = hp_ref[...]
    hn = hn_ref[...]

    def from_prev_column(cur, halo):
        edge = jnp.where(prev_ok, pltpu.roll(halo, 1, 0), 0.0)
        return jnp.where(sub == 0, edge, pltpu.roll(cur, 1, 0))

    e_ref[CONV_LEFT:CONV_LEFT + ROWS] = u
    e_ref[0] = from_prev_column(u[ROWS - 2], hp[0])
    e_ref[1] = from_prev_column(u[ROWS - 1], hp[1])
    edge = jnp.where(next_ok, pltpu.roll(hn[0], SUB - 1, 0), 0.0)
    e_ref[CONV_LEFT + ROWS] = jnp.where(sub == SUB - 1, edge, pltpu.roll(u[0], SUB - 1, 0))
    cw = cw_ref[...]
    uc = cb_ref[...] + cw[0:1] * e_ref[0:ROWS]
    for j in range(1, CONV_W):
        uc = uc + cw[j:j + 1] * e_ref[j:j + ROWS]
    uc2 = uc.reshape(ROWS * SUB, LRU_BLOCK)
    ub = uc2.astype(BF16)
    lam = lam_ref[...]
    uch = 0.5 * uc2
    a0, b0 = _lru_coeffs(ub, uch, wg0_ref, bg0_ref, lam[0:1])
    a1, b1 = _lru_coeffs(ub, uch, wg1_ref, bg1_ref, lam[1:2])

    h = jnp.zeros((SUB, LRU_BLOCK), F32)
    p = jnp.ones((SUB, LRU_BLOCK), F32)
    for r in range(ROWS):
        h = a0[r] * h + b0[r]
        p = a0[r] * p
        s0_ref[r] = h
        pf_ref[r] = p
    sum_ref[0] = p
    sum_ref[1] = h
    h = jnp.zeros((SUB, LRU_BLOCK), F32)
    p = jnp.ones((SUB, LRU_BLOCK), F32)
    for r in range(ROWS - 1, -1, -1):
        h = a1[r] * h + b1[r]
        p = a1[r] * p
        s0_ref[r] = s0_ref[r] + h
        pb_ref[r] = p
    sum_ref[2] = p
    sum_ref[3] = h


def _lru_chain_kernel(sum_ref, hf_ref, hb_ref):
    s = sum_ref[1, 0:1, :]
    for w in range(GRID_W):
        hf_ref[w:w + 1, :] = s
        s = sum_ref[0, SUB + w:SUB + w + 1, :] * s + sum_ref[1, SUB + w:SUB + w + 1, :]
    s = sum_ref[3, 0:1, :]
    for w in range(GRID_W - 1, -1, -1):
        hb_ref[w:w + 1, :] = s
        s = sum_ref[2, SUB + w:SUB + w + 1, :] * s + sum_ref[3, SUB + w:SUB + w + 1, :]


def _lru_out_kernel(x_ref, mod_ref, y_ref, s0_ref, pf_ref, pb_ref, hf_ref, hb_ref,
                    wout_ref, gffn_ref, w1_ref, w3_ref, w2_ref, gfin_ref, o_ref):
    mod = mod_ref[...]
    rows = [slice(j * OUT_TM, (j + 1) * OUT_TM) for j in range(OUT_SUBTILES)]
    ym = []
    for rs in rows:
        hs = s0_ref[rs] + pf_ref[rs] * hf_ref[...] + pb_ref[rs] * hb_ref[...]
        ym.append((hs * y_ref[rs].astype(F32)).astype(BF16))
    x1 = [x_ref[rs] + mod[2:3] * _dot(t, wout_ref[...]) for rs, t in zip(rows, ym)]
    x2 = _ffn_stages(x1, mod, gffn_ref[...], w1_ref, w3_ref, w2_ref)
    for rs, t in zip(rows, x2):
        o_ref[rs] = _rms(t, gfin_ref[...])


def _const(shape, index=None):
    nd = len(shape)
    index = (0,) * nd if index is None else index
    return pl.BlockSpec(shape, lambda *_: index, pipeline_mode=pl.Buffered(1))


def _params(n_axes=1):
    return pltpu.CompilerParams(dimension_semantics=("arbitrary",) * n_axes,
                                vmem_limit_bytes=V7X_VMEM_LIMIT)


def kernel(x, c, ctx, c_ctx, norm_mix_g, norm_ffn_g, w_mod, b_mod, gla_w_in, gla_gate_w1, gla_gate_w2, gla_gate_b, gla_head_norm_g, gla_w_out, lru_w_in, lru_conv_w, lru_conv_b, lru_gate_a_w, lru_gate_a_b, lru_gate_x_w, lru_gate_x_b, lru_lambda, lru_w_out, ffn_w1, ffn_w3, ffn_w2, final_norm_g):
    assert x.shape == (1, SEQ, D) and ctx.shape == (1, CTX, D) and CTX == TM == ROWS
    x2d = x.reshape(SEQ, D)
    ctx2d = ctx.reshape(CTX, D)
    nseq = (NT + SUBTILES) * TM

    m = _modulation(jnp.stack([c[0], c_ctx], axis=1), w_mod, b_mod)
    mods = m[:, :2].reshape(2, 2, 6, D)
    mods = jnp.pad(mods, ((0, 0), (0, 0), (0, SUB - 6), (0, 0)))

    def mod_spec(layer):
        return pl.BlockSpec((None, None, SUB, D), lambda i: (layer, jnp.where(i == CTX_STEP, 1, 0), 0, 0))

    vec = lambda t: t.reshape(1, -1)
    tile = lambda w, f: pl.BlockSpec((TM, w), f)

    def ffn_specs(layer):
        return [_const((1, D)), _const((None, D, FFN), (layer, 0, 0)),
                _const((None, D, FFN), (layer, 0, 0)), _const((None, FFN, D), (layer, 0, 0))]

    win = gla_w_in[0].astype(BF16)
    gw1 = jnp.concatenate([gla_gate_w1[0, 0], gla_gate_w1[0, 1]], axis=1)
    gw1 = jnp.pad(gw1, ((0, 0), (0, 128 - 2 * RANK))).astype(BF16)
    gw2 = jnp.zeros((128, 2 * DK), F32)
    gw2 = gw2.at[:RANK, :DK].set(gla_gate_w2[0, 0]).at[RANK:2 * RANK, DK:].set(gla_gate_w2[0, 1])
    gw2 = gw2.astype(BF16)
    gb = gla_gate_b[0].reshape(1, 2 * DK)

    pair = lambda w: pl.BlockSpec((SUBTILES * TM, w), lambda i: (i, 0))
    lat_pair = pl.BlockSpec((SUBTILES * TM, D), lambda i: (jnp.minimum(i, CTX_STEP - 1), 0))
    q, k, v, g, la = pl.pallas_call(
        _gla_proj_kernel,
        grid=(CTX_STEP + 1,),
        in_specs=[lat_pair, _const((CTX, D)), mod_spec(0), _const((1, D)),
                  _const((D, 2 * DK + 2 * DV)), _const((D, 128)), _const((128, 2 * DK)),
                  _const((1, 2 * DK))],
        out_specs=[pair(DK), pair(DK), pair(DV), pair(DV), pair(2 * DK)],
        out_shape=[jax.ShapeDtypeStruct((nseq, DK), BF16), jax.ShapeDtypeStruct((nseq, DK), BF16),
                   jax.ShapeDtypeStruct((nseq, DV), BF16), jax.ShapeDtypeStruct((nseq, DV), BF16),
                   jax.ShapeDtypeStruct((nseq, 2 * DK), F32)],
        compiler_params=_params(),
        name="gla_proj",
    )(x2d, ctx2d, mods, vec(norm_mix_g[0]), win, gw1, gw2, gb)

    seq_f = lambda i: (jnp.where(i == 0, NT, i - 1), 0)
    seq_b = lambda i: (jnp.where(i == 0, NT, NT - i), 0)
    streamed = [ffn_w1.reshape(2 * D, FFN), ffn_w3.reshape(2 * D, FFN), ffn_w2.reshape(2 * FFN, D),
                gla_w_out[0], lru_w_in[0], lru_w_out[0]]
    assert len(streamed) == N_STREAMED_WEIGHTS

    def slab_spec(w):
        nblk = NT if w.shape[0] % (NT * BF16_SUBLANES) == 0 else NT // 2
        rows = w.shape[0] // nblk
        assert rows * nblk == w.shape[0] and rows % BF16_SUBLANES == 0
        return pl.BlockSpec((rows, w.shape[1]), lambda i: (jnp.minimum(i, nblk - 1), 0))

    slabs = [slab_spec(w) for w in streamed]
    la_b = lambda i: (seq_b(i)[0], 1)
    o_f, o_b, w1b, w3b, w2b, gwob, lwinb, lwob = pl.pallas_call(
        _gla_scan_kernel,
        grid=(NT + 1,),
        in_specs=[tile(DK, seq_f), tile(DK, seq_f), tile(DV, seq_f), tile(DK, seq_f),
                  tile(DK, seq_b), tile(DK, seq_b), tile(DV, seq_b), tile(DK, la_b)] + slabs,
        out_specs=[tile(DV, seq_f), tile(DV, seq_b)] + slabs,
        out_shape=[jax.ShapeDtypeStruct(((NT + 1) * TM, DV), BF16)] * 2
                  + [jax.ShapeDtypeStruct(w.shape, BF16) for w in streamed],
        scratch_shapes=[pltpu.VMEM((HEADS, HK, HV), F32)] * 2,
        compiler_params=_params(),
        name="gla_scan",
    )(q, k, v, la, q, k, v, la, *streamed)
    w1b, w3b, w2b = w1b.reshape(2, D, FFN), w3b.reshape(2, D, FFN), w2b.reshape(2, FFN, D)

    sub = lambda w, j: pl.BlockSpec((TM, w), lambda i: (jnp.minimum(SUBTILES * i + j, NT), 0))
    x_mid, y_br, u_all = pl.pallas_call(
        _gla_out_kernel,
        grid=(CTX_STEP + 1,),
        in_specs=[lat_pair, _const((CTX, D)), mod_spec(0), mod_spec(1),
                  sub(DV, 0), sub(DV, 1), sub(DV, 0), sub(DV, 1), sub(DV, 0), sub(DV, 1),
                  _const((1, HV)), _const((DV, D))] + ffn_specs(0) + [_const((1, D)), _const((D, 2 * D))],
        out_specs=[pair(D), pair(D), pair(D)],
        out_shape=[jax.ShapeDtypeStruct((nseq, D), F32), jax.ShapeDtypeStruct((nseq, D), BF16),
                   jax.ShapeDtypeStruct((nseq, D), F32)],
        compiler_params=_params(),
        name="gla_out",
    )(x2d, ctx2d, mods, mods, o_f, o_f, o_b, o_b, g, g,
      vec(gla_head_norm_g[0]), gwob, vec(norm_ffn_g[0]), w1b, w3b, w2b,
      vec(norm_mix_g[1]), lwinb)

    u3 = u_all.reshape(nseq // GRID_W, GRID_W, D)
    u_ctx = u_all[NT * TM:(NT + 1) * TM]
    uctx3 = jnp.pad(u_ctx[:, None, :], ((0, 0), (0, SUB - 1), (0, 0)))
    grp = lambda wg: jnp.maximum(wg - 1, 0)
    blk = (ROWS, SUB, LRU_BLOCK)
    gate_w = lambda d: (0.5 * jnp.concatenate([lru_gate_a_w[0, d], lru_gate_x_w[0, d]], axis=-1)).astype(BF16)
    gate_b = lambda d: 0.5 * jnp.stack([lru_gate_a_b[0, d], lru_gate_x_b[0, d]])
    chan = lambda rows: pl.BlockSpec((rows, LRU_BLOCK), lambda cb, wg: (0, cb))
    gate_spec = pl.BlockSpec((None, LRU_BLOCK, 2 * LRU_BLOCK), lambda cb, wg: (cb, 0, 0))
    last2 = ROWS // 2 - 1
    s0, pf, pb, sums = pl.pallas_call(
        _lru_scan_kernel,
        grid=(LRU_BLOCKS, NGRP + 1),
        in_specs=[pl.BlockSpec(blk, lambda cb, wg: (0, grp(wg), cb)),
                  pl.BlockSpec(blk, lambda cb, wg: (0, 0, cb)),
                  pl.BlockSpec((2, SUB, LRU_BLOCK), lambda cb, wg: (last2, jnp.maximum(wg - 2, 0), cb)),
                  pl.BlockSpec((2, SUB, LRU_BLOCK), lambda cb, wg: (0, jnp.minimum(wg, NGRP - 1), cb)),
                  chan(CONV_W), chan(1), gate_spec, gate_spec, chan(2), chan(2), chan(2)],
        out_specs=[pl.BlockSpec(blk, lambda cb, wg: (0, grp(wg), cb)),
                   pl.BlockSpec(blk, lambda cb, wg: (0, grp(wg), cb)),
                   pl.BlockSpec(blk, lambda cb, wg: (0, grp(wg), cb)),
                   pl.BlockSpec((4, SUB, LRU_BLOCK), lambda cb, wg: (0, wg, cb))],
        out_shape=[jax.ShapeDtypeStruct((ROWS, GRID_W, D), F32)] * 3
                  + [jax.ShapeDtypeStruct((4, (NGRP + 1) * SUB, D), F32)],
        scratch_shapes=[pltpu.VMEM((ROWS + CONV_W - 1, SUB, LRU_BLOCK), F32)],
        compiler_params=_params(2),
        name="lru_scan",
    )(u3, uctx3, u3, u3, lru_conv_w[0], vec(lru_conv_b[0]), gate_w(0), gate_w(1),
      gate_b(0), gate_b(1), lru_lambda[0])

    h_f, h_b = pl.pallas_call(
        _lru_chain_kernel,
        out_shape=[jax.ShapeDtypeStruct((GRID_W, D), F32)] * 2,
        name="lru_chain",
    )(sums)

    reps = OUT_TM // GRID_W
    lat = lambda i: (i, 0)
    big = lambda f: pl.BlockSpec((OUT_SUBTILES * OUT_TM, D), f)
    out = pl.pallas_call(
        _lru_out_kernel,
        grid=(SEQ // (OUT_SUBTILES * OUT_TM),),
        in_specs=[big(lat), _const((None, None, SUB, D), (1, 0, 0, 0)),
                  big(lat), big(lat), big(lat), big(lat),
                  _const((OUT_TM, D)), _const((OUT_TM, D)), _const((D, D))] + ffn_specs(1) + [_const((1, D))],
        out_specs=big(lat),
        out_shape=jax.ShapeDtypeStruct((SEQ, D), F32),
        compiler_params=_params(),
        name="lru_out",
    )(x_mid, mods, y_br, s0.reshape(SEQ, D), pf.reshape(SEQ, D), pb.reshape(SEQ, D),
      jnp.tile(h_f, (reps, 1)), jnp.tile(h_b, (reps, 1)),
      lwob, vec(norm_ffn_g[1]), w1b, w3b, w2b, vec(final_norm_g))
    return out.reshape(1, SEQ, D)
```

```python
import functools

import jax
import jax.numpy as jnp
from jax import lax
from jax.experimental import pallas as pl
from jax.experimental.pallas import tpu as pltpu

D = 1024
SEQ = 16384
GRID_W = 64
ROWS = SEQ // GRID_W
CTX = 256
TM = 256
NT = SEQ // TM
HEADS = 4
DK = 512
DV = 1024
HK = DK // HEADS
HV = DV // HEADS
RANK = 16
TAU = 16.0
CHUNK = 64
NCH = TM // CHUNK
LRU_BLOCKS = 4
LRU_BLOCK = D // LRU_BLOCKS
LRU_C = 8.0
FFN = 2816
EPS = 1e-6
SUB = 8
BF16_SUBLANES = 16
N_STREAMED_WEIGHTS = 6
CONV_W = 4
CONV_LEFT = 2
NGRP = GRID_W // SUB
SCAN_ROWS = 64
CTX_SEG = CTX // SUB
SUBTILES = 2
CTX_STEP = NT // SUBTILES
OUT_TM = 256
OUT_SUBTILES = 2

F32 = jnp.float32
BF16 = jnp.bfloat16

V7X_VMEM_LIMIT = 56 * 1024 * 1024
LOG2E = 1.4426950408889634
TINY = 1e-30


def _dot(a, b):
    return jnp.dot(a, b, preferred_element_type=F32)


def _dot_nt(a, b):
    return lax.dot_general(a, b, (((1,), (1,)), ((), ())), preferred_element_type=F32)


def _dot_tn(a, b):
    return lax.dot_general(a, b, (((0,), (0,)), ((), ())), preferred_element_type=F32)


def _rms(x, g):
    ms = jnp.mean(x * x, axis=-1, keepdims=True)
    return x * lax.rsqrt(ms + EPS) * g


def _rms_mod(x, g, scale, shift):
    return _rms(x, g) * (1.0 + scale) + shift


def _silu(x):
    return x * jax.nn.sigmoid(x)


def _split_bf16(x):
    hi = x.astype(BF16)
    lo = (x - hi.astype(F32)).astype(BF16)
    return hi, lo


def _mod_kernel(c_ref, w_ref, b_ref, o_ref):
    s = _silu(c_ref[...])
    w = w_ref[...]
    b = b_ref[...]
    o_ref[...] = jnp.zeros_like(o_ref)
    for r in range(2):
        o_ref[r:r + 1, :] = jnp.sum(w * s[:, r:r + 1], axis=0, keepdims=True) + b


def _modulation(ccols, w_mod, b_mod):
    depth = w_mod.shape[0]
    tn = 1536
    return pl.pallas_call(
        _mod_kernel,
        grid=(depth, 6 * D // tn),
        in_specs=[
            pl.BlockSpec((D, 2), lambda l, n: (0, 0)),
            pl.BlockSpec((None, D, tn), lambda l, n: (l, 0, n)),
            pl.BlockSpec((None, 1, tn), lambda l, n: (l, 0, n)),
        ],
        out_specs=pl.BlockSpec((None, SUB, tn), lambda l, n: (l, 0, n)),
        out_shape=jax.ShapeDtypeStruct((depth, SUB, 6 * D), F32),
        compiler_params=pltpu.CompilerParams(
            dimension_semantics=("arbitrary", "arbitrary"),
            vmem_limit_bytes=V7X_VMEM_LIMIT),
        name="mod",
    )(ccols, w_mod, b_mod.reshape(depth, 1, 6 * D))


def _gla_tiles(jobs):
    row = lax.broadcasted_iota(jnp.int32, (TM, TM), 0)
    col = lax.broadcasted_iota(jnp.int32, (TM, TM), 1)
    shift = CHUNK.bit_length() - 1
    same = jnp.right_shift(row, shift) == jnp.right_shift(col, shift)
    prep = []
    for q, k, v, la, s_ref, o_ref, rev in jobs:
        tri = same & ((col >= row) if rev else (col <= row))
        tri_b = tri.astype(BF16)
        la_hi, la_lo = _split_bf16(la)
        b = _dot(tri_b, la_hi) + _dot(tri_b, la_lo)
        prep.append((tri, b))
    work = []
    for (q, k, v, la, s_ref, o_ref, rev), (tri, b) in zip(jobs, prep):
        ends = [c * CHUNK if rev else (c + 1) * CHUNK - 1 for c in range(NCH)]
        bl = jnp.concatenate([jnp.broadcast_to(b[r:r + 1], (CHUNK, DK)) for r in ends], axis=0)
        q_dec = (q * jnp.exp(b)).astype(BF16)
        k_inv = (k * jnp.exp(-b)).astype(BF16)
        k_end = (k * jnp.exp(bl - b)).astype(BF16)
        totals = jnp.concatenate([b[r:r + 1] for r in ends] + [jnp.zeros((SUB - NCH, DK), F32)], axis=0)
        decay_t = jnp.exp(totals).T
        order = list(range(NCH - 1, -1, -1) if rev else range(NCH))
        for h in range(HEADS):
            ks = slice(h * HK, (h + 1) * HK)
            vs = slice(h * HV, (h + 1) * HV)
            work.append(dict(tri=tri, qd=q_dec[:, ks], ki=k_inv[:, ks], ke=k_end[:, ks], vh=v[:, vs],
                             dec=decay_t[ks], order=order, s_ref=s_ref, h=h, o_ref=o_ref, vs=vs))
    for w in work:
        w["att"] = jnp.where(w["tri"], _dot_nt(w["qd"], w["ki"]), 0.0).astype(BF16)
    for w in work:
        w["o_intra"] = _dot(w["att"], w["vh"])
        w["st"] = w["s_ref"][w["h"]]
    for idx in range(NCH):
        for w in work:
            c = w["order"][idx]
            rs = slice(c * CHUNK, (c + 1) * CHUNK)
            o_inter = _dot(w["qd"][rs], w["st"].astype(BF16))
            w["o_ref"][rs, w["vs"]] = (w["o_intra"][rs] + o_inter).astype(w["o_ref"].dtype)
            u = _dot_tn(w["ke"][rs], w["vh"][rs])
            w["st"] = w["dec"][:, c:c + 1] * w["st"] + u
    for w in work:
        w["s_ref"][w["h"]] = w["st"]


def _proj_store(rs, z, zg, q_ref, k_ref, v_ref, g_ref, la_ref):
    q_ref[rs] = (z[:, :DK] * (HK ** -0.5)).astype(BF16)
    k_ref[rs] = z[:, DK:2 * DK].astype(BF16)
    v_ref[rs] = z[:, 2 * DK:2 * DK + DV].astype(BF16)
    g_ref[rs] = z[:, 2 * DK + DV:].astype(BF16)
    la_ref[rs] = (jnp.minimum(zg, 0.0) - jnp.log(1.0 + jnp.exp(-jnp.abs(zg)))) * (1.0 / TAU)


def _gla_proj_kernel(x_ref, ctx_ref, mod_ref, gmix_ref, win_ref, gw1_ref, gw2_ref, gb_ref,
                     q_ref, k_ref, v_ref, g_ref, la_ref):
    is_ctx = pl.program_id(0) == CTX_STEP
    mod = mod_ref[...]
    rows = [slice(j * TM, (j + 1) * TM) for j in range(SUBTILES)]
    a = [_rms_mod(jnp.where(is_ctx, ctx_ref[...], x_ref[rs]), gmix_ref[...], mod[1:2], mod[0:1]).astype(BF16)
         for rs in rows]
    pending = None
    for rs, at in zip(rows, a):
        z = _dot(at, win_ref[...])
        t = _dot(at, gw1_ref[...]).astype(BF16)
        zg = _dot(t, gw2_ref[...]) + gb_ref[...]
        if pending is not None:
            _proj_store(*pending, q_ref, k_ref, v_ref, g_ref, la_ref)
        pending = (rs, z, zg)
    _proj_store(*pending, q_ref, k_ref, v_ref, g_ref, la_ref)


def _gla_scan_kernel(qf_ref, kf_ref, vf_ref, laf_ref, qb_ref, kb_ref, vb_ref, lab_ref, *rest):
    nw = N_STREAMED_WEIGHTS
    w_in, (of_ref, ob_ref), w_out, (sf_ref, sb_ref) = (
        rest[:nw], rest[nw:nw + 2], rest[nw + 2:2 * nw + 2], rest[2 * nw + 2:])

    @pl.when(pl.program_id(0) == 0)
    def _():
        sf_ref[...] = jnp.zeros_like(sf_ref)
        sb_ref[...] = jnp.zeros_like(sb_ref)

    _gla_tiles([
        (qf_ref[...].astype(F32), kf_ref[...].astype(F32), vf_ref[...], laf_ref[...], sf_ref, of_ref, False),
        (qb_ref[...].astype(F32), kb_ref[...].astype(F32), vb_ref[...], lab_ref[...], sb_ref, ob_ref, True)])
    for src, dst in zip(w_in, w_out):
        dst[...] = src[...].astype(BF16)


def _ffn_stages(x1, mod, gffn, w1_ref, w3_ref, w2_ref):
    f = [_rms_mod(t, gffn, mod[4:5], mod[3:4]).astype(BF16) for t in x1]
    p = [(_silu(_dot(t, w1_ref[...])) * _dot(t, w3_ref[...])).astype(BF16) for t in f]
    return [t + mod[5:6] * _dot(pt, w2_ref[...]) for t, pt in zip(x1, p)]


def _gla_out_kernel(x_ref, ctx_ref, mod0_ref, mod1_ref, of0_ref, of1_ref, ob0_ref, ob1_ref,
                    g0_ref, g1_ref,
                    hn_ref, wout_ref, gffn_ref, w1_ref, w3_ref, w2_ref, gmix1_ref, lwin_ref,
                    xo_ref, y_ref, u_ref):
    is_ctx = pl.program_id(0) == CTX_STEP
    hn = hn_ref[...]
    mod0 = mod0_ref[...]
    mod1 = mod1_ref[...]
    subtiles = ((of0_ref, ob0_ref, g0_ref), (of1_ref, ob1_ref, g1_ref))
    rows = [slice(j * TM, (j + 1) * TM) for j in range(SUBTILES)]
    gated = []
    for of_ref, ob_ref, g_ref in subtiles:
        o = of_ref[...].astype(F32) + ob_ref[...].astype(F32)
        parts = [_rms(o[:, h * HV:(h + 1) * HV], hn) for h in range(HEADS)]
        on = jnp.concatenate(parts, axis=-1)
        gated.append((on * _silu(g_ref[...].astype(F32))).astype(BF16))
    x1 = [jnp.where(is_ctx, ctx_ref[...], x_ref[rs]) + mod0[2:3] * _dot(gt, wout_ref[...])
          for rs, gt in zip(rows, gated)]
    x2 = _ffn_stages(x1, mod0, gffn_ref[...], w1_ref, w3_ref, w2_ref)
    a = []
    for rs, t in zip(rows, x2):
        xo_ref[rs] = t
        a.append(_rms_mod(t, gmix1_ref[...], mod1[1:2], mod1[0:1]).astype(BF16))
    z = [_dot(t, lwin_ref[...]) for t in a]
    for rs, t in zip(rows, z):
        y_ref[rs] = jax.nn.gelu(t[:, :D]).astype(BF16)
        u_ref[rs] = t[:, D:]


def _lru_coeffs(ub, uch, wg_ref, bg_ref, c2, rows):
    zz = _dot(ub, wg_ref[...])
    bg = bg_ref[...]
    ta = jnp.tanh(zz[:, :LRU_BLOCK] + bg[0:1])
    tx = jnp.tanh(zz[:, LRU_BLOCK:] + bg[1:2])
    a = jnp.exp2(c2 * ta + c2)
    v = 1.0 - a * a
    s = v * lax.rsqrt(jnp.maximum(v, TINY))
    b = s * ((tx + 1.0) * uch)
    shape = (rows, SUB, LRU_BLOCK)
    return a.reshape(shape), b.reshape(shape)


def _lru_scan_kernel(u_ref, hp_ref, hn_ref, cw_ref, cb_ref, wg0_ref, wg1_ref,
                     bg0_ref, bg1_ref, lam_ref,
                     s0_ref, pf_ref, pb_ref, sum_ref,
                     e_ref, *, rows, blk_rows, ngroups):
    wg = pl.program_id(1)
    u = u_ref[...]
    prev_ok = wg >= 1
    next_ok = wg <= ngroups - 2
    sub = lax.broadcasted_iota(jnp.int32, (SUB, LRU_BLOCK), 0)
    hp = hp_ref[...]
    hn = hn_ref[...]

    def from_prev_column(cur, halo):
        edge = jnp.where(prev_ok, pltpu.roll(halo, 1, 0), 0.0)
        return jnp.where(sub == 0, edge, pltpu.roll(cur, 1, 0))

    e_ref[CONV_LEFT:CONV_LEFT + rows] = u
    e_ref[0] = from_prev_column(u[rows - 2], hp[0])
    e_ref[1] = from_prev_column(u[rows - 1], hp[1])
    edge = jnp.where(next_ok, pltpu.roll(hn[0], SUB - 1, 0), 0.0)
    e_ref[CONV_LEFT + rows] = jnp.where(sub == SUB - 1, edge, pltpu.roll(u[0], SUB - 1, 0))
    cw = 0.5 * cw_ref[...]
    cb = 0.5 * cb_ref[...]
    lam = lam_ref[...]
    c2 = (-0.5 * LRU_C * LOG2E) * jax.nn.softplus(-lam)
    nblk = rows // blk_rows
    ab = [[None] * nblk, [None] * nblk]
    for k in range(nblk):
        for d, blk in ((0, k), (1, nblk - 1 - k)):
            r0 = blk * blk_rows
            uch = cb + cw[0:1] * e_ref[r0:r0 + blk_rows]
            for j in range(1, CONV_W):
                uch = uch + cw[j:j + 1] * e_ref[r0 + j:r0 + j + blk_rows]
            uch = uch.reshape(blk_rows * SUB, LRU_BLOCK)
            ab[d][blk] = _lru_coeffs(uch.astype(BF16), uch, (wg0_ref, wg1_ref)[d], (bg0_ref, bg1_ref)[d],
                                     c2[d:d + 1], blk_rows)

    h = jnp.zeros((SUB, LRU_BLOCK), F32)
    p = jnp.ones((SUB, LRU_BLOCK), F32)
    for r in range(rows):
        a, b = ab[0][r // blk_rows]
        h = a[r % blk_rows] * h + b[r % blk_rows]
        p = a[r % blk_rows] * p
        s0_ref[r] = h
        pf_ref[r] = p
    sum_ref[0] = p
    sum_ref[1] = h
    h = jnp.zeros((SUB, LRU_BLOCK), F32)
    p = jnp.ones((SUB, LRU_BLOCK), F32)
    for r in range(rows - 1, -1, -1):
        a, b = ab[1][r // blk_rows]
        h = a[r % blk_rows] * h + b[r % blk_rows]
        p = a[r % blk_rows] * p
        s0_ref[r] = s0_ref[r] + h
        pb_ref[r] = p
    sum_ref[2] = p
    sum_ref[3] = h


def _lru_chain_kernel(csum_ref, sum_ref, hf_ref, hb_ref):
    s = jnp.zeros((1, D), F32)
    for w in range(SUB):
        s = csum_ref[0, w:w + 1, :] * s + csum_ref[1, w:w + 1, :]
    for w in range(GRID_W):
        hf_ref[w:w + 1, :] = s
        s = sum_ref[0, w:w + 1, :] * s + sum_ref[1, w:w + 1, :]
    s = jnp.zeros((1, D), F32)
    for w in range(SUB - 1, -1, -1):
        s = csum_ref[2, w:w + 1, :] * s + csum_ref[3, w:w + 1, :]
    for w in range(GRID_W - 1, -1, -1):
        hb_ref[w:w + 1, :] = s
        s = sum_ref[2, w:w + 1, :] * s + sum_ref[3, w:w + 1, :]


def _lru_out_kernel(x_ref, mod_ref, y_ref, s0_ref, pf_ref, pb_ref, hf_ref, hb_ref,
                    wout_ref, gffn_ref, w1_ref, w3_ref, w2_ref, gfin_ref, o_ref):
    mod = mod_ref[...]
    rows = [slice(j * OUT_TM, (j + 1) * OUT_TM) for j in range(OUT_SUBTILES)]
    ym = []
    for rs in rows:
        hs = s0_ref[rs] + pf_ref[rs] * hf_ref[...] + pb_ref[rs] * hb_ref[...]
        ym.append((hs * y_ref[rs].astype(F32)).astype(BF16))
    x1 = [x_ref[rs] + mod[2:3] * _dot(t, wout_ref[...]) for rs, t in zip(rows, ym)]
    x2 = _ffn_stages(x1, mod, gffn_ref[...], w1_ref, w3_ref, w2_ref)
    for rs, t in zip(rows, x2):
        o_ref[rs] = _rms(t, gfin_ref[...])


def _const(shape, index=None):
    nd = len(shape)
    index = (0,) * nd if index is None else index
    return pl.BlockSpec(shape, lambda *_: index, pipeline_mode=pl.Buffered(1))


def _params(n_axes=1):
    return pltpu.CompilerParams(dimension_semantics=("arbitrary",) * n_axes,
                                vmem_limit_bytes=V7X_VMEM_LIMIT)


def kernel(x, c, ctx, c_ctx, norm_mix_g, norm_ffn_g, w_mod, b_mod, gla_w_in, gla_gate_w1, gla_gate_w2, gla_gate_b, gla_head_norm_g, gla_w_out, lru_w_in, lru_conv_w, lru_conv_b, lru_gate_a_w, lru_gate_a_b, lru_gate_x_w, lru_gate_x_b, lru_lambda, lru_w_out, ffn_w1, ffn_w3, ffn_w2, final_norm_g):
    assert x.shape == (1, SEQ, D) and ctx.shape == (1, CTX, D) and CTX == TM == ROWS
    x2d = x.reshape(SEQ, D)
    ctx2d = ctx.reshape(CTX, D)
    nseq = (NT + SUBTILES) * TM

    m = _modulation(jnp.stack([c[0], c_ctx], axis=1), w_mod, b_mod)
    mods = m[:, :2].reshape(2, 2, 6, D)
    mods = jnp.pad(mods, ((0, 0), (0, 0), (0, SUB - 6), (0, 0)))

    def mod_spec(layer):
        return pl.BlockSpec((None, None, SUB, D), lambda i: (layer, jnp.where(i == CTX_STEP, 1, 0), 0, 0))

    vec = lambda t: t.reshape(1, -1)
    tile = lambda w, f: pl.BlockSpec((TM, w), f)

    def ffn_specs(layer):
        return [_const((1, D)), _const((None, D, FFN), (layer, 0, 0)),
                _const((None, D, FFN), (layer, 0, 0)), _const((None, FFN, D), (layer, 0, 0))]

    win = gla_w_in[0].astype(BF16)
    gw1 = jnp.concatenate([gla_gate_w1[0, 0], gla_gate_w1[0, 1]], axis=1)
    gw1 = jnp.pad(gw1, ((0, 0), (0, 128 - 2 * RANK))).astype(BF16)
    gw2 = jnp.zeros((128, 2 * DK), F32)
    gw2 = gw2.at[:RANK, :DK].set(gla_gate_w2[0, 0]).at[RANK:2 * RANK, DK:].set(gla_gate_w2[0, 1])
    gw2 = gw2.astype(BF16)
    gb = gla_gate_b[0].reshape(1, 2 * DK)

    pair = lambda w: pl.BlockSpec((SUBTILES * TM, w), lambda i: (i, 0))
    lat_pair = pl.BlockSpec((SUBTILES * TM, D), lambda i: (jnp.minimum(i, CTX_STEP - 1), 0))
    q, k, v, g, la = pl.pallas_call(
        _gla_proj_kernel,
        grid=(CTX_STEP + 1,),
        in_specs=[lat_pair, _const((CTX, D)), mod_spec(0), _const((1, D)),
                  _const((D, 2 * DK + 2 * DV)), _const((D, 128)), _const((128, 2 * DK)),
                  _const((1, 2 * DK))],
        out_specs=[pair(DK), pair(DK), pair(DV), pair(DV), pair(2 * DK)],
        out_shape=[jax.ShapeDtypeStruct((nseq, DK), BF16), jax.ShapeDtypeStruct((nseq, DK), BF16),
                   jax.ShapeDtypeStruct((nseq, DV), BF16), jax.ShapeDtypeStruct((nseq, DV), BF16),
                   jax.ShapeDtypeStruct((nseq, 2 * DK), F32)],
        compiler_params=_params(),
        name="gla_proj",
    )(x2d, ctx2d, mods, vec(norm_mix_g[0]), win, gw1, gw2, gb)

    seq_f = lambda i: (jnp.where(i == 0, NT, i - 1), 0)
    seq_b = lambda i: (jnp.where(i == 0, NT, NT - i), 0)
    streamed = [ffn_w1.reshape(2 * D, FFN), ffn_w3.reshape(2 * D, FFN), ffn_w2.reshape(2 * FFN, D),
                gla_w_out[0], lru_w_in[0], lru_w_out[0]]
    assert len(streamed) == N_STREAMED_WEIGHTS

    def slab_spec(w):
        nblk = NT if w.shape[0] % (NT * BF16_SUBLANES) == 0 else NT // 2
        rows = w.shape[0] // nblk
        assert rows * nblk == w.shape[0] and rows % BF16_SUBLANES == 0
        return pl.BlockSpec((rows, w.shape[1]), lambda i: (jnp.minimum(i, nblk - 1), 0))

    slabs = [slab_spec(w) for w in streamed]
    la_b = lambda i: (seq_b(i)[0], 1)
    o_f, o_b, w1b, w3b, w2b, gwob, lwinb, lwob = pl.pallas_call(
        _gla_scan_kernel,
        grid=(NT + 1,),
        in_specs=[tile(DK, seq_f), tile(DK, seq_f), tile(DV, seq_f), tile(DK, seq_f),
                  tile(DK, seq_b), tile(DK, seq_b), tile(DV, seq_b), tile(DK, la_b)] + slabs,
        out_specs=[tile(DV, seq_f), tile(DV, seq_b)] + slabs,
        out_shape=[jax.ShapeDtypeStruct(((NT + 1) * TM, DV), BF16)] * 2
                  + [jax.ShapeDtypeStruct(w.shape, BF16) for w in streamed],
        scratch_shapes=[pltpu.VMEM((HEADS, HK, HV), F32)] * 2,
        compiler_params=_params(),
        name="gla_scan",
    )(q, k, v, la, q, k, v, la, *streamed)
    w1b, w3b, w2b = w1b.reshape(2, D, FFN), w3b.reshape(2, D, FFN), w2b.reshape(2, FFN, D)

    sub = lambda w, j: pl.BlockSpec((TM, w), lambda i: (jnp.minimum(SUBTILES * i + j, NT), 0))
    x_mid, y_br, u_all = pl.pallas_call(
        _gla_out_kernel,
        grid=(CTX_STEP + 1,),
        in_specs=[lat_pair, _const((CTX, D)), mod_spec(0), mod_spec(1),
                  sub(DV, 0), sub(DV, 1), sub(DV, 0), sub(DV, 1), sub(DV, 0), sub(DV, 1),
                  _const((1, HV)), _const((DV, D))] + ffn_specs(0) + [_const((1, D)), _const((D, 2 * D))],
        out_specs=[pair(D), pair(D), pair(D)],
        out_shape=[jax.ShapeDtypeStruct((nseq, D), F32), jax.ShapeDtypeStruct((nseq, D), BF16),
                   jax.ShapeDtypeStruct((nseq, D), F32)],
        compiler_params=_params(),
        name="gla_out",
    )(x2d, ctx2d, mods, mods, o_f, o_f, o_b, o_b, g, g,
      vec(gla_head_norm_g[0]), gwob, vec(norm_ffn_g[0]), w1b, w3b, w2b,
      vec(norm_mix_g[1]), lwinb)

    u3 = u_all.reshape(nseq // GRID_W, GRID_W, D)
    uctx3 = u_all[NT * TM:(NT + 1) * TM].reshape(SUB, CTX_SEG, D).transpose(1, 0, 2)
    gate_w = lambda d: jnp.concatenate([lru_gate_a_w[0, d], lru_gate_x_w[0, d]], axis=-1).astype(BF16)
    gate_b = lambda d: 0.5 * jnp.stack([lru_gate_a_b[0, d], lru_gate_x_b[0, d]])
    chan = lambda rows: pl.BlockSpec((rows, LRU_BLOCK), lambda cb, wg: (0, cb))
    gate_spec = pl.BlockSpec((None, LRU_BLOCK, 2 * LRU_BLOCK), lambda cb, wg: (cb, 0, 0))
    lru_weights = (lru_conv_w[0], vec(lru_conv_b[0]), gate_w(0), gate_w(1), gate_b(0), gate_b(1),
                   lru_lambda[0])

    def column_scans(u3d, rows, blk_rows, ngroups):
        blk = (rows, SUB, LRU_BLOCK)
        here = pl.BlockSpec(blk, lambda cb, wg: (0, wg, cb))
        return pl.pallas_call(
            functools.partial(_lru_scan_kernel, rows=rows, blk_rows=blk_rows, ngroups=ngroups),
            grid=(LRU_BLOCKS, ngroups),
            in_specs=[here,
                      pl.BlockSpec((2, SUB, LRU_BLOCK), lambda cb, wg: (rows // 2 - 1, jnp.maximum(wg - 1, 0), cb)),
                      pl.BlockSpec((2, SUB, LRU_BLOCK), lambda cb, wg: (0, jnp.minimum(wg + 1, ngroups - 1), cb)),
                      chan(CONV_W), chan(1), gate_spec, gate_spec, chan(2), chan(2), chan(2)],
            out_specs=[here, here, here, pl.BlockSpec((4, SUB, LRU_BLOCK), lambda cb, wg: (0, wg, cb))],
            out_shape=[jax.ShapeDtypeStruct((rows, ngroups * SUB, D), F32)] * 3
                      + [jax.ShapeDtypeStruct((4, ngroups * SUB, D), F32)],
            scratch_shapes=[pltpu.VMEM((rows + CONV_W - 1, SUB, LRU_BLOCK), F32)],
            compiler_params=_params(2),
            name=f"lru_scan_{ngroups * SUB}x{rows}",
        )(u3d, u3d, u3d, *lru_weights)

    ctx_sums = column_scans(uctx3, CTX_SEG, CTX_SEG, 1)[3]
    s0, pf, pb, sums = column_scans(u3, ROWS, SCAN_ROWS, NGRP)

    h_f, h_b = pl.pallas_call(
        _lru_chain_kernel,
        out_shape=[jax.ShapeDtypeStruct((GRID_W, D), F32)] * 2,
        name="lru_chain",
    )(ctx_sums, sums)

    reps = OUT_TM // GRID_W
    lat = lambda i: (i, 0)
    big = lambda f: pl.BlockSpec((OUT_SUBTILES * OUT_TM, D), f)
    out = pl.pallas_call(
        _lru_out_kernel,
        grid=(SEQ // (OUT_SUBTILES * OUT_TM),),
        in_specs=[big(lat), _const((None, None, SUB, D), (1, 0, 0, 0)),
                  big(lat), big(lat), big(lat), big(lat),
                  _const((OUT_TM, D)), _const((OUT_TM, D)), _const((D, D))] + ffn_specs(1) + [_const((1, D))],
        out_specs=big(lat),
        out_shape=jax.ShapeDtypeStruct((SEQ, D), F32),
        compiler_params=_params(),
        name="lru_out",
    )(x_mid, mods, y_br, s0.reshape(SEQ, D), pf.reshape(SEQ, D), pb.reshape(SEQ, D),
      jnp.tile(h_f, (reps, 1)), jnp.tile(h_b, (reps, 1)),
      lwob, vec(norm_ffn_g[1]), w1b, w3b, w2b, vec(final_norm_g))
    return out.reshape(1, SEQ, D)
```

```python
import functools

import jax
import jax.numpy as jnp
from jax import lax
from jax.experimental import pallas as pl
from jax.experimental.pallas import tpu as pltpu

D = 1024
SEQ = 16384
GRID_W = 64
ROWS = SEQ // GRID_W
CTX = 256
TM = 256
NT = SEQ // TM
HEADS = 4
DK = 512
DV = 1024
HK = DK // HEADS
HV = DV // HEADS
RANK = 16
TAU = 16.0
CHUNK = 64
NCH = TM // CHUNK
LRU_BLOCKS = 4
LRU_BLOCK = D // LRU_BLOCKS
LRU_C = 8.0
FFN = 2816
EPS = 1e-6
SUB = 8
BF16_SUBLANES = 16
N_STREAMED_WEIGHTS = 6
CONV_W = 4
CONV_LEFT = 2
NGRP = GRID_W // SUB
SCAN_ROWS = 64
CTX_SEG = CTX // SUB
SUBTILES = 2
CTX_STEP = NT // SUBTILES
OUT_TM = 256
OUT_SUBTILES = 2

F32 = jnp.float32
BF16 = jnp.bfloat16

V7X_VMEM_LIMIT = 56 * 1024 * 1024
LOG2E = 1.4426950408889634
TINY = 1e-30


def _dot(a, b):
    return jnp.dot(a, b, preferred_element_type=F32)


def _dot_nt(a, b):
    return lax.dot_general(a, b, (((1,), (1,)), ((), ())), preferred_element_type=F32)


def _dot_tn(a, b):
    return lax.dot_general(a, b, (((0,), (0,)), ((), ())), preferred_element_type=F32)


def _rms(x, g):
    ms = jnp.mean(x * x, axis=-1, keepdims=True)
    return x * lax.rsqrt(ms + EPS) * g


def _rms_mod(x, g, scale, shift):
    return _rms(x, g) * (1.0 + scale) + shift


def _silu(x):
    return x * jax.nn.sigmoid(x)


def _split_bf16(x):
    hi = x.astype(BF16)
    lo = (x - hi.astype(F32)).astype(BF16)
    return hi, lo


def _mod_kernel(c_ref, w_ref, b_ref, o_ref):
    s = _silu(c_ref[...])
    w = w_ref[...]
    b = b_ref[...]
    o_ref[...] = jnp.zeros_like(o_ref)
    for r in range(2):
        o_ref[r:r + 1, :] = jnp.sum(w * s[:, r:r + 1], axis=0, keepdims=True) + b


def _modulation(ccols, w_mod, b_mod):
    depth = w_mod.shape[0]
    tn = 1536
    return pl.pallas_call(
        _mod_kernel,
        grid=(depth, 6 * D // tn),
        in_specs=[
            pl.BlockSpec((D, 2), lambda l, n: (0, 0)),
            pl.BlockSpec((None, D, tn), lambda l, n: (l, 0, n)),
            pl.BlockSpec((None, 1, tn), lambda l, n: (l, 0, n)),
        ],
        out_specs=pl.BlockSpec((None, SUB, tn), lambda l, n: (l, 0, n)),
        out_shape=jax.ShapeDtypeStruct((depth, SUB, 6 * D), F32),
        compiler_params=pltpu.CompilerParams(
            dimension_semantics=("arbitrary", "arbitrary"),
            vmem_limit_bytes=V7X_VMEM_LIMIT),
        name="mod",
    )(ccols, w_mod, b_mod.reshape(depth, 1, 6 * D))


def _gla_tiles(jobs):
    row = lax.broadcasted_iota(jnp.int32, (TM, TM), 0)
    col = lax.broadcasted_iota(jnp.int32, (TM, TM), 1)
    shift = CHUNK.bit_length() - 1
    same = jnp.right_shift(row, shift) == jnp.right_shift(col, shift)
    prep = []
    for q, k, v, la, s_ref, o_ref, rev in jobs:
        tri = same & ((col >= row) if rev else (col <= row))
        tri_b = tri.astype(BF16)
        la_hi, la_lo = _split_bf16(la)
        b = _dot(tri_b, la_hi) + _dot(tri_b, la_lo)
        prep.append((tri, b))
    work = []
    for (q, k, v, la, s_ref, o_ref, rev), (tri, b) in zip(jobs, prep):
        ends = [c * CHUNK if rev else (c + 1) * CHUNK - 1 for c in range(NCH)]
        bl = jnp.concatenate([jnp.broadcast_to(b[r:r + 1], (CHUNK, DK)) for r in ends], axis=0)
        q_dec = (q * jnp.exp(b)).astype(BF16)
        k_inv = (k * jnp.exp(-b)).astype(BF16)
        k_end = (k * jnp.exp(bl - b)).astype(BF16)
        totals = jnp.concatenate([b[r:r + 1] for r in ends] + [jnp.zeros((SUB - NCH, DK), F32)], axis=0)
        decay_t = jnp.exp(totals).T
        order = list(range(NCH - 1, -1, -1) if rev else range(NCH))
        for h in range(HEADS):
            ks = slice(h * HK, (h + 1) * HK)
            vs = slice(h * HV, (h + 1) * HV)
            work.append(dict(tri=tri, qd=q_dec[:, ks], ki=k_inv[:, ks], ke=k_end[:, ks], vh=v[:, vs],
                             dec=decay_t[ks], order=order, s_ref=s_ref, h=h, o_ref=o_ref, vs=vs))
    for w in work:
        w["att"] = jnp.where(w["tri"], _dot_nt(w["qd"], w["ki"]), 0.0).astype(BF16)
    for w in work:
        w["o_intra"] = _dot(w["att"], w["vh"])
        w["st"] = w["s_ref"][w["h"]]
    for idx in range(NCH):
        for w in work:
            c = w["order"][idx]
            rs = slice(c * CHUNK, (c + 1) * CHUNK)
            o_inter = _dot(w["qd"][rs], w["st"].astype(BF16))
            w["o_ref"][rs, w["vs"]] = (w["o_intra"][rs] + o_inter).astype(w["o_ref"].dtype)
            u = _dot_tn(w["ke"][rs], w["vh"][rs])
            w["st"] = w["dec"][:, c:c + 1] * w["st"] + u
    for w in work:
        w["s_ref"][w["h"]] = w["st"]


def _proj_store(rs, z, zg, q_ref, k_ref, v_ref, g_ref, la_ref):
    q_ref[rs] = (z[:, :DK] * (HK ** -0.5)).astype(BF16)
    k_ref[rs] = z[:, DK:2 * DK].astype(BF16)
    v_ref[rs] = z[:, 2 * DK:2 * DK + DV].astype(BF16)
    g_ref[rs] = z[:, 2 * DK + DV:].astype(BF16)
    la_ref[rs] = (jnp.minimum(zg, 0.0) - jnp.log(1.0 + jnp.exp(-jnp.abs(zg)))) * (1.0 / TAU)


def _gla_proj_kernel(x_ref, ctx_ref, mod_ref, gmix_ref, win_ref, gw1_ref, gw2_ref, gb_ref,
                     q_ref, k_ref, v_ref, g_ref, la_ref):
    is_ctx = pl.program_id(0) == CTX_STEP
    mod = mod_ref[...]
    rows = [slice(j * TM, (j + 1) * TM) for j in range(SUBTILES)]
    a = [_rms_mod(jnp.where(is_ctx, ctx_ref[...], x_ref[rs]), gmix_ref[...], mod[1:2], mod[0:1]).astype(BF16)
         for rs in rows]
    pending = None
    for rs, at in zip(rows, a):
        z = _dot(at, win_ref[...])
        t = _dot(at, gw1_ref[...]).astype(BF16)
        zg = _dot(t, gw2_ref[...]) + gb_ref[...]
        if pending is not None:
            _proj_store(*pending, q_ref, k_ref, v_ref, g_ref, la_ref)
        pending = (rs, z, zg)
    _proj_store(*pending, q_ref, k_ref, v_ref, g_ref, la_ref)


def _gla_scan_kernel(qf_ref, kf_ref, vf_ref, laf_ref, qb_ref, kb_ref, vb_ref, lab_ref, *rest):
    nw = N_STREAMED_WEIGHTS
    w_in, (of_ref, ob_ref), w_out, (sf_ref, sb_ref) = (
        rest[:nw], rest[nw:nw + 2], rest[nw + 2:2 * nw + 2], rest[2 * nw + 2:])

    @pl.when(pl.program_id(0) == 0)
    def _():
        sf_ref[...] = jnp.zeros_like(sf_ref)
        sb_ref[...] = jnp.zeros_like(sb_ref)

    _gla_tiles([
        (qf_ref[...].astype(F32), kf_ref[...].astype(F32), vf_ref[...], laf_ref[...], sf_ref, of_ref, False),
        (qb_ref[...].astype(F32), kb_ref[...].astype(F32), vb_ref[...], lab_ref[...], sb_ref, ob_ref, True)])
    for src, dst in zip(w_in, w_out):
        dst[...] = src[...].astype(BF16)


def _ffn_stages(x1, mod, gffn, w1_ref, w3_ref, w2_ref):
    f = [_rms_mod(t, gffn, mod[4:5], mod[3:4]).astype(BF16) for t in x1]
    p = [(_silu(_dot(t, w1_ref[...])) * _dot(t, w3_ref[...])).astype(BF16) for t in f]
    return [t + mod[5:6] * _dot(pt, w2_ref[...]) for t, pt in zip(x1, p)]


def _gla_out_kernel(x_ref, ctx_ref, mod0_ref, mod1_ref, of0_ref, of1_ref, ob0_ref, ob1_ref,
                    g0_ref, g1_ref,
                    hn_ref, wout_ref, gffn_ref, w1_ref, w3_ref, w2_ref, gmix1_ref, lwin_ref,
                    xo_ref, y_ref, u_ref):
    is_ctx = pl.program_id(0) == CTX_STEP
    hn = hn_ref[...]
    mod0 = mod0_ref[...]
    mod1 = mod1_ref[...]
    subtiles = ((of0_ref, ob0_ref, g0_ref), (of1_ref, ob1_ref, g1_ref))
    rows = [slice(j * TM, (j + 1) * TM) for j in range(SUBTILES)]
    gffn = gffn_ref[...]
    v = [dict() for _ in subtiles]

    def gate(j):
        of_ref, ob_ref, g_ref = subtiles[j]
        y = None
        for h in range(HEADS):
            hs = slice(h * HV, (h + 1) * HV)
            o = of_ref[:, hs].astype(F32) + ob_ref[:, hs].astype(F32)
            gated = (_rms(o, hn) * _silu(g_ref[:, hs].astype(F32))).astype(BF16)
            part = _dot(gated, wout_ref[hs, :])
            y = part if y is None else y + part
        v[j]["y"] = y

    def mix_out(j):
        xt = jnp.where(is_ctx, ctx_ref[...], x_ref[rows[j]])
        v[j]["x1"] = xt + mod0[2:3] * v[j]["y"]
        v[j]["f"] = _rms_mod(v[j]["x1"], gffn, mod0[4:5], mod0[3:4]).astype(BF16)

    def ffn_up(j):
        v[j]["h1"] = _dot(v[j]["f"], w1_ref[...])
        v[j]["h3"] = _dot(v[j]["f"], w3_ref[...])

    def ffn_act(j):
        v[j]["p"] = (_silu(v[j]["h1"]) * v[j]["h3"]).astype(BF16)

    def ffn_down(j):
        x2 = v[j]["x1"] + mod0[5:6] * _dot(v[j]["p"], w2_ref[...])
        xo_ref[rows[j]] = x2
        v[j]["a"] = _rms_mod(x2, gmix1_ref[...], mod1[1:2], mod1[0:1]).astype(BF16)

    def lru_proj(j):
        v[j]["z"] = _dot(v[j]["a"], lwin_ref[...])

    def lru_store(j):
        y_ref[rows[j]] = jax.nn.gelu(v[j]["z"][:, :D]).astype(BF16)
        u_ref[rows[j]] = v[j]["z"][:, D:]

    for stage in (gate, mix_out, ffn_up, ffn_act, ffn_down, lru_proj, lru_store):
        for j in range(SUBTILES):
            stage(j)


def _lru_coeffs(ub, uch, wg_ref, bg_ref, c2, rows):
    zz = _dot(ub, wg_ref[...])
    bg = bg_ref[...]
    ta = jnp.tanh(zz[:, :LRU_BLOCK] + bg[0:1])
    tx = jnp.tanh(zz[:, LRU_BLOCK:] + bg[1:2])
    a = jnp.exp2(c2 * ta + c2)
    v = 1.0 - a * a
    s = v * lax.rsqrt(jnp.maximum(v, TINY))
    b = s * ((tx + 1.0) * uch)
    shape = (rows, SUB, LRU_BLOCK)
    return a.reshape(shape), b.reshape(shape)


def _lru_scan_kernel(u_ref, hp_ref, hn_ref, cw_ref, cb_ref, wg0_ref, wg1_ref,
                     bg0_ref, bg1_ref, lam_ref,
                     s0_ref, pf_ref, pb_ref, sum_ref,
                     e_ref, *, rows, blk_rows, ngroups):
    wg = pl.program_id(1)
    u = u_ref[...]
    prev_ok = wg >= 1
    next_ok = wg <= ngroups - 2
    sub = lax.broadcasted_iota(jnp.int32, (SUB, LRU_BLOCK), 0)
    hp = hp_ref[...]
    hn = hn_ref[...]

    def from_prev_column(cur, halo):
        edge = jnp.where(prev_ok, pltpu.roll(halo, 1, 0), 0.0)
        return jnp.where(sub == 0, edge, pltpu.roll(cur, 1, 0))

    e_ref[CONV_LEFT:CONV_LEFT + rows] = u
    e_ref[0] = from_prev_column(u[rows - 2], hp[0])
    e_ref[1] = from_prev_column(u[rows - 1], hp[1])
    edge = jnp.where(next_ok, pltpu.roll(hn[0], SUB - 1, 0), 0.0)
    e_ref[CONV_LEFT + rows] = jnp.where(sub == SUB - 1, edge, pltpu.roll(u[0], SUB - 1, 0))
    cw = 0.5 * cw_ref[...]
    cb = 0.5 * cb_ref[...]
    lam = lam_ref[...]
    c2 = (-0.5 * LRU_C * LOG2E) * jax.nn.softplus(-lam)
    nblk = rows // blk_rows
    ab = [[None] * nblk, [None] * nblk]
    for k in range(nblk):
        for d, blk in ((0, k), (1, nblk - 1 - k)):
            r0 = blk * blk_rows
            uch = cb + cw[0:1] * e_ref[r0:r0 + blk_rows]
            for j in range(1, CONV_W):
                uch = uch + cw[j:j + 1] * e_ref[r0 + j:r0 + j + blk_rows]
            uch = uch.reshape(blk_rows * SUB, LRU_BLOCK)
            ab[d][blk] = _lru_coeffs(uch.astype(BF16), uch, (wg0_ref, wg1_ref)[d], (bg0_ref, bg1_ref)[d],
                                     c2[d:d + 1], blk_rows)

    h = jnp.zeros((SUB, LRU_BLOCK), F32)
    p = jnp.ones((SUB, LRU_BLOCK), F32)
    for r in range(rows):
        a, b = ab[0][r // blk_rows]
        h = a[r % blk_rows] * h + b[r % blk_rows]
        p = a[r % blk_rows] * p
        s0_ref[r] = h
        pf_ref[r] = p
    sum_ref[0] = p
    sum_ref[1] = h
    h = jnp.zeros((SUB, LRU_BLOCK), F32)
    p = jnp.ones((SUB, LRU_BLOCK), F32)
    for r in range(rows - 1, -1, -1):
        a, b = ab[1][r // blk_rows]
        h = a[r % blk_rows] * h + b[r % blk_rows]
        p = a[r % blk_rows] * p
        s0_ref[r] = s0_ref[r] + h
        pb_ref[r] = p
    sum_ref[2] = p
    sum_ref[3] = h


def _lru_chain_kernel(csum_ref, sum_ref, hf_ref, hb_ref):
    s = jnp.zeros((1, D), F32)
    for w in range(SUB):
        s = csum_ref[0, w:w + 1, :] * s + csum_ref[1, w:w + 1, :]
    for w in range(GRID_W):
        hf_ref[w:w + 1, :] = s
        s = sum_ref[0, w:w + 1, :] * s + sum_ref[1, w:w + 1, :]
    s = jnp.zeros((1, D), F32)
    for w in range(SUB - 1, -1, -1):
        s = csum_ref[2, w:w + 1, :] * s + csum_ref[3, w:w + 1, :]
    for w in range(GRID_W - 1, -1, -1):
        hb_ref[w:w + 1, :] = s
        s = sum_ref[2, w:w + 1, :] * s + sum_ref[3, w:w + 1, :]


def _lru_out_kernel(x_ref, mod_ref, y_ref, s0_ref, pf_ref, pb_ref, hf_ref, hb_ref,
                    wout_ref, gffn_ref, w1_ref, w3_ref, w2_ref, gfin_ref, o_ref):
    mod = mod_ref[...]
    rows = [slice(j * OUT_TM, (j + 1) * OUT_TM) for j in range(OUT_SUBTILES)]
    x1 = []
    for rs in rows:
        yl = None
        for c in range(LRU_BLOCKS):
            cs = slice(c * LRU_BLOCK, (c + 1) * LRU_BLOCK)
            hs = s0_ref[rs, cs] + pf_ref[rs, cs] * hf_ref[:, cs] + pb_ref[rs, cs] * hb_ref[:, cs]
            ym = (hs * y_ref[rs, cs].astype(F32)).astype(BF16)
            part = _dot(ym, wout_ref[cs, :])
            yl = part if yl is None else yl + part
        x1.append(x_ref[rs] + mod[2:3] * yl)
    x2 = _ffn_stages(x1, mod, gffn_ref[...], w1_ref, w3_ref, w2_ref)
    for rs, t in zip(rows, x2):
        o_ref[rs] = _rms(t, gfin_ref[...])


def _const(shape, index=None):
    nd = len(shape)
    index = (0,) * nd if index is None else index
    return pl.BlockSpec(shape, lambda *_: index, pipeline_mode=pl.Buffered(1))


def _params(n_axes=1):
    return pltpu.CompilerParams(dimension_semantics=("arbitrary",) * n_axes,
                                vmem_limit_bytes=V7X_VMEM_LIMIT)


def kernel(x, c, ctx, c_ctx, norm_mix_g, norm_ffn_g, w_mod, b_mod, gla_w_in, gla_gate_w1, gla_gate_w2, gla_gate_b, gla_head_norm_g, gla_w_out, lru_w_in, lru_conv_w, lru_conv_b, lru_gate_a_w, lru_gate_a_b, lru_gate_x_w, lru_gate_x_b, lru_lambda, lru_w_out, ffn_w1, ffn_w3, ffn_w2, final_norm_g):
    assert x.shape == (1, SEQ, D) and ctx.shape == (1, CTX, D) and CTX == TM == ROWS
    x2d = x.reshape(SEQ, D)
    ctx2d = ctx.reshape(CTX, D)
    nseq = (NT + SUBTILES) * TM

    m = _modulation(jnp.stack([c[0], c_ctx], axis=1), w_mod, b_mod)
    mods = m[:, :2].reshape(2, 2, 6, D)
    mods = jnp.pad(mods, ((0, 0), (0, 0), (0, SUB - 6), (0, 0)))

    def mod_spec(layer):
        return pl.BlockSpec((None, None, SUB, D), lambda i: (layer, jnp.where(i == CTX_STEP, 1, 0), 0, 0))

    vec = lambda t: t.reshape(1, -1)
    tile = lambda w, f: pl.BlockSpec((TM, w), f)

    def ffn_specs(layer):
        return [_const((1, D)), _const((None, D, FFN), (layer, 0, 0)),
                _const((None, D, FFN), (layer, 0, 0)), _const((None, FFN, D), (layer, 0, 0))]

    win = gla_w_in[0].astype(BF16)
    gw1 = jnp.concatenate([gla_gate_w1[0, 0], gla_gate_w1[0, 1]], axis=1)
    gw1 = jnp.pad(gw1, ((0, 0), (0, 128 - 2 * RANK))).astype(BF16)
    gw2 = jnp.zeros((128, 2 * DK), F32)
    gw2 = gw2.at[:RANK, :DK].set(gla_gate_w2[0, 0]).at[RANK:2 * RANK, DK:].set(gla_gate_w2[0, 1])
    gw2 = gw2.astype(BF16)
    gb = gla_gate_b[0].reshape(1, 2 * DK)

    pair = lambda w: pl.BlockSpec((SUBTILES * TM, w), lambda i: (i, 0))
    lat_pair = pl.BlockSpec((SUBTILES * TM, D), lambda i: (jnp.minimum(i, CTX_STEP - 1), 0))
    q, k, v, g, la = pl.pallas_call(
        _gla_proj_kernel,
        grid=(CTX_STEP + 1,),
        in_specs=[lat_pair, _const((CTX, D)), mod_spec(0), _const((1, D)),
                  _const((D, 2 * DK + 2 * DV)), _const((D, 128)), _const((128, 2 * DK)),
                  _const((1, 2 * DK))],
        out_specs=[pair(DK), pair(DK), pair(DV), pair(DV), pair(2 * DK)],
        out_shape=[jax.ShapeDtypeStruct((nseq, DK), BF16), jax.ShapeDtypeStruct((nseq, DK), BF16),
                   jax.ShapeDtypeStruct((nseq, DV), BF16), jax.ShapeDtypeStruct((nseq, DV), BF16),
                   jax.ShapeDtypeStruct((nseq, 2 * DK), F32)],
        compiler_params=_params(),
        name="gla_proj",
    )(x2d, ctx2d, mods, vec(norm_mix_g[0]), win, gw1, gw2, gb)

    seq_f = lambda i: (jnp.where(i == 0, NT, i - 1), 0)
    seq_b = lambda i: (jnp.where(i == 0, NT, NT - i), 0)
    streamed = [ffn_w1.reshape(2 * D, FFN), ffn_w3.reshape(2 * D, FFN), ffn_w2.reshape(2 * FFN, D),
                gla_w_out[0], lru_w_in[0], lru_w_out[0]]
    assert len(streamed) == N_STREAMED_WEIGHTS

    def slab_spec(w):
        nblk = NT if w.shape[0] % (NT * BF16_SUBLANES) == 0 else NT // 2
        rows = w.shape[0] // nblk
        assert rows * nblk == w.shape[0] and rows % BF16_SUBLANES == 0
        return pl.BlockSpec((rows, w.shape[1]), lambda i: (jnp.minimum(i, nblk - 1), 0))

    slabs = [slab_spec(w) for w in streamed]
    la_b = lambda i: (seq_b(i)[0], 1)
    o_f, o_b, w1b, w3b, w2b, gwob, lwinb, lwob = pl.pallas_call(
        _gla_scan_kernel,
        grid=(NT + 1,),
        in_specs=[tile(DK, seq_f), tile(DK, seq_f), tile(DV, seq_f), tile(DK, seq_f),
                  tile(DK, seq_b), tile(DK, seq_b), tile(DV, seq_b), tile(DK, la_b)] + slabs,
        out_specs=[tile(DV, seq_f), tile(DV, seq_b)] + slabs,
        out_shape=[jax.ShapeDtypeStruct(((NT + 1) * TM, DV), BF16)] * 2
                  + [jax.ShapeDtypeStruct(w.shape, BF16) for w in streamed],
        scratch_shapes=[pltpu.VMEM((HEADS, HK, HV), F32)] * 2,
        compiler_params=_params(),
        name="gla_scan",
    )(q, k, v, la, q, k, v, la, *streamed)
    w1b, w3b, w2b = w1b.reshape(2, D, FFN), w3b.reshape(2, D, FFN), w2b.reshape(2, FFN, D)

    sub = lambda w, j: pl.BlockSpec((TM, w), lambda i: (jnp.minimum(SUBTILES * i + j, NT), 0))
    x_mid, y_br, u_all = pl.pallas_call(
        _gla_out_kernel,
        grid=(CTX_STEP + 1,),
        in_specs=[lat_pair, _const((CTX, D)), mod_spec(0), mod_spec(1),
                  sub(DV, 0), sub(DV, 1), sub(DV, 0), sub(DV, 1), sub(DV, 0), sub(DV, 1),
                  _const((1, HV)), _const((DV, D))] + ffn_specs(0) + [_const((1, D)), _const((D, 2 * D))],
        out_specs=[pair(D), pair(D), pair(D)],
        out_shape=[jax.ShapeDtypeStruct((nseq, D), F32), jax.ShapeDtypeStruct((nseq, D), BF16),
                   jax.ShapeDtypeStruct((nseq, D), F32)],
        compiler_params=_params(),
        name="gla_out",
    )(x2d, ctx2d, mods, mods, o_f, o_f, o_b, o_b, g, g,
      vec(gla_head_norm_g[0]), gwob, vec(norm_ffn_g[0]), w1b, w3b, w2b,
      vec(norm_mix_g[1]), lwinb)

    u3 = u_all.reshape(nseq // GRID_W, GRID_W, D)
    uctx3 = u_all[NT * TM:(NT + 1) * TM].reshape(SUB, CTX_SEG, D).transpose(1, 0, 2)
    gate_w = lambda d: jnp.concatenate([lru_gate_a_w[0, d], lru_gate_x_w[0, d]], axis=-1).astype(BF16)
    gate_b = lambda d: 0.5 * jnp.stack([lru_gate_a_b[0, d], lru_gate_x_b[0, d]])
    chan = lambda rows: pl.BlockSpec((rows, LRU_BLOCK), lambda cb, wg: (0, cb))
    gate_spec = pl.BlockSpec((None, LRU_BLOCK, 2 * LRU_BLOCK), lambda cb, wg: (cb, 0, 0))
    lru_weights = (lru_conv_w[0], vec(lru_conv_b[0]), gate_w(0), gate_w(1), gate_b(0), gate_b(1),
                   lru_lambda[0])

    def column_scans(u3d, rows, blk_rows, ngroups):
        blk = (rows, SUB, LRU_BLOCK)
        here = pl.BlockSpec(blk, lambda cb, wg: (0, wg, cb))
        return pl.pallas_call(
            functools.partial(_lru_scan_kernel, rows=rows, blk_rows=blk_rows, ngroups=ngroups),
            grid=(LRU_BLOCKS, ngroups),
            in_specs=[here,
                      pl.BlockSpec((2, SUB, LRU_BLOCK), lambda cb, wg: (rows // 2 - 1, jnp.maximum(wg - 1, 0), cb)),
                      pl.BlockSpec((2, SUB, LRU_BLOCK), lambda cb, wg: (0, jnp.minimum(wg + 1, ngroups - 1), cb)),
                      chan(CONV_W), chan(1), gate_spec, gate_spec, chan(2), chan(2), chan(2)],
            out_specs=[here, here, here, pl.BlockSpec((4, SUB, LRU_BLOCK), lambda cb, wg: (0, wg, cb))],
            out_shape=[jax.ShapeDtypeStruct((rows, ngroups * SUB, D), F32)] * 3
                      + [jax.ShapeDtypeStruct((4, ngroups * SUB, D), F32)],
            scratch_shapes=[pltpu.VMEM((rows + CONV_W - 1, SUB, LRU_BLOCK), F32)],
            compiler_params=_params(2),
            name=f"lru_scan_{ngroups * SUB}x{rows}",
        )(u3d, u3d, u3d, *lru_weights)

    ctx_sums = column_scans(uctx3, CTX_SEG, CTX_SEG, 1)[3]
    s0, pf, pb, sums = column_scans(u3, ROWS, SCAN_ROWS, NGRP)

    h_f, h_b = pl.pallas_call(
        _lru_chain_kernel,
        out_shape=[jax.ShapeDtypeStruct((GRID_W, D), F32)] * 2,
        name="lru_chain",
    )(ctx_sums, sums)

    reps = OUT_TM // GRID_W
    lat = lambda i: (i, 0)
    big = lambda f: pl.BlockSpec((OUT_SUBTILES * OUT_TM, D), f)
    out = pl.pallas_call(
        _lru_out_kernel,
        grid=(SEQ // (OUT_SUBTILES * OUT_TM),),
        in_specs=[big(lat), _const((None, None, SUB, D), (1, 0, 0, 0)),
                  big(lat), big(lat), big(lat), big(lat),
                  _const((OUT_TM, D)), _const((OUT_TM, D)), _const((D, D))] + ffn_specs(1) + [_const((1, D))],
        out_specs=big(lat),
        out_shape=jax.ShapeDtypeStruct((SEQ, D), F32),
        compiler_params=_params(),
        name="lru_out",
    )(x_mid, mods, y_br, s0.reshape(SEQ, D), pf.reshape(SEQ, D), pb.reshape(SEQ, D),
      jnp.tile(h_f, (reps, 1)), jnp.tile(h_b, (reps, 1)),
      lwob, vec(norm_ffn_g[1]), w1b, w3b, w2b, vec(final_norm_g))
    return out.reshape(1, SEQ, D)
```

```python
import functools

import jax
import jax.numpy as jnp
from jax import lax
from jax.experimental import pallas as pl
from jax.experimental.pallas import tpu as pltpu

D = 1024
SEQ = 16384
GRID_W = 64
ROWS = SEQ // GRID_W
CTX = 256
TM = 256
NT = SEQ // TM
HEADS = 4
DK = 512
DV = 1024
HK = DK // HEADS
HV = DV // HEADS
RANK = 16
TAU = 16.0
CHUNK = 64
NCH = TM // CHUNK
LRU_BLOCKS = 4
LRU_BLOCK = D // LRU_BLOCKS
LRU_C = 8.0
FFN = 2816
EPS = 1e-6
SUB = 8
BF16_SUBLANES = 16
N_STREAMED_WEIGHTS = 6
CONV_W = 4
CONV_LEFT = 2
NGRP = GRID_W // SUB
CTX_SEG = CTX // SUB
TILE_GR = TM // GRID_W
FFN_CHUNK = 256
CHAIN_LANES = 128
SUBTILES = 2
CTX_STEP = NT // SUBTILES
OUT_TM = 256
OUT_SUBTILES = 2

F32 = jnp.float32
BF16 = jnp.bfloat16

V7X_VMEM_LIMIT = 56 * 1024 * 1024
LOG2E = 1.4426950408889634
TINY = 1e-30


def _dot(a, b):
    return jnp.dot(a, b, preferred_element_type=F32)


def _dot_nt(a, b):
    return lax.dot_general(a, b, (((1,), (1,)), ((), ())), preferred_element_type=F32)


def _dot_tn(a, b):
    return lax.dot_general(a, b, (((0,), (0,)), ((), ())), preferred_element_type=F32)


def _rms(x, g):
    ms = jnp.mean(x * x, axis=-1, keepdims=True)
    return x * lax.rsqrt(ms + EPS) * g


def _rms_mod(x, g, scale, shift):
    return _rms(x, g) * (1.0 + scale) + shift


def _silu(x):
    return x * jax.nn.sigmoid(x)


def _split_bf16(x):
    hi = x.astype(BF16)
    lo = (x - hi.astype(F32)).astype(BF16)
    return hi, lo


def _mod_kernel(c_ref, w_ref, b_ref, o_ref):
    s = _silu(c_ref[...])
    w = w_ref[...]
    b = b_ref[...]
    o_ref[...] = jnp.zeros_like(o_ref)
    for r in range(2):
        o_ref[r:r + 1, :] = jnp.sum(w * s[:, r:r + 1], axis=0, keepdims=True) + b


def _modulation(ccols, w_mod, b_mod):
    depth = w_mod.shape[0]
    tn = 1536
    return pl.pallas_call(
        _mod_kernel,
        grid=(depth, 6 * D // tn),
        in_specs=[
            pl.BlockSpec((D, 2), lambda l, n: (0, 0)),
            pl.BlockSpec((None, D, tn), lambda l, n: (l, 0, n)),
            pl.BlockSpec((None, 1, tn), lambda l, n: (l, 0, n)),
        ],
        out_specs=pl.BlockSpec((None, SUB, tn), lambda l, n: (l, 0, n)),
        out_shape=jax.ShapeDtypeStruct((depth, SUB, 6 * D), F32),
        compiler_params=pltpu.CompilerParams(
            dimension_semantics=("arbitrary", "arbitrary"),
            vmem_limit_bytes=V7X_VMEM_LIMIT),
        name="mod",
    )(ccols, w_mod, b_mod.reshape(depth, 1, 6 * D))


def _gla_tiles(jobs):
    row = lax.broadcasted_iota(jnp.int32, (TM, TM), 0)
    col = lax.broadcasted_iota(jnp.int32, (TM, TM), 1)
    shift = CHUNK.bit_length() - 1
    same = jnp.right_shift(row, shift) == jnp.right_shift(col, shift)
    prep = []
    for q, k, v, la, s_ref, o_ref, rev in jobs:
        tri = same & ((col >= row) if rev else (col <= row))
        tri_b = tri.astype(BF16)
        la_hi, la_lo = _split_bf16(la)
        b = _dot(tri_b, la_hi) + _dot(tri_b, la_lo)
        prep.append((tri, b))
    work = []
    for (q, k, v, la, s_ref, o_ref, rev), (tri, b) in zip(jobs, prep):
        ends = [c * CHUNK if rev else (c + 1) * CHUNK - 1 for c in range(NCH)]
        bl = jnp.concatenate([jnp.broadcast_to(b[r:r + 1], (CHUNK, DK)) for r in ends], axis=0)
        q_dec = (q * jnp.exp(b)).astype(BF16)
        k_inv = (k * jnp.exp(-b)).astype(BF16)
        k_end = (k * jnp.exp(bl - b)).astype(BF16)
        totals = jnp.concatenate([b[r:r + 1] for r in ends] + [jnp.zeros((SUB - NCH, DK), F32)], axis=0)
        decay_t = jnp.exp(totals).T
        order = list(range(NCH - 1, -1, -1) if rev else range(NCH))
        for h in range(HEADS):
            ks = slice(h * HK, (h + 1) * HK)
            vs = slice(h * HV, (h + 1) * HV)
            work.append(dict(tri=tri, qd=q_dec[:, ks], ki=k_inv[:, ks], ke=k_end[:, ks], vh=v[:, vs],
                             dec=decay_t[ks], order=order, s_ref=s_ref, h=h, o_ref=o_ref, vs=vs))
    for w in work:
        w["att"] = jnp.where(w["tri"], _dot_nt(w["qd"], w["ki"]), 0.0).astype(BF16)
    for w in work:
        w["o_intra"] = _dot(w["att"], w["vh"])
        w["st"] = w["s_ref"][w["h"]]
    for idx in range(NCH):
        for w in work:
            c = w["order"][idx]
            rs = slice(c * CHUNK, (c + 1) * CHUNK)
            o_inter = _dot(w["qd"][rs], w["st"].astype(BF16))
            w["o_ref"][rs, w["vs"]] = (w["o_intra"][rs] + o_inter).astype(w["o_ref"].dtype)
            u = _dot_tn(w["ke"][rs], w["vh"][rs])
            w["st"] = w["dec"][:, c:c + 1] * w["st"] + u
    for w in work:
        w["s_ref"][w["h"]] = w["st"]


def _proj_store(rs, z, zg, q_ref, k_ref, v_ref, g_ref, la_ref):
    q_ref[rs] = (z[:, :DK] * (HK ** -0.5)).astype(BF16)
    k_ref[rs] = z[:, DK:2 * DK].astype(BF16)
    v_ref[rs] = z[:, 2 * DK:2 * DK + DV].astype(BF16)
    g_ref[rs] = z[:, 2 * DK + DV:].astype(BF16)
    la_ref[rs] = (jnp.minimum(zg, 0.0) - jnp.log(1.0 + jnp.exp(-jnp.abs(zg)))) * (1.0 / TAU)


def _gla_proj_kernel(x_ref, ctx_ref, mod_ref, gmix_ref, win_ref, gw1_ref, gw2_ref, gb_ref,
                     q_ref, k_ref, v_ref, g_ref, la_ref):
    is_ctx = pl.program_id(0) == CTX_STEP
    mod = mod_ref[...]
    rows = [slice(j * TM, (j + 1) * TM) for j in range(SUBTILES)]
    a = [_rms_mod(jnp.where(is_ctx, ctx_ref[...], x_ref[rs]), gmix_ref[...], mod[1:2], mod[0:1]).astype(BF16)
         for rs in rows]
    pending = None
    for rs, at in zip(rows, a):
        z = _dot(at, win_ref[...])
        t = _dot(at, gw1_ref[...]).astype(BF16)
        zg = _dot(t, gw2_ref[...]) + gb_ref[...]
        if pending is not None:
            _proj_store(*pending, q_ref, k_ref, v_ref, g_ref, la_ref)
        pending = (rs, z, zg)
    _proj_store(*pending, q_ref, k_ref, v_ref, g_ref, la_ref)


def _gla_scan_kernel(qf_ref, kf_ref, vf_ref, laf_ref, qb_ref, kb_ref, vb_ref, lab_ref, *rest):
    nw = N_STREAMED_WEIGHTS
    w_in, (of_ref, ob_ref), w_out, (sf_ref, sb_ref) = (
        rest[:nw], rest[nw:nw + 2], rest[nw + 2:2 * nw + 2], rest[2 * nw + 2:])

    @pl.when(pl.program_id(0) == 0)
    def _():
        sf_ref[...] = jnp.zeros_like(sf_ref)
        sb_ref[...] = jnp.zeros_like(sb_ref)

    _gla_tiles([
        (qf_ref[...].astype(F32), kf_ref[...].astype(F32), vf_ref[...], laf_ref[...], sf_ref, of_ref, False),
        (qb_ref[...].astype(F32), kb_ref[...].astype(F32), vb_ref[...], lab_ref[...], sb_ref, ob_ref, True)])
    for src, dst in zip(w_in, w_out):
        dst[...] = src[...].astype(BF16)


def _ffn_stages(x1, mod, gffn, w1_ref, w3_ref, w2_ref):
    f = [_rms_mod(t, gffn, mod[4:5], mod[3:4]).astype(BF16) for t in x1]
    p = [(_silu(_dot(t, w1_ref[...])) * _dot(t, w3_ref[...])).astype(BF16) for t in f]
    return [t + mod[5:6] * _dot(pt, w2_ref[...]) for t, pt in zip(x1, p)]


def _lru_local_stages(prev2, cur, next1, cw_ref, cb_ref, wg_refs, bg_refs, lam_ref, outs):
    s0_ref, pf_ref, pb_ref, sum_ref = outs
    st = {}

    def conv():
        e = jnp.concatenate([prev2, cur, next1], axis=0).reshape(TILE_GR + CONV_W - 1, GRID_W, D)
        cw = 0.5 * cw_ref[...]
        uch = 0.5 * cb_ref[...] + cw[0:1] * e[0:TILE_GR]
        for j in range(1, CONV_W):
            uch = uch + cw[j:j + 1] * e[j:j + TILE_GR]
        st["uch"] = uch.reshape(TM, D)
        st["ub"] = st["uch"].astype(BF16)
        st["c2"] = (-0.5 * LRU_C * LOG2E) * jax.nn.softplus(-lam_ref[...])

    def coeffs(d, g):
        def run():
            cs = slice(g * LRU_BLOCK, (g + 1) * LRU_BLOCK)
            zz = _dot(st["ub"][:, cs], wg_refs[d][g])
            bg = bg_refs[d][:, cs]
            ta = jnp.tanh(zz[:, :LRU_BLOCK] + bg[0:1])
            tx = jnp.tanh(zz[:, LRU_BLOCK:] + bg[1:2])
            c2 = st["c2"][d:d + 1, cs]
            a = jnp.exp2(c2 * ta + c2)
            v = 1.0 - a * a
            s = v * lax.rsqrt(jnp.maximum(v, TINY))
            st[("a", d, g)] = a
            st[("b", d, g)] = s * ((tx + 1.0) * st["uch"][:, cs])
        return run

    def scans():
        ab = []
        for d in range(2):
            a = jnp.concatenate([st[("a", d, g)] for g in range(LRU_BLOCKS)], axis=1)
            b = jnp.concatenate([st[("b", d, g)] for g in range(LRU_BLOCKS)], axis=1)
            ab.append((a.reshape(TILE_GR, GRID_W, D), b.reshape(TILE_GR, GRID_W, D)))
        (af, bf), (ab_, bb) = ab
        hf, pf = [bf[0]], [af[0]]
        for r in range(1, TILE_GR):
            hf.append(af[r] * hf[-1] + bf[r])
            pf.append(af[r] * pf[-1])
        hb, pb = [bb[TILE_GR - 1]], [ab_[TILE_GR - 1]]
        for r in range(TILE_GR - 2, -1, -1):
            hb.insert(0, ab_[r] * hb[0] + bb[r])
            pb.insert(0, ab_[r] * pb[0])
        for r in range(TILE_GR):
            rs = slice(r * GRID_W, (r + 1) * GRID_W)
            s0_ref[rs] = hf[r] + hb[r]
            pf_ref[rs] = pf[r]
            pb_ref[rs] = pb[r]
        sum_ref[0] = pf[-1]
        sum_ref[1] = hf[-1]
        sum_ref[2] = pb[0]
        sum_ref[3] = hb[0]

    return [conv] + [coeffs(d, g) for d in range(2) for g in range(LRU_BLOCKS)] + [scans]


def _mix_prologue_stages(x_ref, ctx_ref, is_ctx, mod0, of_ref, ob_ref, g_ref, hn, wout_ref, gffn,
                         x1_ref, f_ref):
    st = {"y": None}

    def head(h):
        def run():
            hs = slice(h * HV, (h + 1) * HV)
            o = of_ref[:, hs].astype(F32) + ob_ref[:, hs].astype(F32)
            gated = (_rms(o, hn) * _silu(g_ref[:, hs].astype(F32))).astype(BF16)
            part = _dot(gated, wout_ref[hs, :])
            st["y"] = part if st["y"] is None else st["y"] + part
        return run

    def mix():
        x1 = jnp.where(is_ctx, ctx_ref[...], x_ref[...]) + mod0[2:3] * st["y"]
        x1_ref[...] = x1
        f_ref[...] = _rms_mod(x1, gffn, mod0[4:5], mod0[3:4]).astype(BF16)

    return [head(h) for h in range(HEADS)] + [mix]


def _gla_out_kernel(x_ref, ctx_ref, mod0n_ref, mod0_ref, mod1_ref, of_ref, ob_ref, g_ref,
                    hn_ref, wout_ref, gffn_ref, w1_ref, w3_ref, w2_ref, gmix1_ref, lwin_ref,
                    cw_ref, cb_ref, wg0_ref, wg1_ref, bg0_ref, bg1_ref, lam_ref,
                    xo_ref, y_ref, u_ref, s0_ref, pf_ref, pb_ref, sum_ref,
                    x1_ref, f_ref, u1_ref, u2_ref, tail_ref):
    i = pl.program_id(0)
    hn = hn_ref[...]
    gffn = gffn_ref[...]
    prologue = lambda: _mix_prologue_stages(x_ref, ctx_ref, i >= NT, mod0n_ref[...], of_ref, ob_ref,
                                            g_ref, hn, wout_ref, gffn, x1_ref, f_ref)

    @pl.when(i == 0)
    def _():
        u1_ref[...] = jnp.zeros_like(u1_ref)
        u2_ref[...] = jnp.zeros_like(u2_ref)
        tail_ref[...] = jnp.zeros_like(tail_ref)
        for stage in prologue():
            stage()

    @pl.when(i > 0)
    def _():
        mod0 = mod0_ref[...]
        mod1 = mod1_ref[...]
        x1 = x1_ref[...]
        f = f_ref[...]
        lru = _lru_local_stages(tail_ref[...], u2_ref[...], u1_ref[0:GRID_W], cw_ref, cb_ref,
                                (wg0_ref, wg1_ref), (bg0_ref, bg1_ref), lam_ref,
                                (s0_ref, pf_ref, pb_ref, sum_ref))
        pro = prologue()
        fillers = []
        while lru or pro:
            for src in (lru, pro):
                if src:
                    fillers.append(src.pop(0))
        n_chunks = FFN // FFN_CHUNK
        p = []
        for c in range(n_chunks):
            cs = slice(c * FFN_CHUNK, (c + 1) * FFN_CHUNK)
            p.append((_silu(_dot(f, w1_ref[:, cs])) * _dot(f, w3_ref[:, cs])).astype(BF16))
            take = -(-len(fillers) // (n_chunks - c))
            for _ in range(take):
                fillers.pop(0)()
        x2 = x1 + mod0[5:6] * _dot(jnp.concatenate(p, axis=1), w2_ref[...])
        xo_ref[...] = x2
        a = _rms_mod(x2, gmix1_ref[...], mod1[1:2], mod1[0:1]).astype(BF16)
        z = _dot(a, lwin_ref[...])
        y_ref[...] = jax.nn.gelu(z[:, :D]).astype(BF16)
        u = z[:, D:]
        u_ref[...] = u
        tail_ref[...] = u2_ref[TM - 2 * GRID_W:TM]
        u2_ref[...] = u1_ref[...]
        u1_ref[...] = u


def _lru_coeffs(ub, uch, wg_ref, bg_ref, c2, rows):
    zz = _dot(ub, wg_ref[...])
    bg = bg_ref[...]
    ta = jnp.tanh(zz[:, :LRU_BLOCK] + bg[0:1])
    tx = jnp.tanh(zz[:, LRU_BLOCK:] + bg[1:2])
    a = jnp.exp2(c2 * ta + c2)
    v = 1.0 - a * a
    s = v * lax.rsqrt(jnp.maximum(v, TINY))
    b = s * ((tx + 1.0) * uch)
    shape = (rows, SUB, LRU_BLOCK)
    return a.reshape(shape), b.reshape(shape)


def _lru_scan_kernel(u_ref, hp_ref, hn_ref, cw_ref, cb_ref, wg0_ref, wg1_ref,
                     bg0_ref, bg1_ref, lam_ref,
                     s0_ref, pf_ref, pb_ref, sum_ref,
                     e_ref, *, rows, blk_rows, ngroups):
    wg = pl.program_id(1)
    u = u_ref[...]
    prev_ok = wg >= 1
    next_ok = wg <= ngroups - 2
    sub = lax.broadcasted_iota(jnp.int32, (SUB, LRU_BLOCK), 0)
    hp = hp_ref[...]
    hn = hn_ref[...]

    def from_prev_column(cur, halo):
        edge = jnp.where(prev_ok, pltpu.roll(halo, 1, 0), 0.0)
        return jnp.where(sub == 0, edge, pltpu.roll(cur, 1, 0))

    e_ref[CONV_LEFT:CONV_LEFT + rows] = u
    e_ref[0] = from_prev_column(u[rows - 2], hp[0])
    e_ref[1] = from_prev_column(u[rows - 1], hp[1])
    edge = jnp.where(next_ok, pltpu.roll(hn[0], SUB - 1, 0), 0.0)
    e_ref[CONV_LEFT + rows] = jnp.where(sub == SUB - 1, edge, pltpu.roll(u[0], SUB - 1, 0))
    cw = 0.5 * cw_ref[...]
    cb = 0.5 * cb_ref[...]
    lam = lam_ref[...]
    c2 = (-0.5 * LRU_C * LOG2E) * jax.nn.softplus(-lam)
    nblk = rows // blk_rows
    ab = [[None] * nblk, [None] * nblk]
    for k in range(nblk):
        for d, blk in ((0, k), (1, nblk - 1 - k)):
            r0 = blk * blk_rows
            uch = cb + cw[0:1] * e_ref[r0:r0 + blk_rows]
            for j in range(1, CONV_W):
                uch = uch + cw[j:j + 1] * e_ref[r0 + j:r0 + j + blk_rows]
            uch = uch.reshape(blk_rows * SUB, LRU_BLOCK)
            ab[d][blk] = _lru_coeffs(uch.astype(BF16), uch, (wg0_ref, wg1_ref)[d], (bg0_ref, bg1_ref)[d],
                                     c2[d:d + 1], blk_rows)

    h = jnp.zeros((SUB, LRU_BLOCK), F32)
    p = jnp.ones((SUB, LRU_BLOCK), F32)
    for r in range(rows):
        a, b = ab[0][r // blk_rows]
        h = a[r % blk_rows] * h + b[r % blk_rows]
        p = a[r % blk_rows] * p
        s0_ref[r] = h
        pf_ref[r] = p
    sum_ref[0] = p
    sum_ref[1] = h
    h = jnp.zeros((SUB, LRU_BLOCK), F32)
    p = jnp.ones((SUB, LRU_BLOCK), F32)
    for r in range(rows - 1, -1, -1):
        a, b = ab[1][r // blk_rows]
        h = a[r % blk_rows] * h + b[r % blk_rows]
        p = a[r % blk_rows] * p
        s0_ref[r] = s0_ref[r] + h
        pb_ref[r] = p
    sum_ref[2] = p
    sum_ref[3] = h


def _lru_fix_kernel(cur_ref, prev_ref, next_ref, cw_ref, cb_ref, wg0_ref, wg1_ref, bg0_ref, bg1_ref,
                    lam_ref, s0_in, pf_in, pb_in, sum_in, s0_ref, pf_ref, pb_ref, sum_ref):
    del s0_in, pf_in, pb_in, sum_in
    first = pl.program_id(0) == 0
    col = lax.broadcasted_iota(jnp.int32, (GRID_W, D), 0)
    prev2 = prev_ref[...]
    wrapped = [jnp.where(col == 0, 0.0, pltpu.roll(prev2[r * GRID_W:(r + 1) * GRID_W], 1, 0))
               for r in range(2)]
    prev2 = jnp.where(first, jnp.concatenate(wrapped, axis=0), prev2)
    next1 = next_ref[...]
    next1 = jnp.where(first, next1, jnp.where(col == GRID_W - 1, 0.0, pltpu.roll(next1, GRID_W - 1, 0)))
    for stage in _lru_local_stages(prev2, cur_ref[...], next1, cw_ref, cb_ref, (wg0_ref, wg1_ref),
                                   (bg0_ref, bg1_ref), lam_ref, (s0_ref, pf_ref, pb_ref, sum_ref)):
        stage()


def _lru_chain_kernel(csum_ref, sum_ref, hf_ref, hb_ref, z_ref, p_ref, cs_ref):
    tiles = [slice(t * GRID_W, (t + 1) * GRID_W) for t in range(NT)]
    for fwd in (True, False):
        ka, kb = (0, 1) if fwd else (2, 3)
        out_ref = hf_ref if fwd else hb_ref
        order = list(range(NT)) if fwd else list(range(NT - 1, -1, -1))
        z = jnp.zeros((GRID_W, CHAIN_LANES), F32)
        p = jnp.ones((GRID_W, CHAIN_LANES), F32)
        for t in order:
            z_ref[tiles[t]] = z
            p_ref[tiles[t]] = p
            a = sum_ref[ka, tiles[t], :]
            z = a * z + sum_ref[kb, tiles[t], :]
            p = a * p
        s = jnp.zeros((1, CHAIN_LANES), F32)
        for w in (range(SUB) if fwd else range(SUB - 1, -1, -1)):
            s = csum_ref[ka, w:w + 1, :] * s + csum_ref[kb, w:w + 1, :]
        for w in (range(GRID_W) if fwd else range(GRID_W - 1, -1, -1)):
            cs_ref[w:w + 1, :] = s
            s = p[w:w + 1] * s + z[w:w + 1]
        cs = cs_ref[...]
        for t in range(NT):
            out_ref[tiles[t]] = z_ref[tiles[t]] + p_ref[tiles[t]] * cs


def _lru_out_kernel(x_ref, mod_ref, y_ref, s0_ref, pf_ref, pb_ref, hf_ref, hb_ref,
                    wout_ref, gffn_ref, w1_ref, w3_ref, w2_ref, gfin_ref, o_ref):
    mod = mod_ref[...]
    x1 = []
    for j in range(OUT_SUBTILES):
        rs = slice(j * OUT_TM, (j + 1) * OUT_TM)
        ts = slice(j * GRID_W, (j + 1) * GRID_W)
        yl = None
        for c in range(LRU_BLOCKS):
            cs = slice(c * LRU_BLOCK, (c + 1) * LRU_BLOCK)
            shape = (TILE_GR, GRID_W, LRU_BLOCK)
            hs = (s0_ref[rs, cs].reshape(shape) + pf_ref[rs, cs].reshape(shape) * hf_ref[ts, cs]
                  + pb_ref[rs, cs].reshape(shape) * hb_ref[ts, cs]).reshape(OUT_TM, LRU_BLOCK)
            ym = (hs * y_ref[rs, cs].astype(F32)).astype(BF16)
            part = _dot(ym, wout_ref[cs, :])
            yl = part if yl is None else yl + part
        x1.append(x_ref[rs] + mod[2:3] * yl)
    x2 = _ffn_stages(x1, mod, gffn_ref[...], w1_ref, w3_ref, w2_ref)
    for j, t in enumerate(x2):
        o_ref[j * OUT_TM:(j + 1) * OUT_TM] = _rms(t, gfin_ref[...])


def _const(shape, index=None):
    nd = len(shape)
    index = (0,) * nd if index is None else index
    return pl.BlockSpec(shape, lambda *_: index, pipeline_mode=pl.Buffered(1))


def _params(n_axes=1):
    return pltpu.CompilerParams(dimension_semantics=("arbitrary",) * n_axes,
                                vmem_limit_bytes=V7X_VMEM_LIMIT)


def kernel(x, c, ctx, c_ctx, norm_mix_g, norm_ffn_g, w_mod, b_mod, gla_w_in, gla_gate_w1, gla_gate_w2, gla_gate_b, gla_head_norm_g, gla_w_out, lru_w_in, lru_conv_w, lru_conv_b, lru_gate_a_w, lru_gate_a_b, lru_gate_x_w, lru_gate_x_b, lru_lambda, lru_w_out, ffn_w1, ffn_w3, ffn_w2, final_norm_g):
    assert x.shape == (1, SEQ, D) and ctx.shape == (1, CTX, D) and CTX == TM == ROWS
    x2d = x.reshape(SEQ, D)
    ctx2d = ctx.reshape(CTX, D)
    nseq = (NT + SUBTILES) * TM

    m = _modulation(jnp.stack([c[0], c_ctx], axis=1), w_mod, b_mod)
    mods = m[:, :2].reshape(2, 2, 6, D)
    mods = jnp.pad(mods, ((0, 0), (0, 0), (0, SUB - 6), (0, 0)))

    def mod_spec(layer, *ctx_steps):
        is_ctx = lambda i: functools.reduce(jnp.logical_or, [i == s for s in ctx_steps])
        return pl.BlockSpec((None, None, SUB, D), lambda i: (layer, jnp.where(is_ctx(i), 1, 0), 0, 0))

    vec = lambda t: t.reshape(1, -1)
    tile = lambda w, f: pl.BlockSpec((TM, w), f)

    def ffn_specs(layer):
        return [_const((1, D)), _const((None, D, FFN), (layer, 0, 0)),
                _const((None, D, FFN), (layer, 0, 0)), _const((None, FFN, D), (layer, 0, 0))]

    win = gla_w_in[0].astype(BF16)
    gw1 = jnp.concatenate([gla_gate_w1[0, 0], gla_gate_w1[0, 1]], axis=1)
    gw1 = jnp.pad(gw1, ((0, 0), (0, 128 - 2 * RANK))).astype(BF16)
    gw2 = jnp.zeros((128, 2 * DK), F32)
    gw2 = gw2.at[:RANK, :DK].set(gla_gate_w2[0, 0]).at[RANK:2 * RANK, DK:].set(gla_gate_w2[0, 1])
    gw2 = gw2.astype(BF16)
    gb = gla_gate_b[0].reshape(1, 2 * DK)

    pair = lambda w: pl.BlockSpec((SUBTILES * TM, w), lambda i: (i, 0))
    lat_pair = pl.BlockSpec((SUBTILES * TM, D), lambda i: (jnp.minimum(i, CTX_STEP - 1), 0))
    q, k, v, g, la = pl.pallas_call(
        _gla_proj_kernel,
        grid=(CTX_STEP + 1,),
        in_specs=[lat_pair, _const((CTX, D)), mod_spec(0, CTX_STEP), _const((1, D)),
                  _const((D, 2 * DK + 2 * DV)), _const((D, 128)), _const((128, 2 * DK)),
                  _const((1, 2 * DK))],
        out_specs=[pair(DK), pair(DK), pair(DV), pair(DV), pair(2 * DK)],
        out_shape=[jax.ShapeDtypeStruct((nseq, DK), BF16), jax.ShapeDtypeStruct((nseq, DK), BF16),
                   jax.ShapeDtypeStruct((nseq, DV), BF16), jax.ShapeDtypeStruct((nseq, DV), BF16),
                   jax.ShapeDtypeStruct((nseq, 2 * DK), F32)],
        compiler_params=_params(),
        name="gla_proj",
    )(x2d, ctx2d, mods, vec(norm_mix_g[0]), win, gw1, gw2, gb)

    seq_f = lambda i: (jnp.where(i == 0, NT, i - 1), 0)
    seq_b = lambda i: (jnp.where(i == 0, NT, NT - i), 0)
    streamed = [ffn_w1.reshape(2 * D, FFN), ffn_w3.reshape(2 * D, FFN), ffn_w2.reshape(2 * FFN, D),
                gla_w_out[0], lru_w_in[0], lru_w_out[0]]
    assert len(streamed) == N_STREAMED_WEIGHTS

    def slab_spec(w):
        nblk = NT if w.shape[0] % (NT * BF16_SUBLANES) == 0 else NT // 2
        rows = w.shape[0] // nblk
        assert rows * nblk == w.shape[0] and rows % BF16_SUBLANES == 0
        return pl.BlockSpec((rows, w.shape[1]), lambda i: (jnp.minimum(i, nblk - 1), 0))

    slabs = [slab_spec(w) for w in streamed]
    la_b = lambda i: (seq_b(i)[0], 1)
    o_f, o_b, w1b, w3b, w2b, gwob, lwinb, lwob = pl.pallas_call(
        _gla_scan_kernel,
        grid=(NT + 1,),
        in_specs=[tile(DK, seq_f), tile(DK, seq_f), tile(DV, seq_f), tile(DK, seq_f),
                  tile(DK, seq_b), tile(DK, seq_b), tile(DV, seq_b), tile(DK, la_b)] + slabs,
        out_specs=[tile(DV, seq_f), tile(DV, seq_b)] + slabs,
        out_shape=[jax.ShapeDtypeStruct(((NT + 1) * TM, DV), BF16)] * 2
                  + [jax.ShapeDtypeStruct(w.shape, BF16) for w in streamed],
        scratch_shapes=[pltpu.VMEM((HEADS, HK, HV), F32)] * 2,
        compiler_params=_params(),
        name="gla_scan",
    )(q, k, v, la, q, k, v, la, *streamed)
    w1b, w3b, w2b = w1b.reshape(2, D, FFN), w3b.reshape(2, D, FFN), w2b.reshape(2, FFN, D)

    gate_w = lambda d: jnp.concatenate([lru_gate_a_w[0, d], lru_gate_x_w[0, d]], axis=-1).astype(BF16)
    gate_b = lambda d: 0.5 * jnp.stack([lru_gate_a_b[0, d], lru_gate_x_b[0, d]])
    lru_weights = (lru_conv_w[0], vec(lru_conv_b[0]), gate_w(0), gate_w(1), gate_b(0), gate_b(1),
                   lru_lambda[0])
    lru_weight_specs = [_const((CONV_W, D)), _const((1, D)), _const((LRU_BLOCKS, LRU_BLOCK, 2 * LRU_BLOCK)),
                        _const((LRU_BLOCKS, LRU_BLOCK, 2 * LRU_BLOCK)), _const((2, D)), _const((2, D)),
                        _const((2, D))]
    pro_seq = lambda i: (jnp.minimum(i, NT), 0)
    pro_lat = lambda i: (jnp.minimum(i, NT - 1), 0)
    main_seq = lambda i: (jnp.clip(i - 1, 0, NT), 0)
    late = lambda i: (jnp.maximum(i - 3, 0), 0)
    run_shapes = [jax.ShapeDtypeStruct((SEQ, D), F32)] * 3 + [jax.ShapeDtypeStruct((4, NT * GRID_W, D), F32)]
    x_mid, y_br, u_all, s0, pf, pb, sums = pl.pallas_call(
        _gla_out_kernel,
        grid=(NT + 2,),
        in_specs=[tile(D, pro_lat), _const((CTX, D)), mod_spec(0, NT, NT + 1), mod_spec(0, NT + 1),
                  mod_spec(1, NT + 1), tile(DV, pro_seq), tile(DV, pro_seq), tile(DV, pro_seq),
                  _const((1, HV)), _const((DV, D))] + ffn_specs(0) + [_const((1, D)), _const((D, 2 * D))]
                 + lru_weight_specs,
        out_specs=[tile(D, main_seq), tile(D, main_seq), tile(D, main_seq),
                   tile(D, late), tile(D, late), tile(D, late),
                   pl.BlockSpec((4, GRID_W, D), lambda i: (0, late(i)[0], 0))],
        out_shape=[jax.ShapeDtypeStruct(((NT + 1) * TM, D), F32), jax.ShapeDtypeStruct(((NT + 1) * TM, D), BF16),
                   jax.ShapeDtypeStruct(((NT + 1) * TM, D), F32)] + run_shapes,
        scratch_shapes=[pltpu.VMEM((TM, D), F32), pltpu.VMEM((TM, D), BF16),
                        pltpu.VMEM((TM, D), F32), pltpu.VMEM((TM, D), F32),
                        pltpu.VMEM((2 * GRID_W, D), F32)],
        compiler_params=_params(),
        name="gla_out",
    )(x2d, ctx2d, mods, mods, mods, o_f, o_b, g,
      vec(gla_head_norm_g[0]), gwob, vec(norm_ffn_g[0]), w1b, w3b, w2b,
      vec(norm_mix_g[1]), lwinb, *lru_weights)

    edge = lambda i: jnp.where(i == 0, 0, NT - 1)
    any_spec = pl.BlockSpec(memory_space=pl.ANY)
    s0, pf, pb, sums = pl.pallas_call(
        _lru_fix_kernel,
        grid=(2,),
        in_specs=[pl.BlockSpec((TM, D), lambda i: (edge(i), 0)),
                  pl.BlockSpec((2 * GRID_W, D), lambda i: (jnp.where(i == 0, NT * 2 - 1, NT * 2 - 3), 0)),
                  pl.BlockSpec((GRID_W, D), lambda i: (jnp.where(i == 0, TILE_GR, 0), 0))]
                 + lru_weight_specs + [any_spec] * 4,
        out_specs=[pl.BlockSpec((TM, D), lambda i: (edge(i), 0))] * 3
                  + [pl.BlockSpec((4, GRID_W, D), lambda i: (0, edge(i), 0))],
        out_shape=run_shapes,
        input_output_aliases={10: 0, 11: 1, 12: 2, 13: 3},
        compiler_params=_params(),
        name="lru_fix",
    )(u_all, u_all, u_all, *lru_weights, s0, pf, pb, sums)

    uctx3 = u_all[NT * TM:(NT + 1) * TM].reshape(SUB, CTX_SEG, D).transpose(1, 0, 2)
    chan = lambda rows: pl.BlockSpec((rows, LRU_BLOCK), lambda cb, wg: (0, cb))
    gate_spec = pl.BlockSpec((None, LRU_BLOCK, 2 * LRU_BLOCK), lambda cb, wg: (cb, 0, 0))

    def column_scans(u3d, rows, blk_rows, ngroups):
        blk = (rows, SUB, LRU_BLOCK)
        here = pl.BlockSpec(blk, lambda cb, wg: (0, wg, cb))
        return pl.pallas_call(
            functools.partial(_lru_scan_kernel, rows=rows, blk_rows=blk_rows, ngroups=ngroups),
            grid=(LRU_BLOCKS, ngroups),
            in_specs=[here,
                      pl.BlockSpec((2, SUB, LRU_BLOCK), lambda cb, wg: (rows // 2 - 1, jnp.maximum(wg - 1, 0), cb)),
                      pl.BlockSpec((2, SUB, LRU_BLOCK), lambda cb, wg: (0, jnp.minimum(wg + 1, ngroups - 1), cb)),
                      chan(CONV_W), chan(1), gate_spec, gate_spec, chan(2), chan(2), chan(2)],
            out_specs=[here, here, here, pl.BlockSpec((4, SUB, LRU_BLOCK), lambda cb, wg: (0, wg, cb))],
            out_shape=[jax.ShapeDtypeStruct((rows, ngroups * SUB, D), F32)] * 3
                      + [jax.ShapeDtypeStruct((4, ngroups * SUB, D), F32)],
            scratch_shapes=[pltpu.VMEM((rows + CONV_W - 1, SUB, LRU_BLOCK), F32)],
            compiler_params=_params(2),
            name=f"lru_scan_{ngroups * SUB}x{rows}",
        )(u3d, u3d, u3d, *lru_weights)

    ctx_sums = column_scans(uctx3, CTX_SEG, CTX_SEG, 1)[3]

    runs = NT * GRID_W
    h_f, h_b = pl.pallas_call(
        _lru_chain_kernel,
        grid=(D // CHAIN_LANES,),
        in_specs=[pl.BlockSpec((4, SUB, CHAIN_LANES), lambda l: (0, 0, l)),
                  pl.BlockSpec((4, runs, CHAIN_LANES), lambda l: (0, 0, l))],
        out_specs=[pl.BlockSpec((runs, CHAIN_LANES), lambda l: (0, l))] * 2,
        out_shape=[jax.ShapeDtypeStruct((runs, D), F32)] * 2,
        scratch_shapes=[pltpu.VMEM((runs, CHAIN_LANES), F32), pltpu.VMEM((runs, CHAIN_LANES), F32),
                        pltpu.VMEM((GRID_W, CHAIN_LANES), F32)],
        compiler_params=_params(),
        name="lru_chain",
    )(ctx_sums, sums)

    lat = lambda i: (i, 0)
    big = lambda f: pl.BlockSpec((OUT_SUBTILES * OUT_TM, D), f)
    starts = pl.BlockSpec((OUT_SUBTILES * GRID_W, D), lat)
    out = pl.pallas_call(
        _lru_out_kernel,
        grid=(SEQ // (OUT_SUBTILES * OUT_TM),),
        in_specs=[big(lat), _const((None, None, SUB, D), (1, 0, 0, 0)),
                  big(lat), big(lat), big(lat), big(lat),
                  starts, starts, _const((D, D))] + ffn_specs(1) + [_const((1, D))],
        out_specs=big(lat),
        out_shape=jax.ShapeDtypeStruct((SEQ, D), F32),
        compiler_params=_params(),
        name="lru_out",
    )(x_mid, mods, y_br, s0, pf, pb, h_f, h_b,
      lwob, vec(norm_ffn_g[1]), w1b, w3b, w2b, vec(final_norm_g))
    return out.reshape(1, SEQ, D)
```

```python
import functools

import jax
import jax.numpy as jnp
from jax import lax
from jax.experimental import pallas as pl
from jax.experimental.pallas import tpu as pltpu

D = 1024
SEQ = 16384
GRID_W = 64
ROWS = SEQ // GRID_W
CTX = 256
TM = 256
NT = SEQ // TM
HEADS = 4
DK = 512
DV = 1024
HK = DK // HEADS
HV = DV // HEADS
RANK = 16
TAU = 16.0
CHUNK = 64
NCH = TM // CHUNK
LRU_BLOCKS = 4
LRU_BLOCK = D // LRU_BLOCKS
LRU_C = 8.0
FFN = 2816
EPS = 1e-6
SUB = 8
BF16_SUBLANES = 16
N_STREAMED_WEIGHTS = 6
CONV_W = 4
CONV_LEFT = 2
NGRP = GRID_W // SUB
SCAN_ROWS = 64
CTX_SEG = CTX // SUB
SUBTILES = 2
CTX_STEP = NT // SUBTILES
OUT_TM = 256
OUT_SUBTILES = 2

F32 = jnp.float32
BF16 = jnp.bfloat16

V7X_VMEM_LIMIT = 56 * 1024 * 1024
LOG2E = 1.4426950408889634
TINY = 1e-30


def _dot(a, b):
    return jnp.dot(a, b, preferred_element_type=F32)


def _dot_nt(a, b):
    return lax.dot_general(a, b, (((1,), (1,)), ((), ())), preferred_element_type=F32)


def _dot_tn(a, b):
    return lax.dot_general(a, b, (((0,), (0,)), ((), ())), preferred_element_type=F32)


def _rms(x, g):
    ms = jnp.mean(x * x, axis=-1, keepdims=True)
    return x * lax.rsqrt(ms + EPS) * g


def _rms_mod(x, g, scale, shift):
    return _rms(x, g) * (1.0 + scale) + shift


def _silu(x):
    return x * jax.nn.sigmoid(x)


def _split_bf16(x):
    hi = x.astype(BF16)
    lo = (x - hi.astype(F32)).astype(BF16)
    return hi, lo


def _mod_kernel(c_ref, w_ref, b_ref, o_ref):
    s = _silu(c_ref[...])
    w = w_ref[...]
    b = b_ref[...]
    o_ref[...] = jnp.zeros_like(o_ref)
    for r in range(2):
        o_ref[r:r + 1, :] = jnp.sum(w * s[:, r:r + 1], axis=0, keepdims=True) + b


def _modulation(ccols, w_mod, b_mod):
    depth = w_mod.shape[0]
    tn = 1536
    return pl.pallas_call(
        _mod_kernel,
        grid=(depth, 6 * D // tn),
        in_specs=[
            pl.BlockSpec((D, 2), lambda l, n: (0, 0)),
            pl.BlockSpec((None, D, tn), lambda l, n: (l, 0, n)),
            pl.BlockSpec((None, 1, tn), lambda l, n: (l, 0, n)),
        ],
        out_specs=pl.BlockSpec((None, SUB, tn), lambda l, n: (l, 0, n)),
        out_shape=jax.ShapeDtypeStruct((depth, SUB, 6 * D), F32),
        compiler_params=pltpu.CompilerParams(
            dimension_semantics=("arbitrary", "arbitrary"),
            vmem_limit_bytes=V7X_VMEM_LIMIT),
        name="mod",
    )(ccols, w_mod, b_mod.reshape(depth, 1, 6 * D))


def _gla_tiles(jobs):
    row = lax.broadcasted_iota(jnp.int32, (TM, TM), 0)
    col = lax.broadcasted_iota(jnp.int32, (TM, TM), 1)
    shift = CHUNK.bit_length() - 1
    same = jnp.right_shift(row, shift) == jnp.right_shift(col, shift)
    prep = []
    for q, k, v, la, s_ref, o_ref, rev in jobs:
        tri = same & ((col >= row) if rev else (col <= row))
        tri_b = tri.astype(BF16)
        la_hi, la_lo = _split_bf16(la)
        b = _dot(tri_b, la_hi) + _dot(tri_b, la_lo)
        prep.append((tri, b))
    work = []
    for (q, k, v, la, s_ref, o_ref, rev), (tri, b) in zip(jobs, prep):
        ends = [c * CHUNK if rev else (c + 1) * CHUNK - 1 for c in range(NCH)]
        bl = jnp.concatenate([jnp.broadcast_to(b[r:r + 1], (CHUNK, DK)) for r in ends], axis=0)
        q_dec = (q * jnp.exp(b)).astype(BF16)
        k_inv = (k * jnp.exp(-b)).astype(BF16)
        k_end = (k * jnp.exp(bl - b)).astype(BF16)
        totals = jnp.concatenate([b[r:r + 1] for r in ends] + [jnp.zeros((SUB - NCH, DK), F32)], axis=0)
        decay_t = jnp.exp(totals).T
        order = list(range(NCH - 1, -1, -1) if rev else range(NCH))
        for h in range(HEADS):
            ks = slice(h * HK, (h + 1) * HK)
            vs = slice(h * HV, (h + 1) * HV)
            work.append(dict(tri=tri, qd=q_dec[:, ks], ki=k_inv[:, ks], ke=k_end[:, ks], vh=v[:, vs],
                             dec=decay_t[ks], order=order, s_ref=s_ref, h=h, o_ref=o_ref, vs=vs))
    for w in work:
        w["att"] = jnp.where(w["tri"], _dot_nt(w["qd"], w["ki"]), 0.0).astype(BF16)
    for w in work:
        w["o_intra"] = _dot(w["att"], w["vh"])
        w["st"] = w["s_ref"][w["h"]]
    for idx in range(NCH):
        for w in work:
            c = w["order"][idx]
            rs = slice(c * CHUNK, (c + 1) * CHUNK)
            o_inter = _dot(w["qd"][rs], w["st"].astype(BF16))
            w["o_ref"][rs, w["vs"]] = (w["o_intra"][rs] + o_inter).astype(w["o_ref"].dtype)
            u = _dot_tn(w["ke"][rs], w["vh"][rs])
            w["st"] = w["dec"][:, c:c + 1] * w["st"] + u
    for w in work:
        w["s_ref"][w["h"]] = w["st"]


def _proj_store(rs, z, zg, q_ref, k_ref, v_ref, g_ref, la_ref):
    q_ref[rs] = (z[:, :DK] * (HK ** -0.5)).astype(BF16)
    k_ref[rs] = z[:, DK:2 * DK].astype(BF16)
    v_ref[rs] = z[:, 2 * DK:2 * DK + DV].astype(BF16)
    g_ref[rs] = _silu(z[:, 2 * DK + DV:]).astype(BF16)
    la_ref[rs] = (jnp.minimum(zg, 0.0) - jnp.log(1.0 + jnp.exp(-jnp.abs(zg)))) * (1.0 / TAU)


def _gla_proj_kernel(x_ref, ctx_ref, mod_ref, gmix_ref, winf_ref, gw1_ref, gw2_ref, gb_ref,
                     q_ref, k_ref, v_ref, g_ref, la_ref, win_ref):
    @pl.when(pl.program_id(0) == 0)
    def _():
        win_ref[...] = winf_ref[...].astype(BF16)

    is_ctx = pl.program_id(0) == CTX_STEP
    mod = mod_ref[...]
    rows = [slice(j * TM, (j + 1) * TM) for j in range(SUBTILES)]
    a = [_rms_mod(jnp.where(is_ctx, ctx_ref[...], x_ref[rs]), gmix_ref[...], mod[1:2], mod[0:1]).astype(BF16)
         for rs in rows]
    pending = None
    for rs, at in zip(rows, a):
        z = _dot(at, win_ref[...])
        t = _dot(at, gw1_ref[...]).astype(BF16)
        zg = _dot(t, gw2_ref[...]) + gb_ref[...]
        if pending is not None:
            _proj_store(*pending, q_ref, k_ref, v_ref, g_ref, la_ref)
        pending = (rs, z, zg)
    _proj_store(*pending, q_ref, k_ref, v_ref, g_ref, la_ref)


def _gla_scan_kernel(qf_ref, kf_ref, vf_ref, laf_ref, qb_ref, kb_ref, vb_ref, lab_ref, *rest):
    nw = N_STREAMED_WEIGHTS
    w_in, (of_ref, ob_ref), w_out, (sf_ref, sb_ref) = (
        rest[:nw], rest[nw:nw + 2], rest[nw + 2:2 * nw + 2], rest[2 * nw + 2:])

    @pl.when(pl.program_id(0) == 0)
    def _():
        sf_ref[...] = jnp.zeros_like(sf_ref)
        sb_ref[...] = jnp.zeros_like(sb_ref)

    _gla_tiles([
        (qf_ref[...].astype(F32), kf_ref[...].astype(F32), vf_ref[...], laf_ref[...], sf_ref, of_ref, False),
        (qb_ref[...].astype(F32), kb_ref[...].astype(F32), vb_ref[...], lab_ref[...], sb_ref, ob_ref, True)])
    for src, dst in zip(w_in, w_out):
        dst[...] = src[...].astype(BF16)


def _ffn_stages(x1, mod, gffn, w1_ref, w3_ref, w2_ref):
    f = [_rms_mod(t, gffn, mod[4:5], mod[3:4]).astype(BF16) for t in x1]
    p = [(_silu(_dot(t, w1_ref[...])) * _dot(t, w3_ref[...])).astype(BF16) for t in f]
    return [t + mod[5:6] * _dot(pt, w2_ref[...]) for t, pt in zip(x1, p)]


def _gla_out_kernel(x_ref, ctx_ref, mod0_ref, mod1_ref, of0_ref, of1_ref, ob0_ref, ob1_ref,
                    g0_ref, g1_ref,
                    hn_ref, wout_ref, gffn_ref, w1_ref, w3_ref, w2_ref, gmix1_ref, lwin_ref,
                    xo_ref, y_ref, u_ref):
    is_ctx = pl.program_id(0) == CTX_STEP
    hn = hn_ref[...]
    mod0 = mod0_ref[...]
    mod1 = mod1_ref[...]
    subtiles = ((of0_ref, ob0_ref, g0_ref), (of1_ref, ob1_ref, g1_ref))
    rows = [slice(j * TM, (j + 1) * TM) for j in range(SUBTILES)]
    gffn = gffn_ref[...]
    v = [dict() for _ in subtiles]

    def gate(j):
        of_ref, ob_ref, g_ref = subtiles[j]
        y = None
        for h in range(HEADS):
            hs = slice(h * HV, (h + 1) * HV)
            o = of_ref[:, hs].astype(F32) + ob_ref[:, hs].astype(F32)
            gated = (_rms(o, hn) * g_ref[:, hs].astype(F32)).astype(BF16)
            part = _dot(gated, wout_ref[hs, :])
            y = part if y is None else y + part
        v[j]["y"] = y

    def mix_out(j):
        xt = jnp.where(is_ctx, ctx_ref[...], x_ref[rows[j]])
        v[j]["x1"] = xt + mod0[2:3] * v[j]["y"]
        v[j]["f"] = _rms_mod(v[j]["x1"], gffn, mod0[4:5], mod0[3:4]).astype(BF16)

    def ffn_up(j):
        v[j]["h1"] = _dot(v[j]["f"], w1_ref[...])
        v[j]["h3"] = _dot(v[j]["f"], w3_ref[...])

    def ffn_act(j):
        v[j]["p"] = (_silu(v[j]["h1"]) * v[j]["h3"]).astype(BF16)

    def ffn_down(j):
        x2 = v[j]["x1"] + mod0[5:6] * _dot(v[j]["p"], w2_ref[...])
        xo_ref[rows[j]] = x2
        v[j]["a"] = _rms_mod(x2, gmix1_ref[...], mod1[1:2], mod1[0:1]).astype(BF16)

    def lru_proj(j):
        v[j]["z"] = _dot(v[j]["a"], lwin_ref[...])

    def lru_store(j):
        y_ref[rows[j]] = jax.nn.gelu(v[j]["z"][:, :D]).astype(BF16)
        u_ref[rows[j]] = v[j]["z"][:, D:]

    for stage in (gate, mix_out, ffn_up, ffn_act, ffn_down, lru_proj, lru_store):
        for j in range(SUBTILES):
            stage(j)


def _lru_coeffs(ub, uch, wg_ref, bg_ref, c2, rows):
    zz = _dot(ub, wg_ref[...])
    bg = bg_ref[...]
    ta = jnp.tanh(zz[:, :LRU_BLOCK] + bg[0:1])
    tx = jnp.tanh(zz[:, LRU_BLOCK:] + bg[1:2])
    a = jnp.exp2(c2 * ta + c2)
    v = 1.0 - a * a
    s = v * lax.rsqrt(jnp.maximum(v, TINY))
    b = s * ((tx + 1.0) * uch)
    shape = (rows, SUB, LRU_BLOCK)
    return a.reshape(shape), b.reshape(shape)


def _lru_scan_kernel(u_ref, hp_ref, hn_ref, cw_ref, cb_ref, wg0_ref, wg1_ref,
                     bg0_ref, bg1_ref, lam_ref,
                     s0_ref, pf_ref, pb_ref, sum_ref,
                     e_ref, *, rows, blk_rows, ngroups):
    wg = pl.program_id(1)
    u = u_ref[...]
    prev_ok = wg >= 1
    next_ok = wg <= ngroups - 2
    sub = lax.broadcasted_iota(jnp.int32, (SUB, LRU_BLOCK), 0)
    hp = hp_ref[...]
    hn = hn_ref[...]

    def from_prev_column(cur, halo):
        edge = jnp.where(prev_ok, pltpu.roll(halo, 1, 0), 0.0)
        return jnp.where(sub == 0, edge, pltpu.roll(cur, 1, 0))

    e_ref[CONV_LEFT:CONV_LEFT + rows] = u
    e_ref[0] = from_prev_column(u[rows - 2], hp[0])
    e_ref[1] = from_prev_column(u[rows - 1], hp[1])
    edge = jnp.where(next_ok, pltpu.roll(hn[0], SUB - 1, 0), 0.0)
    e_ref[CONV_LEFT + rows] = jnp.where(sub == SUB - 1, edge, pltpu.roll(u[0], SUB - 1, 0))
    cw = 0.5 * cw_ref[...]
    cb = 0.5 * cb_ref[...]
    lam = lam_ref[...]
    c2 = (-0.5 * LRU_C * LOG2E) * jax.nn.softplus(-lam)
    nblk = rows // blk_rows
    ab = [[None] * nblk, [None] * nblk]
    for k in range(nblk):
        for d, blk in ((0, k), (1, nblk - 1 - k)):
            r0 = blk * blk_rows
            uch = cb + cw[0:1] * e_ref[r0:r0 + blk_rows]
            for j in range(1, CONV_W):
                uch = uch + cw[j:j + 1] * e_ref[r0 + j:r0 + j + blk_rows]
            uch = uch.reshape(blk_rows * SUB, LRU_BLOCK)
            ab[d][blk] = _lru_coeffs(uch.astype(BF16), uch, (wg0_ref, wg1_ref)[d], (bg0_ref, bg1_ref)[d],
                                     c2[d:d + 1], blk_rows)

    h = jnp.zeros((SUB, LRU_BLOCK), F32)
    p = jnp.ones((SUB, LRU_BLOCK), F32)
    for r in range(rows):
        a, b = ab[0][r // blk_rows]
        h = a[r % blk_rows] * h + b[r % blk_rows]
        p = a[r % blk_rows] * p
        s0_ref[r] = h
        pf_ref[r] = p
    sum_ref[0] = p
    sum_ref[1] = h
    h = jnp.zeros((SUB, LRU_BLOCK), F32)
    p = jnp.ones((SUB, LRU_BLOCK), F32)
    for r in range(rows - 1, -1, -1):
        a, b = ab[1][r // blk_rows]
        h = a[r % blk_rows] * h + b[r % blk_rows]
        p = a[r % blk_rows] * p
        s0_ref[r] = s0_ref[r] + h
        pb_ref[r] = p
    sum_ref[2] = p
    sum_ref[3] = h


def _lru_chain_kernel(csum_ref, sum_ref, hf_ref, hb_ref):
    s = jnp.zeros((1, D), F32)
    for w in range(SUB):
        s = csum_ref[0, w:w + 1, :] * s + csum_ref[1, w:w + 1, :]
    for w in range(GRID_W):
        hf_ref[w:w + 1, :] = s
        s = sum_ref[0, w:w + 1, :] * s + sum_ref[1, w:w + 1, :]
    s = jnp.zeros((1, D), F32)
    for w in range(SUB - 1, -1, -1):
        s = csum_ref[2, w:w + 1, :] * s + csum_ref[3, w:w + 1, :]
    for w in range(GRID_W - 1, -1, -1):
        hb_ref[w:w + 1, :] = s
        s = sum_ref[2, w:w + 1, :] * s + sum_ref[3, w:w + 1, :]


def _lru_out_kernel(x_ref, mod_ref, y_ref, s0_ref, pf_ref, pb_ref, hf_ref, hb_ref,
                    wout_ref, gffn_ref, w1_ref, w3_ref, w2_ref, gfin_ref, o_ref):
    mod = mod_ref[...]
    rows = [slice(j * OUT_TM, (j + 1) * OUT_TM) for j in range(OUT_SUBTILES)]
    by_col = (OUT_TM // GRID_W, GRID_W, LRU_BLOCK)
    x1 = []
    for rs in rows:
        yl = None
        for c in range(LRU_BLOCKS):
            cs = slice(c * LRU_BLOCK, (c + 1) * LRU_BLOCK)
            hs = (s0_ref[rs, cs].reshape(by_col) + pf_ref[rs, cs].reshape(by_col) * hf_ref[:, cs]
                  + pb_ref[rs, cs].reshape(by_col) * hb_ref[:, cs]).reshape(OUT_TM, LRU_BLOCK)
            ym = (hs * y_ref[rs, cs].astype(F32)).astype(BF16)
            part = _dot(ym, wout_ref[cs, :])
            yl = part if yl is None else yl + part
        x1.append(x_ref[rs] + mod[2:3] * yl)
    x2 = _ffn_stages(x1, mod, gffn_ref[...], w1_ref, w3_ref, w2_ref)
    for rs, t in zip(rows, x2):
        o_ref[rs] = _rms(t, gfin_ref[...])


def _const(shape, index=None):
    nd = len(shape)
    index = (0,) * nd if index is None else index
    return pl.BlockSpec(shape, lambda *_: index, pipeline_mode=pl.Buffered(1))


def _params(n_axes=1):
    return pltpu.CompilerParams(dimension_semantics=("arbitrary",) * n_axes,
                                vmem_limit_bytes=V7X_VMEM_LIMIT)


def kernel(x, c, ctx, c_ctx, norm_mix_g, norm_ffn_g, w_mod, b_mod, gla_w_in, gla_gate_w1, gla_gate_w2, gla_gate_b, gla_head_norm_g, gla_w_out, lru_w_in, lru_conv_w, lru_conv_b, lru_gate_a_w, lru_gate_a_b, lru_gate_x_w, lru_gate_x_b, lru_lambda, lru_w_out, ffn_w1, ffn_w3, ffn_w2, final_norm_g):
    assert x.shape == (1, SEQ, D) and ctx.shape == (1, CTX, D) and CTX == TM == ROWS
    x2d = x.reshape(SEQ, D)
    ctx2d = ctx.reshape(CTX, D)
    nseq = (NT + SUBTILES) * TM

    m = _modulation(jnp.stack([c[0], c_ctx], axis=1), w_mod, b_mod)
    mods = m[:, :2].reshape(2, 2, 6, D)
    mods = jnp.pad(mods, ((0, 0), (0, 0), (0, SUB - 6), (0, 0)))

    def mod_spec(layer):
        return pl.BlockSpec((None, None, SUB, D), lambda i: (layer, jnp.where(i == CTX_STEP, 1, 0), 0, 0))

    vec = lambda t: t.reshape(1, -1)
    tile = lambda w, f: pl.BlockSpec((TM, w), f)

    def ffn_specs(layer):
        return [_const((1, D)), _const((None, D, FFN), (layer, 0, 0)),
                _const((None, D, FFN), (layer, 0, 0)), _const((None, FFN, D), (layer, 0, 0))]

    gw1 = jnp.concatenate([gla_gate_w1[0, 0], gla_gate_w1[0, 1]], axis=1)
    gw1 = jnp.pad(gw1, ((0, 0), (0, 128 - 2 * RANK))).astype(BF16)
    gw2 = jnp.zeros((128, 2 * DK), F32)
    gw2 = gw2.at[:RANK, :DK].set(gla_gate_w2[0, 0]).at[RANK:2 * RANK, DK:].set(gla_gate_w2[0, 1])
    gw2 = gw2.astype(BF16)
    gb = gla_gate_b[0].reshape(1, 2 * DK)

    pair = lambda w: pl.BlockSpec((SUBTILES * TM, w), lambda i: (i, 0))
    lat_pair = pl.BlockSpec((SUBTILES * TM, D), lambda i: (jnp.minimum(i, CTX_STEP - 1), 0))
    q, k, v, g, la = pl.pallas_call(
        _gla_proj_kernel,
        grid=(CTX_STEP + 1,),
        in_specs=[lat_pair, _const((CTX, D)), mod_spec(0), _const((1, D)),
                  _const((D, 2 * DK + 2 * DV)), _const((D, 128)), _const((128, 2 * DK)),
                  _const((1, 2 * DK))],
        out_specs=[pair(DK), pair(DK), pair(DV), pair(DV), pair(2 * DK)],
        out_shape=[jax.ShapeDtypeStruct((nseq, DK), BF16), jax.ShapeDtypeStruct((nseq, DK), BF16),
                   jax.ShapeDtypeStruct((nseq, DV), BF16), jax.ShapeDtypeStruct((nseq, DV), BF16),
                   jax.ShapeDtypeStruct((nseq, 2 * DK), F32)],
        scratch_shapes=[pltpu.VMEM((D, 2 * DK + 2 * DV), BF16)],
        compiler_params=_params(),
        name="gla_proj",
    )(x2d, ctx2d, mods, vec(norm_mix_g[0]), gla_w_in[0], gw1, gw2, gb)

    seq_f = lambda i: (jnp.where(i == 0, NT, i - 1), 0)
    seq_b = lambda i: (jnp.where(i == 0, NT, NT - i), 0)
    streamed = [ffn_w1.reshape(2 * D, FFN), ffn_w3.reshape(2 * D, FFN), ffn_w2.reshape(2 * FFN, D),
                gla_w_out[0], lru_w_in[0], lru_w_out[0]]
    assert len(streamed) == N_STREAMED_WEIGHTS

    def slab_spec(w):
        nblk = NT if w.shape[0] % (NT * BF16_SUBLANES) == 0 else NT // 2
        rows = w.shape[0] // nblk
        assert rows * nblk == w.shape[0] and rows % BF16_SUBLANES == 0
        return pl.BlockSpec((rows, w.shape[1]), lambda i: (jnp.minimum(i, nblk - 1), 0))

    slabs = [slab_spec(w) for w in streamed]
    la_b = lambda i: (seq_b(i)[0], 1)
    o_f, o_b, w1b, w3b, w2b, gwob, lwinb, lwob = pl.pallas_call(
        _gla_scan_kernel,
        grid=(NT + 1,),
        in_specs=[tile(DK, seq_f), tile(DK, seq_f), tile(DV, seq_f), tile(DK, seq_f),
                  tile(DK, seq_b), tile(DK, seq_b), tile(DV, seq_b), tile(DK, la_b)] + slabs,
        out_specs=[tile(DV, seq_f), tile(DV, seq_b)] + slabs,
        out_shape=[jax.ShapeDtypeStruct(((NT + 1) * TM, DV), BF16)] * 2
                  + [jax.ShapeDtypeStruct(w.shape, BF16) for w in streamed],
        scratch_shapes=[pltpu.VMEM((HEADS, HK, HV), F32)] * 2,
        compiler_params=_params(),
        name="gla_scan",
    )(q, k, v, la, q, k, v, la, *streamed)
    w1b, w3b, w2b = w1b.reshape(2, D, FFN), w3b.reshape(2, D, FFN), w2b.reshape(2, FFN, D)

    sub = lambda w, j: pl.BlockSpec((TM, w), lambda i: (jnp.minimum(SUBTILES * i + j, NT), 0))
    x_mid, y_br, u_all = pl.pallas_call(
        _gla_out_kernel,
        grid=(CTX_STEP + 1,),
        in_specs=[lat_pair, _const((CTX, D)), mod_spec(0), mod_spec(1),
                  sub(DV, 0), sub(DV, 1), sub(DV, 0), sub(DV, 1), sub(DV, 0), sub(DV, 1),
                  _const((1, HV)), _const((DV, D))] + ffn_specs(0) + [_const((1, D)), _const((D, 2 * D))],
        out_specs=[pair(D), pair(D), pair(D)],
        out_shape=[jax.ShapeDtypeStruct((nseq, D), F32), jax.ShapeDtypeStruct((nseq, D), BF16),
                   jax.ShapeDtypeStruct((nseq, D), F32)],
        compiler_params=_params(),
        name="gla_out",
    )(x2d, ctx2d, mods, mods, o_f, o_f, o_b, o_b, g, g,
      vec(gla_head_norm_g[0]), gwob, vec(norm_ffn_g[0]), w1b, w3b, w2b,
      vec(norm_mix_g[1]), lwinb)

    u3 = u_all.reshape(nseq // GRID_W, GRID_W, D)
    uctx3 = u_all[NT * TM:(NT + 1) * TM].reshape(SUB, CTX_SEG, D).transpose(1, 0, 2)
    gate_w = lambda d: jnp.concatenate([lru_gate_a_w[0, d], lru_gate_x_w[0, d]], axis=-1).astype(BF16)
    gate_b = lambda d: 0.5 * jnp.stack([lru_gate_a_b[0, d], lru_gate_x_b[0, d]])
    chan = lambda rows: pl.BlockSpec((rows, LRU_BLOCK), lambda cb, wg: (0, cb))
    gate_spec = pl.BlockSpec((None, LRU_BLOCK, 2 * LRU_BLOCK), lambda cb, wg: (cb, 0, 0))
    lru_weights = (lru_conv_w[0], vec(lru_conv_b[0]), gate_w(0), gate_w(1), gate_b(0), gate_b(1),
                   lru_lambda[0])

    def column_scans(u3d, rows, blk_rows, ngroups):
        blk = (rows, SUB, LRU_BLOCK)
        here = pl.BlockSpec(blk, lambda cb, wg: (0, wg, cb))
        return pl.pallas_call(
            functools.partial(_lru_scan_kernel, rows=rows, blk_rows=blk_rows, ngroups=ngroups),
            grid=(LRU_BLOCKS, ngroups),
            in_specs=[here,
                      pl.BlockSpec((2, SUB, LRU_BLOCK), lambda cb, wg: (rows // 2 - 1, jnp.maximum(wg - 1, 0), cb)),
                      pl.BlockSpec((2, SUB, LRU_BLOCK), lambda cb, wg: (0, jnp.minimum(wg + 1, ngroups - 1), cb)),
                      chan(CONV_W), chan(1), gate_spec, gate_spec, chan(2), chan(2), chan(2)],
            out_specs=[here, here, here, pl.BlockSpec((4, SUB, LRU_BLOCK), lambda cb, wg: (0, wg, cb))],
            out_shape=[jax.ShapeDtypeStruct((rows, ngroups * SUB, D), F32)] * 3
                      + [jax.ShapeDtypeStruct((4, ngroups * SUB, D), F32)],
            scratch_shapes=[pltpu.VMEM((rows + CONV_W - 1, SUB, LRU_BLOCK), F32)],
            compiler_params=_params(2),
            name=f"lru_scan_{ngroups * SUB}x{rows}",
        )(u3d, u3d, u3d, *lru_weights)

    ctx_sums = column_scans(uctx3, CTX_SEG, CTX_SEG, 1)[3]
    s0, pf, pb, sums = column_scans(u3, ROWS, SCAN_ROWS, NGRP)

    h_f, h_b = pl.pallas_call(
        _lru_chain_kernel,
        out_shape=[jax.ShapeDtypeStruct((GRID_W, D), F32)] * 2,
        name="lru_chain",
    )(ctx_sums, sums)

    lat = lambda i: (i, 0)
    big = lambda f: pl.BlockSpec((OUT_SUBTILES * OUT_TM, D), f)
    out = pl.pallas_call(
        _lru_out_kernel,
        grid=(SEQ // (OUT_SUBTILES * OUT_TM),),
        in_specs=[big(lat), _const((None, None, SUB, D), (1, 0, 0, 0)),
                  big(lat), big(lat), big(lat), big(lat),
                  _const((GRID_W, D)), _const((GRID_W, D)), _const((D, D))] + ffn_specs(1) + [_const((1, D))],
        out_specs=big(lat),
        out_shape=jax.ShapeDtypeStruct((SEQ, D), F32),
        compiler_params=_params(),
        name="lru_out",
    )(x_mid, mods, y_br, s0.reshape(SEQ, D), pf.reshape(SEQ, D), pb.reshape(SEQ, D),
      h_f, h_b,
      lwob, vec(norm_ffn_g[1]), w1b, w3b, w2b, vec(final_norm_g))
    return out.reshape(1, SEQ, D)
```

```python
import functools

import jax
import jax.numpy as jnp
from jax import lax
from jax.experimental import pallas as pl
from jax.experimental.pallas import tpu as pltpu

D = 1024
SEQ = 16384
GRID_W = 64
ROWS = SEQ // GRID_W
CTX = 256
TM = 256
NT = SEQ // TM
HEADS = 4
DK = 512
DV = 1024
HK = DK // HEADS
HV = DV // HEADS
RANK = 16
TAU = 16.0
CHUNK = 64
NCH = TM // CHUNK
LRU_BLOCKS = 4
LRU_BLOCK = D // LRU_BLOCKS
LRU_C = 8.0
FFN = 2816
EPS = 1e-6
SUB = 8
BF16_SUBLANES = 16
N_STREAMED_WEIGHTS = 6
CONV_W = 4
CONV_LEFT = 2
NGRP = GRID_W // SUB
SCAN_ROWS = 64
PROJ_CHUNK = 512
CTX_SEG = CTX // SUB
SUBTILES = 2
CTX_STEP = NT // SUBTILES
OUT_TM = 256
OUT_SUBTILES = 2

F32 = jnp.float32
BF16 = jnp.bfloat16

V7X_VMEM_LIMIT = 56 * 1024 * 1024
LOG2E = 1.4426950408889634
TINY = 1e-30


def _dot(a, b):
    return jnp.dot(a, b, preferred_element_type=F32)


def _dot_nt(a, b):
    return lax.dot_general(a, b, (((1,), (1,)), ((), ())), preferred_element_type=F32)


def _dot_tn(a, b):
    return lax.dot_general(a, b, (((0,), (0,)), ((), ())), preferred_element_type=F32)


def _rms(x, g):
    ms = jnp.mean(x * x, axis=-1, keepdims=True)
    return x * lax.rsqrt(ms + EPS) * g


def _rms_mod(x, g, scale, shift):
    return _rms(x, g) * (1.0 + scale) + shift


def _silu(x):
    return x * jax.nn.sigmoid(x)


def _split_bf16(x):
    hi = x.astype(BF16)
    lo = (x - hi.astype(F32)).astype(BF16)
    return hi, lo


def _mod_kernel(c_ref, w_ref, b_ref, o_ref):
    s = _silu(c_ref[...])
    w = w_ref[...]
    b = b_ref[...]
    o_ref[...] = jnp.zeros_like(o_ref)
    for r in range(2):
        o_ref[r:r + 1, :] = jnp.sum(w * s[:, r:r + 1], axis=0, keepdims=True) + b


def _modulation(ccols, w_mod, b_mod):
    depth = w_mod.shape[0]
    tn = 1536
    return pl.pallas_call(
        _mod_kernel,
        grid=(depth, 6 * D // tn),
        in_specs=[
            pl.BlockSpec((D, 2), lambda l, n: (0, 0)),
            pl.BlockSpec((None, D, tn), lambda l, n: (l, 0, n)),
            pl.BlockSpec((None, 1, tn), lambda l, n: (l, 0, n)),
        ],
        out_specs=pl.BlockSpec((None, SUB, tn), lambda l, n: (l, 0, n)),
        out_shape=jax.ShapeDtypeStruct((depth, SUB, 6 * D), F32),
        compiler_params=pltpu.CompilerParams(
            dimension_semantics=("arbitrary", "arbitrary"),
            vmem_limit_bytes=V7X_VMEM_LIMIT),
        name="mod",
    )(ccols, w_mod, b_mod.reshape(depth, 1, 6 * D))


def _gla_tiles(jobs):
    row = lax.broadcasted_iota(jnp.int32, (TM, TM), 0)
    col = lax.broadcasted_iota(jnp.int32, (TM, TM), 1)
    shift = CHUNK.bit_length() - 1
    same = jnp.right_shift(row, shift) == jnp.right_shift(col, shift)
    prep = []
    for q, k, v, la, s_ref, o_ref, rev in jobs:
        tri = same & ((col >= row) if rev else (col <= row))
        tri_b = tri.astype(BF16)
        la_hi, la_lo = _split_bf16(la)
        b = _dot(tri_b, la_hi) + _dot(tri_b, la_lo)
        prep.append((tri, b))
    work = []
    for (q, k, v, la, s_ref, o_ref, rev), (tri, b) in zip(jobs, prep):
        ends = [c * CHUNK if rev else (c + 1) * CHUNK - 1 for c in range(NCH)]
        bl = jnp.concatenate([jnp.broadcast_to(b[r:r + 1], (CHUNK, DK)) for r in ends], axis=0)
        q_dec = (q * jnp.exp(b)).astype(BF16)
        k_inv = (k * jnp.exp(-b)).astype(BF16)
        k_end = (k * jnp.exp(bl - b)).astype(BF16)
        totals = jnp.concatenate([b[r:r + 1] for r in ends] + [jnp.zeros((SUB - NCH, DK), F32)], axis=0)
        decay_t = jnp.exp(totals).T
        order = list(range(NCH - 1, -1, -1) if rev else range(NCH))
        for h in range(HEADS):
            ks = slice(h * HK, (h + 1) * HK)
            vs = slice(h * HV, (h + 1) * HV)
            work.append(dict(tri=tri, qd=q_dec[:, ks], ki=k_inv[:, ks], ke=k_end[:, ks], vh=v[:, vs],
                             dec=decay_t[ks], order=order, s_ref=s_ref, h=h, o_ref=o_ref, vs=vs))
    for w in work:
        w["att"] = jnp.where(w["tri"], _dot_nt(w["qd"], w["ki"]), 0.0).astype(BF16)
    for w in work:
        w["o_intra"] = _dot(w["att"], w["vh"])
        w["st"] = w["s_ref"][w["h"]]
    for idx in range(NCH):
        for w in work:
            c = w["order"][idx]
            rs = slice(c * CHUNK, (c + 1) * CHUNK)
            o_inter = _dot(w["qd"][rs], w["st"].astype(BF16))
            w["o_ref"][rs, w["vs"]] = (w["o_intra"][rs] + o_inter).astype(w["o_ref"].dtype)
            u = _dot_tn(w["ke"][rs], w["vh"][rs])
            w["st"] = w["dec"][:, c:c + 1] * w["st"] + u
    for w in work:
        w["s_ref"][w["h"]] = w["st"]


def _gla_proj_kernel(x_ref, ctx_ref, mod_ref, gmix_ref, winf_ref, gw1_ref, gw2_ref, gb_ref,
                     q_ref, k_ref, v_ref, g_ref, la_ref, win_ref):
    @pl.when(pl.program_id(0) == 0)
    def _():
        win_ref[...] = winf_ref[...].astype(BF16)

    is_ctx = pl.program_id(0) == CTX_STEP
    mod = mod_ref[...]
    pieces = ([(g_ref, c, 2 * DK + DV + c, _silu) for c in range(0, DV, PROJ_CHUNK)]
              + [(q_ref, c, c, lambda t: t * (HK ** -0.5)) for c in range(0, DK, PROJ_CHUNK)]
              + [(v_ref, c, 2 * DK + c, lambda t: t) for c in range(0, DV, PROJ_CHUNK)]
              + [(k_ref, c, DK + c, lambda t: t) for c in range(0, DK, PROJ_CHUNK)])
    for j in range(SUBTILES):
        rs = slice(j * TM, (j + 1) * TM)
        xt = jnp.where(is_ctx, ctx_ref[...], x_ref[rs])
        a = _rms_mod(xt, gmix_ref[...], mod[1:2], mod[0:1]).astype(BF16)
        t = _dot(a, gw1_ref[...]).astype(BF16)
        zg = _dot(t, gw2_ref[...]) + gb_ref[...]
        for n, (dst, c, col, post) in enumerate(pieces):
            z = _dot(a, win_ref[:, col:col + PROJ_CHUNK])
            dst[rs, c:c + PROJ_CHUNK] = post(z).astype(BF16)
            if n == len(pieces) // 2:
                la_ref[rs] = (jnp.minimum(zg, 0.0) - jnp.log(1.0 + jnp.exp(-jnp.abs(zg)))) * (1.0 / TAU)


def _gla_scan_kernel(qf_ref, kf_ref, vf_ref, laf_ref, qb_ref, kb_ref, vb_ref, lab_ref, *rest):
    nw = N_STREAMED_WEIGHTS
    w_in, (of_ref, ob_ref), w_out, (sf_ref, sb_ref) = (
        rest[:nw], rest[nw:nw + 2], rest[nw + 2:2 * nw + 2], rest[2 * nw + 2:])

    @pl.when(pl.program_id(0) == 0)
    def _():
        sf_ref[...] = jnp.zeros_like(sf_ref)
        sb_ref[...] = jnp.zeros_like(sb_ref)

    _gla_tiles([
        (qf_ref[...].astype(F32), kf_ref[...].astype(F32), vf_ref[...], laf_ref[...], sf_ref, of_ref, False),
        (qb_ref[...].astype(F32), kb_ref[...].astype(F32), vb_ref[...], lab_ref[...], sb_ref, ob_ref, True)])
    for src, dst in zip(w_in, w_out):
        dst[...] = src[...].astype(BF16)


def _ffn_stages(x1, mod, gffn, w1_ref, w3_ref, w2_ref):
    f = [_rms_mod(t, gffn, mod[4:5], mod[3:4]).astype(BF16) for t in x1]
    p = [(_silu(_dot(t, w1_ref[...])) * _dot(t, w3_ref[...])).astype(BF16) for t in f]
    return [t + mod[5:6] * _dot(pt, w2_ref[...]) for t, pt in zip(x1, p)]


def _gla_out_kernel(x_ref, ctx_ref, mod0_ref, mod1_ref, of0_ref, of1_ref, ob0_ref, ob1_ref,
                    g0_ref, g1_ref,
                    hn_ref, wout_ref, gffn_ref, w1_ref, w3_ref, w2_ref, gmix1_ref, lwin_ref,
                    xo_ref, y_ref, u_ref):
    is_ctx = pl.program_id(0) == CTX_STEP
    hn = hn_ref[...]
    mod0 = mod0_ref[...]
    mod1 = mod1_ref[...]
    subtiles = ((of0_ref, ob0_ref, g0_ref), (of1_ref, ob1_ref, g1_ref))
    rows = [slice(j * TM, (j + 1) * TM) for j in range(SUBTILES)]
    gffn = gffn_ref[...]
    v = [dict() for _ in subtiles]

    def gate(j):
        of_ref, ob_ref, g_ref = subtiles[j]
        y = None
        for h in range(HEADS):
            hs = slice(h * HV, (h + 1) * HV)
            o = of_ref[:, hs].astype(F32) + ob_ref[:, hs].astype(F32)
            gated = (_rms(o, hn) * g_ref[:, hs].astype(F32)).astype(BF16)
            part = _dot(gated, wout_ref[hs, :])
            y = part if y is None else y + part
        v[j]["y"] = y

    def mix_out(j):
        xt = jnp.where(is_ctx, ctx_ref[...], x_ref[rows[j]])
        v[j]["x1"] = xt + mod0[2:3] * v[j]["y"]
        v[j]["f"] = _rms_mod(v[j]["x1"], gffn, mod0[4:5], mod0[3:4]).astype(BF16)

    def ffn_up(j):
        v[j]["h1"] = _dot(v[j]["f"], w1_ref[...])
        v[j]["h3"] = _dot(v[j]["f"], w3_ref[...])

    def ffn_act(j):
        v[j]["p"] = (_silu(v[j]["h1"]) * v[j]["h3"]).astype(BF16)

    def ffn_down(j):
        x2 = v[j]["x1"] + mod0[5:6] * _dot(v[j]["p"], w2_ref[...])
        xo_ref[rows[j]] = x2
        v[j]["a"] = _rms_mod(x2, gmix1_ref[...], mod1[1:2], mod1[0:1]).astype(BF16)

    def lru_proj(j):
        v[j]["z"] = _dot(v[j]["a"], lwin_ref[...])

    def lru_store(j):
        y_ref[rows[j]] = jax.nn.gelu(v[j]["z"][:, :D]).astype(BF16)
        u_ref[rows[j]] = v[j]["z"][:, D:]

    for stage in (gate, mix_out, ffn_up, ffn_act, ffn_down, lru_proj, lru_store):
        for j in range(SUBTILES):
            stage(j)


def _lru_coeffs(ub, uch, wg_ref, bg_ref, c2, rows):
    zz = _dot(ub, wg_ref[...])
    bg = bg_ref[...]
    ta = jnp.tanh(zz[:, :LRU_BLOCK] + bg[0:1])
    tx = jnp.tanh(zz[:, LRU_BLOCK:] + bg[1:2])
    a = jnp.exp2(c2 * ta + c2)
    v = 1.0 - a * a
    s = v * lax.rsqrt(jnp.maximum(v, TINY))
    b = s * ((tx + 1.0) * uch)
    shape = (rows, SUB, LRU_BLOCK)
    return a.reshape(shape), b.reshape(shape)


def _lru_scan_kernel(u_ref, hp_ref, hn_ref, cw_ref, cb_ref, wg0_ref, wg1_ref,
                     bg0_ref, bg1_ref, lam_ref,
                     s0_ref, pf_ref, pb_ref, sum_ref,
                     e_ref, *, rows, blk_rows, ngroups):
    wg = pl.program_id(1)
    u = u_ref[...]
    prev_ok = wg >= 1
    next_ok = wg <= ngroups - 2
    sub = lax.broadcasted_iota(jnp.int32, (SUB, LRU_BLOCK), 0)
    hp = hp_ref[...]
    hn = hn_ref[...]

    def from_prev_column(cur, halo):
        edge = jnp.where(prev_ok, pltpu.roll(halo, 1, 0), 0.0)
        return jnp.where(sub == 0, edge, pltpu.roll(cur, 1, 0))

    e_ref[CONV_LEFT:CONV_LEFT + rows] = u
    e_ref[0] = from_prev_column(u[rows - 2], hp[0])
    e_ref[1] = from_prev_column(u[rows - 1], hp[1])
    edge = jnp.where(next_ok, pltpu.roll(hn[0], SUB - 1, 0), 0.0)
    e_ref[CONV_LEFT + rows] = jnp.where(sub == SUB - 1, edge, pltpu.roll(u[0], SUB - 1, 0))
    cw = 0.5 * cw_ref[...]
    cb = 0.5 * cb_ref[...]
    lam = lam_ref[...]
    c2 = (-0.5 * LRU_C * LOG2E) * jax.nn.softplus(-lam)
    nblk = rows // blk_rows
    ab = [[None] * nblk, [None] * nblk]
    for k in range(nblk):
        for d, blk in ((0, k), (1, nblk - 1 - k)):
            r0 = blk * blk_rows
            uch = cb + cw[0:1] * e_ref[r0:r0 + blk_rows]
            for j in range(1, CONV_W):
                uch = uch + cw[j:j + 1] * e_ref[r0 + j:r0 + j + blk_rows]
            uch = uch.reshape(blk_rows * SUB, LRU_BLOCK)
            ab[d][blk] = _lru_coeffs(uch.astype(BF16), uch, (wg0_ref, wg1_ref)[d], (bg0_ref, bg1_ref)[d],
                                     c2[d:d + 1], blk_rows)

    h = jnp.zeros((SUB, LRU_BLOCK), F32)
    p = jnp.ones((SUB, LRU_BLOCK), F32)
    for r in range(rows):
        a, b = ab[0][r // blk_rows]
        h = a[r % blk_rows] * h + b[r % blk_rows]
        p = a[r % blk_rows] * p
        s0_ref[r] = h
        pf_ref[r] = p
    sum_ref[0] = p
    sum_ref[1] = h
    h = jnp.zeros((SUB, LRU_BLOCK), F32)
    p = jnp.ones((SUB, LRU_BLOCK), F32)
    for r in range(rows - 1, -1, -1):
        a, b = ab[1][r // blk_rows]
        h = a[r % blk_rows] * h + b[r % blk_rows]
        p = a[r % blk_rows] * p
        s0_ref[r] = s0_ref[r] + h
        pb_ref[r] = p
    sum_ref[2] = p
    sum_ref[3] = h


def _lru_chain_kernel(csum_ref, sum_ref, hf_ref, hb_ref):
    s = jnp.zeros((1, D), F32)
    for w in range(SUB):
        s = csum_ref[0, w:w + 1, :] * s + csum_ref[1, w:w + 1, :]
    for w in range(GRID_W):
        hf_ref[w:w + 1, :] = s
        s = sum_ref[0, w:w + 1, :] * s + sum_ref[1, w:w + 1, :]
    s = jnp.zeros((1, D), F32)
    for w in range(SUB - 1, -1, -1):
        s = csum_ref[2, w:w + 1, :] * s + csum_ref[3, w:w + 1, :]
    for w in range(GRID_W - 1, -1, -1):
        hb_ref[w:w + 1, :] = s
        s = sum_ref[2, w:w + 1, :] * s + sum_ref[3, w:w + 1, :]


def _lru_out_kernel(x_ref, mod_ref, y_ref, s0_ref, pf_ref, pb_ref, hf_ref, hb_ref,
                    wout_ref, gffn_ref, w1_ref, w3_ref, w2_ref, gfin_ref, o_ref):
    mod = mod_ref[...]
    rows = [slice(j * OUT_TM, (j + 1) * OUT_TM) for j in range(OUT_SUBTILES)]
    by_col = (OUT_TM // GRID_W, GRID_W, LRU_BLOCK)
    x1 = []
    for rs in rows:
        yl = None
        for c in range(LRU_BLOCKS):
            cs = slice(c * LRU_BLOCK, (c + 1) * LRU_BLOCK)
            hs = (s0_ref[rs, cs].reshape(by_col) + pf_ref[rs, cs].reshape(by_col) * hf_ref[:, cs]
                  + pb_ref[rs, cs].reshape(by_col) * hb_ref[:, cs]).reshape(OUT_TM, LRU_BLOCK)
            ym = (hs * y_ref[rs, cs].astype(F32)).astype(BF16)
            part = _dot(ym, wout_ref[cs, :])
            yl = part if yl is None else yl + part
        x1.append(x_ref[rs] + mod[2:3] * yl)
    x2 = _ffn_stages(x1, mod, gffn_ref[...], w1_ref, w3_ref, w2_ref)
    for rs, t in zip(rows, x2):
        o_ref[rs] = _rms(t, gfin_ref[...])


def _const(shape, index=None):
    nd = len(shape)
    index = (0,) * nd if index is None else index
    return pl.BlockSpec(shape, lambda *_: index, pipeline_mode=pl.Buffered(1))


def _params(n_axes=1):
    return pltpu.CompilerParams(dimension_semantics=("arbitrary",) * n_axes,
                                vmem_limit_bytes=V7X_VMEM_LIMIT)


def kernel(x, c, ctx, c_ctx, norm_mix_g, norm_ffn_g, w_mod, b_mod, gla_w_in, gla_gate_w1, gla_gate_w2, gla_gate_b, gla_head_norm_g, gla_w_out, lru_w_in, lru_conv_w, lru_conv_b, lru_gate_a_w, lru_gate_a_b, lru_gate_x_w, lru_gate_x_b, lru_lambda, lru_w_out, ffn_w1, ffn_w3, ffn_w2, final_norm_g):
    assert x.shape == (1, SEQ, D) and ctx.shape == (1, CTX, D) and CTX == TM == ROWS
    x2d = x.reshape(SEQ, D)
    ctx2d = ctx.reshape(CTX, D)
    nseq = (NT + SUBTILES) * TM

    m = _modulation(jnp.stack([c[0], c_ctx], axis=1), w_mod, b_mod)
    mods = m[:, :2].reshape(2, 2, 6, D)
    mods = jnp.pad(mods, ((0, 0), (0, 0), (0, SUB - 6), (0, 0)))

    def mod_spec(layer):
        return pl.BlockSpec((None, None, SUB, D), lambda i: (layer, jnp.where(i == CTX_STEP, 1, 0), 0, 0))

    vec = lambda t: t.reshape(1, -1)
    tile = lambda w, f: pl.BlockSpec((TM, w), f)

    def ffn_specs(layer):
        return [_const((1, D)), _const((None, D, FFN), (layer, 0, 0)),
                _const((None, D, FFN), (layer, 0, 0)), _const((None, FFN, D), (layer, 0, 0))]

    gw1 = jnp.concatenate([gla_gate_w1[0, 0], gla_gate_w1[0, 1]], axis=1)
    gw1 = jnp.pad(gw1, ((0, 0), (0, 128 - 2 * RANK))).astype(BF16)
    gw2 = jnp.zeros((128, 2 * DK), F32)
    gw2 = gw2.at[:RANK, :DK].set(gla_gate_w2[0, 0]).at[RANK:2 * RANK, DK:].set(gla_gate_w2[0, 1])
    gw2 = gw2.astype(BF16)
    gb = gla_gate_b[0].reshape(1, 2 * DK)

    pair = lambda w: pl.BlockSpec((SUBTILES * TM, w), lambda i: (i, 0))
    lat_pair = pl.BlockSpec((SUBTILES * TM, D), lambda i: (jnp.minimum(i, CTX_STEP - 1), 0))
    q, k, v, g, la = pl.pallas_call(
        _gla_proj_kernel,
        grid=(CTX_STEP + 1,),
        in_specs=[lat_pair, _const((CTX, D)), mod_spec(0), _const((1, D)),
                  _const((D, 2 * DK + 2 * DV)), _const((D, 128)), _const((128, 2 * DK)),
                  _const((1, 2 * DK))],
        out_specs=[pair(DK), pair(DK), pair(DV), pair(DV), pair(2 * DK)],
        out_shape=[jax.ShapeDtypeStruct((nseq, DK), BF16), jax.ShapeDtypeStruct((nseq, DK), BF16),
                   jax.ShapeDtypeStruct((nseq, DV), BF16), jax.ShapeDtypeStruct((nseq, DV), BF16),
                   jax.ShapeDtypeStruct((nseq, 2 * DK), F32)],
        scratch_shapes=[pltpu.VMEM((D, 2 * DK + 2 * DV), BF16)],
        compiler_params=_params(),
        name="gla_proj",
    )(x2d, ctx2d, mods, vec(norm_mix_g[0]), gla_w_in[0], gw1, gw2, gb)

    seq_f = lambda i: (jnp.where(i == 0, NT, i - 1), 0)
    seq_b = lambda i: (jnp.where(i == 0, NT, NT - i), 0)
    streamed = [ffn_w1.reshape(2 * D, FFN), ffn_w3.reshape(2 * D, FFN), ffn_w2.reshape(2 * FFN, D),
                gla_w_out[0], lru_w_in[0], lru_w_out[0]]
    assert len(streamed) == N_STREAMED_WEIGHTS

    def slab_spec(w):
        nblk = NT if w.shape[0] % (NT * BF16_SUBLANES) == 0 else NT // 2
        rows = w.shape[0] // nblk
        assert rows * nblk == w.shape[0] and rows % BF16_SUBLANES == 0
        return pl.BlockSpec((rows, w.shape[1]), lambda i: (jnp.minimum(i, nblk - 1), 0))

    slabs = [slab_spec(w) for w in streamed]
    la_b = lambda i: (seq_b(i)[0], 1)
    o_f, o_b, w1b, w3b, w2b, gwob, lwinb, lwob = pl.pallas_call(
        _gla_scan_kernel,
        grid=(NT + 1,),
        in_specs=[tile(DK, seq_f), tile(DK, seq_f), tile(DV, seq_f), tile(DK, seq_f),
                  tile(DK, seq_b), tile(DK, seq_b), tile(DV, seq_b), tile(DK, la_b)] + slabs,
        out_specs=[tile(DV, seq_f), tile(DV, seq_b)] + slabs,
        out_shape=[jax.ShapeDtypeStruct(((NT + 1) * TM, DV), BF16)] * 2
                  + [jax.ShapeDtypeStruct(w.shape, BF16) for w in streamed],
        scratch_shapes=[pltpu.VMEM((HEADS, HK, HV), F32)] * 2,
        compiler_params=_params(),
        name="gla_scan",
    )(q, k, v, la, q, k, v, la, *streamed)
    w1b, w3b, w2b = w1b.reshape(2, D, FFN), w3b.reshape(2, D, FFN), w2b.reshape(2, FFN, D)

    sub = lambda w, j: pl.BlockSpec((TM, w), lambda i: (jnp.minimum(SUBTILES * i + j, NT), 0))
    x_mid, y_br, u_all = pl.pallas_call(
        _gla_out_kernel,
        grid=(CTX_STEP + 1,),
        in_specs=[lat_pair, _const((CTX, D)), mod_spec(0), mod_spec(1),
                  sub(DV, 0), sub(DV, 1), sub(DV, 0), sub(DV, 1), sub(DV, 0), sub(DV, 1),
                  _const((1, HV)), _const((DV, D))] + ffn_specs(0) + [_const((1, D)), _const((D, 2 * D))],
        out_specs=[pair(D), pair(D), pair(D)],
        out_shape=[jax.ShapeDtypeStruct((nseq, D), F32), jax.ShapeDtypeStruct((nseq, D), BF16),
                   jax.ShapeDtypeStruct((nseq, D), F32)],
        compiler_params=_params(),
        name="gla_out",
    )(x2d, ctx2d, mods, mods, o_f, o_f, o_b, o_b, g, g,
      vec(gla_head_norm_g[0]), gwob, vec(norm_ffn_g[0]), w1b, w3b, w2b,
      vec(norm_mix_g[1]), lwinb)

    u3 = u_all.reshape(nseq // GRID_W, GRID_W, D)
    uctx3 = u_all[NT * TM:(NT + 1) * TM].reshape(SUB, CTX_SEG, D).transpose(1, 0, 2)
    gate_w = lambda d: jnp.concatenate([lru_gate_a_w[0, d], lru_gate_x_w[0, d]], axis=-1).astype(BF16)
    gate_b = lambda d: 0.5 * jnp.stack([lru_gate_a_b[0, d], lru_gate_x_b[0, d]])
    chan = lambda rows: pl.BlockSpec((rows, LRU_BLOCK), lambda cb, wg: (0, cb))
    gate_spec = pl.BlockSpec((None, LRU_BLOCK, 2 * LRU_BLOCK), lambda cb, wg: (cb, 0, 0))
    lru_weights = (lru_conv_w[0], vec(lru_conv_b[0]), gate_w(0), gate_w(1), gate_b(0), gate_b(1),
                   lru_lambda[0])

    def column_scans(u3d, rows, blk_rows, ngroups):
        blk = (rows, SUB, LRU_BLOCK)
        here = pl.BlockSpec(blk, lambda cb, wg: (0, wg, cb))
        return pl.pallas_call(
            functools.partial(_lru_scan_kernel, rows=rows, blk_rows=blk_rows, ngroups=ngroups),
            grid=(LRU_BLOCKS, ngroups),
            in_specs=[here,
                      pl.BlockSpec((2, SUB, LRU_BLOCK), lambda cb, wg: (rows // 2 - 1, jnp.maximum(wg - 1, 0), cb)),
                      pl.BlockSpec((2, SUB, LRU_BLOCK), lambda cb, wg: (0, jnp.minimum(wg + 1, ngroups - 1), cb)),
                      chan(CONV_W), chan(1), gate_spec, gate_spec, chan(2), chan(2), chan(2)],
            out_specs=[here, here, here, pl.BlockSpec((4, SUB, LRU_BLOCK), lambda cb, wg: (0, wg, cb))],
            out_shape=[jax.ShapeDtypeStruct((rows, ngroups * SUB, D), F32)] * 3
                      + [jax.ShapeDtypeStruct((4, ngroups * SUB, D), F32)],
            scratch_shapes=[pltpu.VMEM((rows + CONV_W - 1, SUB, LRU_BLOCK), F32)],
            compiler_params=_params(2),
            name=f"lru_scan_{ngroups * SUB}x{rows}",
        )(u3d, u3d, u3d, *lru_weights)

    ctx_sums = column_scans(uctx3, CTX_SEG, CTX_SEG, 1)[3]
    s0, pf, pb, sums = column_scans(u3, ROWS, SCAN_ROWS, NGRP)

    h_f, h_b = pl.pallas_call(
        _lru_chain_kernel,
        out_shape=[jax.ShapeDtypeStruct((GRID_W, D), F32)] * 2,
        name="lru_chain",
    )(ctx_sums, sums)

    lat = lambda i: (i, 0)
    big = lambda f: pl.BlockSpec((OUT_SUBTILES * OUT_TM, D), f)
    out = pl.pallas_call(
        _lru_out_kernel,
        grid=(SEQ // (OUT_SUBTILES * OUT_TM),),
        in_specs=[big(lat), _const((None, None, SUB, D), (1, 0, 0, 0)),
                  big(lat), big(lat), big(lat), big(lat),
                  _const((GRID_W, D)), _const((GRID_W, D)), _const((D, D))] + ffn_specs(1) + [_const((1, D))],
        out_specs=big(lat),
        out_shape=jax.ShapeDtypeStruct((SEQ, D), F32),
        compiler_params=_params(),
        name="lru_out",
    )(x_mid, mods, y_br, s0.reshape(SEQ, D), pf.reshape(SEQ, D), pb.reshape(SEQ, D),
      h_f, h_b,
      lwob, vec(norm_ffn_g[1]), w1b, w3b, w2b, vec(final_norm_g))
    return out.reshape(1, SEQ, D)
```

```python
import functools

import jax
import jax.numpy as jnp
from jax import lax
from jax.experimental import pallas as pl
from jax.experimental.pallas import tpu as pltpu

D = 1024
SEQ = 16384
GRID_W = 64
ROWS = SEQ // GRID_W
CTX = 256
TM = 256
NT = SEQ // TM
HEADS = 4
DK = 512
DV = 1024
HK = DK // HEADS
HV = DV // HEADS
RANK = 16
TAU = 16.0
CHUNK = 64
NCH = TM // CHUNK
LRU_BLOCKS = 4
LRU_BLOCK = D // LRU_BLOCKS
LRU_C = 8.0
FFN = 2816
EPS = 1e-6
SUB = 8
LANES = 128
BF16_SUBLANES = 16
N_STREAMED_WEIGHTS = 6
CONV_W = 4
CONV_LEFT = 2
NGRP = GRID_W // SUB
SCAN_ROWS = 32
CTX_SEG = CTX // SUB
SUBTILES = 2
CTX_STEP = NT // SUBTILES
OUT_TM = 256
OUT_SUBTILES = 2

F32 = jnp.float32
BF16 = jnp.bfloat16

V7X_VMEM_LIMIT = 56 * 1024 * 1024
LOG2E = 1.4426950408889634
TINY = 1e-30


def _dot(a, b):
    return jnp.dot(a, b, preferred_element_type=F32)


def _dot_nt(a, b):
    return lax.dot_general(a, b, (((1,), (1,)), ((), ())), preferred_element_type=F32)


def _dot_tn(a, b):
    return lax.dot_general(a, b, (((0,), (0,)), ((), ())), preferred_element_type=F32)


def _rms(x, g):
    ms = jnp.mean(x * x, axis=-1, keepdims=True)
    return x * lax.rsqrt(ms + EPS) * g


def _rms_mod(x, g, scale, shift):
    return _rms(x, g) * (1.0 + scale) + shift


def _silu(x):
    return x * jax.nn.sigmoid(x)


def _split_bf16(x):
    hi = x.astype(BF16)
    lo = (x - hi.astype(F32)).astype(BF16)
    return hi, lo


def _mod_kernel(c_ref, w_ref, b_ref, o_ref):
    s = _silu(c_ref[...])
    w = w_ref[...]
    b = b_ref[...]
    o_ref[...] = jnp.zeros_like(o_ref)
    for r in range(2):
        o_ref[r:r + 1, :] = jnp.sum(w * s[:, r:r + 1], axis=0, keepdims=True) + b


def _modulation(ccols, w_mod, b_mod):
    depth = w_mod.shape[0]
    tn = 1536
    return pl.pallas_call(
        _mod_kernel,
        grid=(depth, 6 * D // tn),
        in_specs=[
            pl.BlockSpec((D, 2), lambda l, n: (0, 0)),
            pl.BlockSpec((None, D, tn), lambda l, n: (l, 0, n)),
            pl.BlockSpec((None, 1, tn), lambda l, n: (l, 0, n)),
        ],
        out_specs=pl.BlockSpec((None, SUB, tn), lambda l, n: (l, 0, n)),
        out_shape=jax.ShapeDtypeStruct((depth, SUB, 6 * D), F32),
        compiler_params=pltpu.CompilerParams(
            dimension_semantics=("arbitrary", "arbitrary"),
            vmem_limit_bytes=V7X_VMEM_LIMIT),
        name="mod",
    )(ccols, w_mod, b_mod.reshape(depth, 1, 6 * D))


def _gla_tiles(jobs):
    row = lax.broadcasted_iota(jnp.int32, (TM, TM), 0)
    col = lax.broadcasted_iota(jnp.int32, (TM, TM), 1)
    shift = CHUNK.bit_length() - 1
    same = jnp.right_shift(row, shift) == jnp.right_shift(col, shift)
    prep = []
    for q, k, v, la, s_ref, o_ref, rev in jobs:
        tri = same & ((col >= row) if rev else (col <= row))
        tri_b = tri.astype(BF16)
        la_hi, la_lo = _split_bf16(la)
        b = _dot(tri_b, la_hi) + _dot(tri_b, la_lo)
        prep.append((tri, b))
    work = []
    for (q, k, v, la, s_ref, o_ref, rev), (tri, b) in zip(jobs, prep):
        ends = [c * CHUNK if rev else (c + 1) * CHUNK - 1 for c in range(NCH)]
        bl = jnp.concatenate([jnp.broadcast_to(b[r:r + 1], (CHUNK, DK)) for r in ends], axis=0)
        q_dec = (q * jnp.exp(b)).astype(BF16)
        k_inv = (k * jnp.exp(-b)).astype(BF16)
        k_end = (k * jnp.exp(bl - b)).astype(BF16)
        totals = jnp.concatenate([b[r:r + 1] for r in ends] + [jnp.zeros((SUB - NCH, DK), F32)], axis=0)
        decay_t = jnp.exp(totals).T
        order = list(range(NCH - 1, -1, -1) if rev else range(NCH))
        for h in range(HEADS):
            ks = slice(h * HK, (h + 1) * HK)
            vs = slice(h * HV, (h + 1) * HV)
            work.append(dict(tri=tri, qd=q_dec[:, ks], ki=k_inv[:, ks], ke=k_end[:, ks], vh=v[:, vs],
                             dec=decay_t[ks], order=order, s_ref=s_ref, h=h, o_ref=o_ref, vs=vs))
    for w in work:
        w["att"] = jnp.where(w["tri"], _dot_nt(w["qd"], w["ki"]), 0.0).astype(BF16)
    for w in work:
        w["o_intra"] = _dot(w["att"], w["vh"])
        w["st"] = w["s_ref"][w["h"]]
    for idx in range(NCH):
        for w in work:
            c = w["order"][idx]
            rs = slice(c * CHUNK, (c + 1) * CHUNK)
            o_inter = _dot(w["qd"][rs], w["st"].astype(BF16))
            w["o_ref"][rs, w["vs"]] = (w["o_intra"][rs] + o_inter).astype(w["o_ref"].dtype)
            u = _dot_tn(w["ke"][rs], w["vh"][rs])
            w["st"] = w["dec"][:, c:c + 1] * w["st"] + u
    for w in work:
        w["s_ref"][w["h"]] = w["st"]


def _proj_store(rs, z, zg, q_ref, k_ref, v_ref, g_ref, la_ref):
    q_ref[rs] = (z[:, :DK] * (HK ** -0.5)).astype(BF16)
    k_ref[rs] = z[:, DK:2 * DK].astype(BF16)
    v_ref[rs] = z[:, 2 * DK:2 * DK + DV].astype(BF16)
    g_ref[rs] = _silu(z[:, 2 * DK + DV:]).astype(BF16)
    la_ref[rs] = (jnp.minimum(zg, 0.0) - jnp.log(1.0 + jnp.exp(-jnp.abs(zg)))) * (1.0 / TAU)


def _gla_proj_kernel(x_ref, ctx_ref, mod_ref, gmix_ref, winf_ref, gw1_ref, gw2_ref, gb_ref,
                     q_ref, k_ref, v_ref, g_ref, la_ref, win_ref):
    @pl.when(pl.program_id(0) == 0)
    def _():
        win_ref[...] = winf_ref[...].astype(BF16)

    is_ctx = pl.program_id(0) == CTX_STEP
    mod = mod_ref[...]
    rows = [slice(j * TM, (j + 1) * TM) for j in range(SUBTILES)]
    a = [_rms_mod(jnp.where(is_ctx, ctx_ref[...], x_ref[rs]), gmix_ref[...], mod[1:2], mod[0:1]).astype(BF16)
         for rs in rows]
    pending = None
    for rs, at in zip(rows, a):
        z = _dot(at, win_ref[...])
        t = _dot(at, gw1_ref[...]).astype(BF16)
        zg = _dot(t, gw2_ref[...]) + gb_ref[...]
        if pending is not None:
            _proj_store(*pending, q_ref, k_ref, v_ref, g_ref, la_ref)
        pending = (rs, z, zg)
    _proj_store(*pending, q_ref, k_ref, v_ref, g_ref, la_ref)


def _gla_scan_kernel(qf_ref, kf_ref, vf_ref, laf_ref, qb_ref, kb_ref, vb_ref, lab_ref, *rest):
    nw = N_STREAMED_WEIGHTS
    w_in, (of_ref, ob_ref), w_out, (sf_ref, sb_ref) = (
        rest[:nw], rest[nw:nw + 2], rest[nw + 2:2 * nw + 2], rest[2 * nw + 2:])

    @pl.when(pl.program_id(0) == 0)
    def _():
        sf_ref[...] = jnp.zeros_like(sf_ref)
        sb_ref[...] = jnp.zeros_like(sb_ref)

    _gla_tiles([
        (qf_ref[...].astype(F32), kf_ref[...].astype(F32), vf_ref[...], laf_ref[...], sf_ref, of_ref, False),
        (qb_ref[...].astype(F32), kb_ref[...].astype(F32), vb_ref[...], lab_ref[...], sb_ref, ob_ref, True)])
    for src, dst in zip(w_in, w_out):
        dst[...] = src[...].astype(BF16)


def _ffn_stages(x1, mod, gffn, w1_ref, w3_ref, w2_ref):
    f = [_rms_mod(t, gffn, mod[4:5], mod[3:4]).astype(BF16) for t in x1]
    p = [(_silu(_dot(t, w1_ref[...])) * _dot(t, w3_ref[...])).astype(BF16) for t in f]
    return [t + mod[5:6] * _dot(pt, w2_ref[...]) for t, pt in zip(x1, p)]


def _gla_out_kernel(x_ref, ctx_ref, mod0_ref, mod1_ref, of0_ref, of1_ref, ob0_ref, ob1_ref,
                    g0_ref, g1_ref,
                    hn_ref, wout_ref, gffn_ref, w1_ref, w3_ref, w2_ref, gmix1_ref, lwin_ref,
                    xo_ref, y_ref, u_ref):
    is_ctx = pl.program_id(0) == CTX_STEP
    hn = hn_ref[...]
    mod0 = mod0_ref[...]
    mod1 = mod1_ref[...]
    subtiles = ((of0_ref, ob0_ref, g0_ref), (of1_ref, ob1_ref, g1_ref))
    rows = [slice(j * TM, (j + 1) * TM) for j in range(SUBTILES)]
    gffn = gffn_ref[...]
    v = [dict() for _ in subtiles]

    def gate(j):
        of_ref, ob_ref, g_ref = subtiles[j]
        y = None
        for h in range(HEADS):
            hs = slice(h * HV, (h + 1) * HV)
            o = of_ref[:, hs].astype(F32) + ob_ref[:, hs].astype(F32)
            gated = (_rms(o, hn) * g_ref[:, hs].astype(F32)).astype(BF16)
            part = _dot(gated, wout_ref[hs, :])
            y = part if y is None else y + part
        v[j]["y"] = y

    def mix_out(j):
        xt = jnp.where(is_ctx, ctx_ref[...], x_ref[rows[j]])
        v[j]["x1"] = xt + mod0[2:3] * v[j]["y"]
        v[j]["f"] = _rms_mod(v[j]["x1"], gffn, mod0[4:5], mod0[3:4]).astype(BF16)

    def ffn_up(j):
        v[j]["h1"] = _dot(v[j]["f"], w1_ref[...])
        v[j]["h3"] = _dot(v[j]["f"], w3_ref[...])

    def ffn_act(j):
        v[j]["p"] = (_silu(v[j]["h1"]) * v[j]["h3"]).astype(BF16)

    def ffn_down(j):
        x2 = v[j]["x1"] + mod0[5:6] * _dot(v[j]["p"], w2_ref[...])
        xo_ref[rows[j]] = x2
        v[j]["a"] = _rms_mod(x2, gmix1_ref[...], mod1[1:2], mod1[0:1]).astype(BF16)

    def lru_proj(j):
        v[j]["z"] = _dot(v[j]["a"], lwin_ref[...])

    def lru_store(j):
        y_ref[rows[j]] = jax.nn.gelu(v[j]["z"][:, :D]).astype(BF16)
        u_ref[rows[j]] = v[j]["z"][:, D:]

    for stage in (gate, mix_out, ffn_up, ffn_act, ffn_down, lru_proj, lru_store):
        for j in range(SUBTILES):
            stage(j)


def _lru_coeffs(ub, uch, wg_ref, bg_ref, c2, rows):
    zz = _dot(ub, wg_ref[...])
    bg = bg_ref[...]
    ta = jnp.tanh(zz[:, :LRU_BLOCK] + bg[0:1])
    tx = jnp.tanh(zz[:, LRU_BLOCK:] + bg[1:2])
    a = jnp.exp2(c2 * ta + c2)
    v = 1.0 - a * a
    s = v * lax.rsqrt(jnp.maximum(v, TINY))
    b = s * ((tx + 1.0) * uch)
    shape = (rows, SUB, LRU_BLOCK)
    return a.reshape(shape), b.reshape(shape)


def _lru_scan_kernel(u_ref, hp_ref, hn_ref, cw_ref, cb_ref, wg0_ref, wg1_ref,
                     bg0_ref, bg1_ref, lam_ref,
                     s0_ref, pf_ref, pb_ref, sum_ref,
                     e_ref, *, rows, blk_rows, ngroups):
    wg = pl.program_id(1)
    u = u_ref[...]
    prev_ok = wg >= 1
    next_ok = wg <= ngroups - 2
    sub = lax.broadcasted_iota(jnp.int32, (SUB, LRU_BLOCK), 0)
    hp = hp_ref[...]
    hn = hn_ref[...]

    def from_prev_column(cur, halo):
        edge = jnp.where(prev_ok, pltpu.roll(halo, 1, 0), 0.0)
        return jnp.where(sub == 0, edge, pltpu.roll(cur, 1, 0))

    e_ref[CONV_LEFT:CONV_LEFT + rows] = u
    e_ref[0] = from_prev_column(u[rows - 2], hp[0])
    e_ref[1] = from_prev_column(u[rows - 1], hp[1])
    edge = jnp.where(next_ok, pltpu.roll(hn[0], SUB - 1, 0), 0.0)
    e_ref[CONV_LEFT + rows] = jnp.where(sub == SUB - 1, edge, pltpu.roll(u[0], SUB - 1, 0))
    cw = 0.5 * cw_ref[...]
    cb = 0.5 * cb_ref[...]
    lam = lam_ref[...]
    c2 = (-0.5 * LRU_C * LOG2E) * jax.nn.softplus(-lam)
    nblk = rows // blk_rows
    ab = [[None] * nblk, [None] * nblk]
    for k in range(nblk):
        for d, blk in ((0, k), (1, nblk - 1 - k)):
            r0 = blk * blk_rows
            uch = cb + cw[0:1] * e_ref[r0:r0 + blk_rows]
            for j in range(1, CONV_W):
                uch = uch + cw[j:j + 1] * e_ref[r0 + j:r0 + j + blk_rows]
            uch = uch.reshape(blk_rows * SUB, LRU_BLOCK)
            ab[d][blk] = _lru_coeffs(uch.astype(BF16), uch, (wg0_ref, wg1_ref)[d], (bg0_ref, bg1_ref)[d],
                                     c2[d:d + 1], blk_rows)

    h = jnp.zeros((SUB, LRU_BLOCK), F32)
    p = jnp.ones((SUB, LRU_BLOCK), F32)
    for r in range(rows):
        a, b = ab[0][r // blk_rows]
        h = a[r % blk_rows] * h + b[r % blk_rows]
        p = a[r % blk_rows] * p
        s0_ref[r] = h
        pf_ref[r] = p
    sum_ref[0] = p
    sum_ref[1] = h
    h = jnp.zeros((SUB, LRU_BLOCK), F32)
    p = jnp.ones((SUB, LRU_BLOCK), F32)
    for r in range(rows - 1, -1, -1):
        a, b = ab[1][r // blk_rows]
        h = a[r % blk_rows] * h + b[r % blk_rows]
        p = a[r % blk_rows] * p
        s0_ref[r] = s0_ref[r] + h
        pb_ref[r] = p
    sum_ref[2] = p
    sum_ref[3] = h


def _lru_chain_kernel(csum_ref, sum_ref, hf_ref, hb_ref):
    s = jnp.zeros((1, D), F32)
    for w in range(SUB):
        s = csum_ref[0, w:w + 1, :] * s + csum_ref[1, w:w + 1, :]
    for w in range(GRID_W):
        hf_ref[w:w + 1, :] = s
        s = sum_ref[0, w:w + 1, :] * s + sum_ref[1, w:w + 1, :]
    s = jnp.zeros((1, D), F32)
    for w in range(SUB - 1, -1, -1):
        s = csum_ref[2, w:w + 1, :] * s + csum_ref[3, w:w + 1, :]
    for w in range(GRID_W - 1, -1, -1):
        hb_ref[w:w + 1, :] = s
        s = sum_ref[2, w:w + 1, :] * s + sum_ref[3, w:w + 1, :]


def _lru_out_kernel(x_ref, mod_ref, y_ref, s0_ref, pf_ref, pb_ref, hf_ref, hb_ref,
                    wout_ref, gffn_ref, w1_ref, w3_ref, w2_ref, gfin_ref, o_ref):
    mod = mod_ref[...]
    rows = [slice(j * OUT_TM, (j + 1) * OUT_TM) for j in range(OUT_SUBTILES)]
    by_col = (OUT_TM // GRID_W, GRID_W, LRU_BLOCK)
    x1 = []
    for rs in rows:
        yl = None
        for c in range(LRU_BLOCKS):
            cs = slice(c * LRU_BLOCK, (c + 1) * LRU_BLOCK)
            hs = (s0_ref[rs, cs].reshape(by_col) + pf_ref[rs, cs].reshape(by_col) * hf_ref[:, cs]
                  + pb_ref[rs, cs].reshape(by_col) * hb_ref[:, cs]).reshape(OUT_TM, LRU_BLOCK)
            ym = (hs * y_ref[rs, cs].astype(F32)).astype(BF16)
            part = _dot(ym, wout_ref[cs, :])
            yl = part if yl is None else yl + part
        x1.append(x_ref[rs] + mod[2:3] * yl)
    x2 = _ffn_stages(x1, mod, gffn_ref[...], w1_ref, w3_ref, w2_ref)
    for rs, t in zip(rows, x2):
        o_ref[rs] = _rms(t, gfin_ref[...])


def _const(shape, index=None):
    nd = len(shape)
    index = (0,) * nd if index is None else index
    return pl.BlockSpec(shape, lambda *_: index, pipeline_mode=pl.Buffered(1))


def _params(n_axes=1):
    return pltpu.CompilerParams(dimension_semantics=("arbitrary",) * n_axes,
                                vmem_limit_bytes=V7X_VMEM_LIMIT)


def kernel(x, c, ctx, c_ctx, norm_mix_g, norm_ffn_g, w_mod, b_mod, gla_w_in, gla_gate_w1, gla_gate_w2, gla_gate_b, gla_head_norm_g, gla_w_out, lru_w_in, lru_conv_w, lru_conv_b, lru_gate_a_w, lru_gate_a_b, lru_gate_x_w, lru_gate_x_b, lru_lambda, lru_w_out, ffn_w1, ffn_w3, ffn_w2, final_norm_g):
    assert x.shape == (1, SEQ, D) and ctx.shape == (1, CTX, D) and CTX == TM == ROWS
    x2d = x.reshape(SEQ, D)
    ctx2d = ctx.reshape(CTX, D)
    nseq = (NT + SUBTILES) * TM

    m = _modulation(jnp.stack([c[0], c_ctx], axis=1), w_mod, b_mod)
    mods = m[:, :2].reshape(2, 2, 6, D)
    mods = jnp.pad(mods, ((0, 0), (0, 0), (0, SUB - 6), (0, 0)))

    def mod_spec(layer):
        return pl.BlockSpec((None, None, SUB, D), lambda i: (layer, jnp.where(i == CTX_STEP, 1, 0), 0, 0))

    vec = lambda t: t.reshape(1, -1)
    tile = lambda w, f: pl.BlockSpec((TM, w), f)

    def ffn_specs(layer):
        return [_const((1, D)), _const((None, D, FFN), (layer, 0, 0)),
                _const((None, D, FFN), (layer, 0, 0)), _const((None, FFN, D), (layer, 0, 0))]

    gw1 = jnp.concatenate([gla_gate_w1[0, 0], gla_gate_w1[0, 1]], axis=1)
    gw1 = jnp.pad(gw1, ((0, 0), (0, LANES - 2 * RANK))).astype(BF16)
    gw2 = jnp.zeros((LANES, 2 * DK), F32)
    gw2 = gw2.at[:RANK, :DK].set(gla_gate_w2[0, 0]).at[RANK:2 * RANK, DK:].set(gla_gate_w2[0, 1])
    gw2 = gw2.astype(BF16)
    gb = gla_gate_b[0].reshape(1, 2 * DK)

    pair = lambda w: pl.BlockSpec((SUBTILES * TM, w), lambda i: (i, 0))
    lat_pair = pl.BlockSpec((SUBTILES * TM, D), lambda i: (jnp.minimum(i, CTX_STEP - 1), 0))
    q, k, v, g, la = pl.pallas_call(
        _gla_proj_kernel,
        grid=(CTX_STEP + 1,),
        in_specs=[lat_pair, _const((CTX, D)), mod_spec(0), _const((1, D)),
                  _const((D, 2 * DK + 2 * DV)), _const((D, LANES)), _const((LANES, 2 * DK)),
                  _const((1, 2 * DK))],
        out_specs=[pair(DK), pair(DK), pair(DV), pair(DV), pair(2 * DK)],
        out_shape=[jax.ShapeDtypeStruct((nseq, DK), BF16), jax.ShapeDtypeStruct((nseq, DK), BF16),
                   jax.ShapeDtypeStruct((nseq, DV), BF16), jax.ShapeDtypeStruct((nseq, DV), BF16),
                   jax.ShapeDtypeStruct((nseq, 2 * DK), F32)],
        scratch_shapes=[pltpu.VMEM((D, 2 * DK + 2 * DV), BF16)],
        compiler_params=_params(),
        name="gla_proj",
    )(x2d, ctx2d, mods, vec(norm_mix_g[0]), gla_w_in[0], gw1, gw2, gb)

    seq_f = lambda i: (jnp.where(i == 0, NT, i - 1), 0)
    seq_b = lambda i: (jnp.where(i == 0, NT, NT - i), 0)
    streamed = [ffn_w1.reshape(2 * D, FFN), ffn_w3.reshape(2 * D, FFN), ffn_w2.reshape(2 * FFN, D),
                gla_w_out[0], lru_w_in[0], lru_w_out[0]]
    assert len(streamed) == N_STREAMED_WEIGHTS

    def slab_spec(w):
        nblk = NT if w.shape[0] % (NT * BF16_SUBLANES) == 0 else NT // 2
        rows = w.shape[0] // nblk
        assert rows * nblk == w.shape[0] and rows % BF16_SUBLANES == 0
        return pl.BlockSpec((rows, w.shape[1]), lambda i: (jnp.minimum(i, nblk - 1), 0))

    slabs = [slab_spec(w) for w in streamed]
    la_b = lambda i: (seq_b(i)[0], 1)
    o_f, o_b, w1b, w3b, w2b, gwob, lwinb, lwob = pl.pallas_call(
        _gla_scan_kernel,
        grid=(NT + 1,),
        in_specs=[tile(DK, seq_f), tile(DK, seq_f), tile(DV, seq_f), tile(DK, seq_f),
                  tile(DK, seq_b), tile(DK, seq_b), tile(DV, seq_b), tile(DK, la_b)] + slabs,
        out_specs=[tile(DV, seq_f), tile(DV, seq_b)] + slabs,
        out_shape=[jax.ShapeDtypeStruct(((NT + 1) * TM, DV), BF16)] * 2
                  + [jax.ShapeDtypeStruct(w.shape, BF16) for w in streamed],
        scratch_shapes=[pltpu.VMEM((HEADS, HK, HV), F32)] * 2,
        compiler_params=_params(),
        name="gla_scan",
    )(q, k, v, la, q, k, v, la, *streamed)
    w1b, w3b, w2b = w1b.reshape(2, D, FFN), w3b.reshape(2, D, FFN), w2b.reshape(2, FFN, D)

    sub = lambda w, j: pl.BlockSpec((TM, w), lambda i: (jnp.minimum(SUBTILES * i + j, NT), 0))
    x_mid, y_br, u_all = pl.pallas_call(
        _gla_out_kernel,
        grid=(CTX_STEP + 1,),
        in_specs=[lat_pair, _const((CTX, D)), mod_spec(0), mod_spec(1),
                  sub(DV, 0), sub(DV, 1), sub(DV, 0), sub(DV, 1), sub(DV, 0), sub(DV, 1),
                  _const((1, HV)), _const((DV, D))] + ffn_specs(0) + [_const((1, D)), _const((D, 2 * D))],
        out_specs=[pair(D), pair(D), pair(D)],
        out_shape=[jax.ShapeDtypeStruct((nseq, D), F32), jax.ShapeDtypeStruct((nseq, D), BF16),
                   jax.ShapeDtypeStruct((nseq, D), F32)],
        compiler_params=_params(),
        name="gla_out",
    )(x2d, ctx2d, mods, mods, o_f, o_f, o_b, o_b, g, g,
      vec(gla_head_norm_g[0]), gwob, vec(norm_ffn_g[0]), w1b, w3b, w2b,
      vec(norm_mix_g[1]), lwinb)

    u3 = u_all.reshape(nseq // GRID_W, GRID_W, D)
    uctx3 = u_all[NT * TM:(NT + 1) * TM].reshape(SUB, CTX_SEG, D).transpose(1, 0, 2)
    gate_w = lambda d: jnp.concatenate([lru_gate_a_w[0, d], lru_gate_x_w[0, d]], axis=-1).astype(BF16)
    gate_b = lambda d: 0.5 * jnp.stack([lru_gate_a_b[0, d], lru_gate_x_b[0, d]])
    chan = lambda rows: pl.BlockSpec((rows, LRU_BLOCK), lambda cb, wg: (0, cb))
    gate_spec = pl.BlockSpec((None, LRU_BLOCK, 2 * LRU_BLOCK), lambda cb, wg: (cb, 0, 0))
    lru_weights = (lru_conv_w[0], vec(lru_conv_b[0]), gate_w(0), gate_w(1), gate_b(0), gate_b(1),
                   lru_lambda[0])

    def column_scans(u3d, rows, blk_rows, ngroups):
        blk = (rows, SUB, LRU_BLOCK)
        here = pl.BlockSpec(blk, lambda cb, wg: (0, wg, cb))
        return pl.pallas_call(
            functools.partial(_lru_scan_kernel, rows=rows, blk_rows=blk_rows, ngroups=ngroups),
            grid=(LRU_BLOCKS, ngroups),
            in_specs=[here,
                      pl.BlockSpec((2, SUB, LRU_BLOCK), lambda cb, wg: (rows // 2 - 1, jnp.maximum(wg - 1, 0), cb)),
                      pl.BlockSpec((2, SUB, LRU_BLOCK), lambda cb, wg: (0, jnp.minimum(wg + 1, ngroups - 1), cb)),
                      chan(CONV_W), chan(1), gate_spec, gate_spec, chan(2), chan(2), chan(2)],
            out_specs=[here, here, here, pl.BlockSpec((4, SUB, LRU_BLOCK), lambda cb, wg: (0, wg, cb))],
            out_shape=[jax.ShapeDtypeStruct((rows, ngroups * SUB, D), F32)] * 3
                      + [jax.ShapeDtypeStruct((4, ngroups * SUB, D), F32)],
            scratch_shapes=[pltpu.VMEM((rows + CONV_W - 1, SUB, LRU_BLOCK), F32)],
            compiler_params=_params(2),
            name=f"lru_scan_{ngroups * SUB}x{rows}",
        )(u3d, u3d, u3d, *lru_weights)

    ctx_sums = column_scans(uctx3, CTX_SEG, CTX_SEG, 1)[3]
    s0, pf, pb, sums = column_scans(u3, ROWS, SCAN_ROWS, NGRP)

    h_f, h_b = pl.pallas_call(
        _lru_chain_kernel,
        out_shape=[jax.ShapeDtypeStruct((GRID_W, D), F32)] * 2,
        name="lru_chain",
    )(ctx_sums, sums)

    lat = lambda i: (i, 0)
    big = lambda f: pl.BlockSpec((OUT_SUBTILES * OUT_TM, D), f)
    out = pl.pallas_call(
        _lru_out_kernel,
        grid=(SEQ // (OUT_SUBTILES * OUT_TM),),
        in_specs=[big(lat), _const((None, None, SUB, D), (1, 0, 0, 0)),
                  big(lat), big(lat), big(lat), big(lat),
                  _const((GRID_W, D)), _const((GRID_W, D)), _const((D, D))] + ffn_specs(1) + [_const((1, D))],
        out_specs=big(lat),
        out_shape=jax.ShapeDtypeStruct((SEQ, D), F32),
        compiler_params=_params(),
        name="lru_out",
    )(x_mid, mods, y_br, s0.reshape(SEQ, D), pf.reshape(SEQ, D), pb.reshape(SEQ, D),
      h_f, h_b,
      lwob, vec(norm_ffn_g[1]), w1b, w3b, w2b, vec(final_norm_g))
    return out.reshape(1, SEQ, D)
```

```python
import functools

import jax
import jax.numpy as jnp
from jax import lax
from jax.experimental import pallas as pl
from jax.experimental.pallas import tpu as pltpu

D = 1024
SEQ = 16384
GRID_W = 64
ROWS = SEQ // GRID_W
CTX = 256
TM = 256
NT = SEQ // TM
HEADS = 4
DK = 512
DV = 1024
HK = DK // HEADS
HV = DV // HEADS
RANK = 16
TAU = 16.0
CHUNK = 64
NCH = TM // CHUNK
LRU_BLOCKS = 4
LRU_BLOCK = D // LRU_BLOCKS
LRU_C = 8.0
FFN = 2816
EPS = 1e-6
SUB = 8
LANES = 128
BF16_SUBLANES = 16
N_STREAMED_WEIGHTS = 6
CONV_W = 4
CONV_LEFT = 2
NGRP = GRID_W // SUB
SCAN_ROWS = 32
CTX_SEG = CTX // SUB
SUBTILES = 2
CTX_STEP = NT // SUBTILES
OUT_TM = 256
OUT_SUBTILES = 2

F32 = jnp.float32
BF16 = jnp.bfloat16

V7X_VMEM_LIMIT = 56 * 1024 * 1024
LOG2E = 1.4426950408889634
TINY = 1e-30


def _dot(a, b):
    return jnp.dot(a, b, preferred_element_type=F32)


def _dot_nt(a, b):
    return lax.dot_general(a, b, (((1,), (1,)), ((), ())), preferred_element_type=F32)


def _dot_tn(a, b):
    return lax.dot_general(a, b, (((0,), (0,)), ((), ())), preferred_element_type=F32)


def _rms(x, g):
    ms = jnp.mean(x * x, axis=-1, keepdims=True)
    return x * lax.rsqrt(ms + EPS) * g


def _rms_mod(x, g, scale, shift):
    return _rms(x, g * (1.0 + scale)) + shift


def _silu(x):
    return x * jax.nn.sigmoid(x)


def _split_bf16(x):
    hi = x.astype(BF16)
    lo = (x - hi.astype(F32)).astype(BF16)
    return hi, lo


def _mod_kernel(c_ref, w_ref, b_ref, o_ref):
    s = _silu(c_ref[...])
    w = w_ref[...]
    b = b_ref[...]
    o_ref[...] = jnp.zeros_like(o_ref)
    for r in range(2):
        o_ref[r:r + 1, :] = jnp.sum(w * s[:, r:r + 1], axis=0, keepdims=True) + b


def _modulation(ccols, w_mod, b_mod):
    depth = w_mod.shape[0]
    tn = 1536
    return pl.pallas_call(
        _mod_kernel,
        grid=(depth, 6 * D // tn),
        in_specs=[
            pl.BlockSpec((D, 2), lambda l, n: (0, 0)),
            pl.BlockSpec((None, D, tn), lambda l, n: (l, 0, n)),
            pl.BlockSpec((None, 1, tn), lambda l, n: (l, 0, n)),
        ],
        out_specs=pl.BlockSpec((None, SUB, tn), lambda l, n: (l, 0, n)),
        out_shape=jax.ShapeDtypeStruct((depth, SUB, 6 * D), F32),
        compiler_params=pltpu.CompilerParams(
            dimension_semantics=("arbitrary", "arbitrary"),
            vmem_limit_bytes=V7X_VMEM_LIMIT),
        name="mod",
    )(ccols, w_mod, b_mod.reshape(depth, 1, 6 * D))


def _gla_tiles(jobs):
    row = lax.broadcasted_iota(jnp.int32, (TM, TM), 0)
    col = lax.broadcasted_iota(jnp.int32, (TM, TM), 1)
    shift = CHUNK.bit_length() - 1
    same = jnp.right_shift(row, shift) == jnp.right_shift(col, shift)
    prep = []
    for q, k, v, la, s_ref, o_ref, rev in jobs:
        tri = same & ((col >= row) if rev else (col <= row))
        tri_b = tri.astype(BF16)
        la_hi, la_lo = _split_bf16(la)
        b = _dot(tri_b, la_hi) + _dot(tri_b, la_lo)
        prep.append((tri, b))
    work = []
    for (q, k, v, la, s_ref, o_ref, rev), (tri, b) in zip(jobs, prep):
        ends = [c * CHUNK if rev else (c + 1) * CHUNK - 1 for c in range(NCH)]
        bl = jnp.concatenate([jnp.broadcast_to(b[r:r + 1], (CHUNK, DK)) for r in ends], axis=0)
        q_dec = (q * jnp.exp(b)).astype(BF16)
        k_inv = (k * jnp.exp(-b)).astype(BF16)
        k_end = (k * jnp.exp(bl - b)).astype(BF16)
        totals = jnp.concatenate([b[r:r + 1] for r in ends] + [jnp.zeros((SUB - NCH, DK), F32)], axis=0)
        decay_t = jnp.exp(totals).T
        order = list(range(NCH - 1, -1, -1) if rev else range(NCH))
        for h in range(HEADS):
            ks = slice(h * HK, (h + 1) * HK)
            vs = slice(h * HV, (h + 1) * HV)
            work.append(dict(tri=tri, qd=q_dec[:, ks], ki=k_inv[:, ks], ke=k_end[:, ks], vh=v[:, vs],
                             dec=decay_t[ks], order=order, s_ref=s_ref, h=h, o_ref=o_ref, vs=vs))
    for w in work:
        w["att"] = jnp.where(w["tri"], _dot_nt(w["qd"], w["ki"]), 0.0).astype(BF16)
    for w in work:
        w["o_intra"] = _dot(w["att"], w["vh"])
        w["st"] = w["s_ref"][w["h"]]
    for idx in range(NCH):
        for w in work:
            c = w["order"][idx]
            rs = slice(c * CHUNK, (c + 1) * CHUNK)
            o_inter = _dot(w["qd"][rs], w["st"].astype(BF16))
            w["o_ref"][rs, w["vs"]] = (w["o_intra"][rs] + o_inter).astype(w["o_ref"].dtype)
            u = _dot_tn(w["ke"][rs], w["vh"][rs])
            w["st"] = w["dec"][:, c:c + 1] * w["st"] + u
    for w in work:
        w["s_ref"][w["h"]] = w["st"]


def _proj_store(rs, z, zg, q_ref, k_ref, v_ref, g_ref, la_ref):
    q_ref[rs] = (z[:, :DK] * (HK ** -0.5)).astype(BF16)
    k_ref[rs] = z[:, DK:2 * DK].astype(BF16)
    v_ref[rs] = z[:, 2 * DK:2 * DK + DV].astype(BF16)
    g_ref[rs] = _silu(z[:, 2 * DK + DV:]).astype(BF16)
    la_ref[rs] = (jnp.minimum(zg, 0.0) - jnp.log(1.0 + jnp.exp(-jnp.abs(zg)))) * (1.0 / TAU)


def _gla_proj_kernel(x_ref, ctx_ref, mod_ref, gmix_ref, winf_ref, gw1_ref, gw2_ref, gb_ref,
                     q_ref, k_ref, v_ref, g_ref, la_ref, win_ref):
    @pl.when(pl.program_id(0) == 0)
    def _():
        win_ref[...] = winf_ref[...].astype(BF16)

    is_ctx = pl.program_id(0) == CTX_STEP
    mod = mod_ref[...]
    rows = [slice(j * TM, (j + 1) * TM) for j in range(SUBTILES)]
    a = [_rms_mod(jnp.where(is_ctx, ctx_ref[...], x_ref[rs]), gmix_ref[...], mod[1:2], mod[0:1]).astype(BF16)
         for rs in rows]
    pending = None
    for rs, at in zip(rows, a):
        z = _dot(at, win_ref[...])
        t = _dot(at, gw1_ref[...]).astype(BF16)
        zg = _dot(t, gw2_ref[...]) + gb_ref[...]
        if pending is not None:
            _proj_store(*pending, q_ref, k_ref, v_ref, g_ref, la_ref)
        pending = (rs, z, zg)
    _proj_store(*pending, q_ref, k_ref, v_ref, g_ref, la_ref)


def _gla_scan_kernel(qf_ref, kf_ref, vf_ref, laf_ref, qb_ref, kb_ref, vb_ref, lab_ref, *rest):
    nw = N_STREAMED_WEIGHTS
    w_in, (of_ref, ob_ref), w_out, (sf_ref, sb_ref) = (
        rest[:nw], rest[nw:nw + 2], rest[nw + 2:2 * nw + 2], rest[2 * nw + 2:])

    @pl.when(pl.program_id(0) == 0)
    def _():
        sf_ref[...] = jnp.zeros_like(sf_ref)
        sb_ref[...] = jnp.zeros_like(sb_ref)

    _gla_tiles([
        (qf_ref[...].astype(F32), kf_ref[...].astype(F32), vf_ref[...], laf_ref[...], sf_ref, of_ref, False),
        (qb_ref[...].astype(F32), kb_ref[...].astype(F32), vb_ref[...], lab_ref[...], sb_ref, ob_ref, True)])
    for src, dst in zip(w_in, w_out):
        dst[...] = src[...].astype(BF16)


def _ffn_stages(x1, mod, gffn, w1_ref, w3_ref, w2_ref):
    f = [_rms_mod(t, gffn, mod[4:5], mod[3:4]).astype(BF16) for t in x1]
    p = [(_silu(_dot(t, w1_ref[...])) * _dot(t, w3_ref[...])).astype(BF16) for t in f]
    return [t + mod[5:6] * _dot(pt, w2_ref[...]) for t, pt in zip(x1, p)]


def _gla_out_kernel(x_ref, ctx_ref, mod0_ref, mod1_ref, of0_ref, of1_ref, ob0_ref, ob1_ref,
                    g0_ref, g1_ref,
                    hn_ref, wout_ref, gffn_ref, w1_ref, w3_ref, w2_ref, gmix1_ref, lwin_ref,
                    xo_ref, y_ref, u_ref):
    is_ctx = pl.program_id(0) == CTX_STEP
    hn = hn_ref[...]
    mod0 = mod0_ref[...]
    mod1 = mod1_ref[...]
    subtiles = ((of0_ref, ob0_ref, g0_ref), (of1_ref, ob1_ref, g1_ref))
    rows = [slice(j * TM, (j + 1) * TM) for j in range(SUBTILES)]
    gffn = gffn_ref[...]
    v = [dict() for _ in subtiles]

    def gate(j):
        of_ref, ob_ref, g_ref = subtiles[j]
        y = None
        for h in range(HEADS):
            hs = slice(h * HV, (h + 1) * HV)
            o = of_ref[:, hs].astype(F32) + ob_ref[:, hs].astype(F32)
            gated = (_rms(o, hn) * g_ref[:, hs].astype(F32)).astype(BF16)
            part = _dot(gated, wout_ref[hs, :])
            y = part if y is None else y + part
        v[j]["y"] = y

    def mix_out(j):
        xt = jnp.where(is_ctx, ctx_ref[...], x_ref[rows[j]])
        v[j]["x1"] = xt + mod0[2:3] * v[j]["y"]
        v[j]["f"] = _rms_mod(v[j]["x1"], gffn, mod0[4:5], mod0[3:4]).astype(BF16)

    def ffn_up(j):
        v[j]["h1"] = _dot(v[j]["f"], w1_ref[...])
        v[j]["h3"] = _dot(v[j]["f"], w3_ref[...])

    def ffn_act(j):
        v[j]["p"] = (_silu(v[j]["h1"]) * v[j]["h3"]).astype(BF16)

    def ffn_down(j):
        x2 = v[j]["x1"] + mod0[5:6] * _dot(v[j]["p"], w2_ref[...])
        xo_ref[rows[j]] = x2
        v[j]["a"] = _rms_mod(x2, gmix1_ref[...], mod1[1:2], mod1[0:1]).astype(BF16)

    def lru_proj(j):
        v[j]["z"] = _dot(v[j]["a"], lwin_ref[...])

    def lru_store(j):
        y_ref[rows[j]] = jax.nn.gelu(v[j]["z"][:, :D]).astype(BF16)
        u_ref[rows[j]] = v[j]["z"][:, D:]

    for stage in (gate, mix_out, ffn_up, ffn_act, ffn_down, lru_proj, lru_store):
        for j in range(SUBTILES):
            stage(j)


def _lru_coeffs(ub, uch, wg_ref, bg_ref, c2, rows):
    zz = _dot(ub, wg_ref[...])
    bg = bg_ref[...]
    ta = jnp.tanh(zz[:, :LRU_BLOCK] + bg[0:1])
    tx = jnp.tanh(zz[:, LRU_BLOCK:] + bg[1:2])
    a = jnp.exp2(c2 * ta + c2)
    v = 1.0 - a * a
    s = v * lax.rsqrt(jnp.maximum(v, TINY))
    b = s * ((tx + 1.0) * uch)
    shape = (rows, SUB, LRU_BLOCK)
    return a.reshape(shape), b.reshape(shape)


def _lru_scan_kernel(u_ref, hp_ref, hn_ref, cw_ref, cb_ref, wg0_ref, wg1_ref,
                     bg0_ref, bg1_ref, lam_ref,
                     s0_ref, pf_ref, pb_ref, sum_ref,
                     e_ref, *, rows, blk_rows, ngroups):
    wg = pl.program_id(1)
    u = u_ref[...]
    prev_ok = wg >= 1
    next_ok = wg <= ngroups - 2
    sub = lax.broadcasted_iota(jnp.int32, (SUB, LRU_BLOCK), 0)
    hp = hp_ref[...]
    hn = hn_ref[...]

    def from_prev_column(cur, halo):
        edge = jnp.where(prev_ok, pltpu.roll(halo, 1, 0), 0.0)
        return jnp.where(sub == 0, edge, pltpu.roll(cur, 1, 0))

    e_ref[CONV_LEFT:CONV_LEFT + rows] = u
    e_ref[0] = from_prev_column(u[rows - 2], hp[0])
    e_ref[1] = from_prev_column(u[rows - 1], hp[1])
    edge = jnp.where(next_ok, pltpu.roll(hn[0], SUB - 1, 0), 0.0)
    e_ref[CONV_LEFT + rows] = jnp.where(sub == SUB - 1, edge, pltpu.roll(u[0], SUB - 1, 0))
    cw = 0.5 * cw_ref[...]
    cb = 0.5 * cb_ref[...]
    lam = lam_ref[...]
    c2 = (-0.5 * LRU_C * LOG2E) * jax.nn.softplus(-lam)
    nblk = rows // blk_rows
    ab = [[None] * nblk, [None] * nblk]
    for k in range(nblk):
        for d, blk in ((0, k), (1, nblk - 1 - k)):
            r0 = blk * blk_rows
            uch = cb + cw[0:1] * e_ref[r0:r0 + blk_rows]
            for j in range(1, CONV_W):
                uch = uch + cw[j:j + 1] * e_ref[r0 + j:r0 + j + blk_rows]
            uch = uch.reshape(blk_rows * SUB, LRU_BLOCK)
            ab[d][blk] = _lru_coeffs(uch.astype(BF16), uch, (wg0_ref, wg1_ref)[d], (bg0_ref, bg1_ref)[d],
                                     c2[d:d + 1], blk_rows)

    h = jnp.zeros((SUB, LRU_BLOCK), F32)
    p = jnp.ones((SUB, LRU_BLOCK), F32)
    for r in range(rows):
        a, b = ab[0][r // blk_rows]
        h = a[r % blk_rows] * h + b[r % blk_rows]
        p = a[r % blk_rows] * p
        s0_ref[r] = h
        pf_ref[r] = p
    sum_ref[0] = p
    sum_ref[1] = h
    h = jnp.zeros((SUB, LRU_BLOCK), F32)
    p = jnp.ones((SUB, LRU_BLOCK), F32)
    for r in range(rows - 1, -1, -1):
        a, b = ab[1][r // blk_rows]
        h = a[r % blk_rows] * h + b[r % blk_rows]
        p = a[r % blk_rows] * p
        s0_ref[r] = s0_ref[r] + h
        pb_ref[r] = p
    sum_ref[2] = p
    sum_ref[3] = h


def _lru_chain_kernel(csum_ref, sum_ref, hf_ref, hb_ref):
    s = jnp.zeros((1, D), F32)
    for w in range(SUB):
        s = csum_ref[0, w:w + 1, :] * s + csum_ref[1, w:w + 1, :]
    for w in range(GRID_W):
        hf_ref[w:w + 1, :] = s
        s = sum_ref[0, w:w + 1, :] * s + sum_ref[1, w:w + 1, :]
    s = jnp.zeros((1, D), F32)
    for w in range(SUB - 1, -1, -1):
        s = csum_ref[2, w:w + 1, :] * s + csum_ref[3, w:w + 1, :]
    for w in range(GRID_W - 1, -1, -1):
        hb_ref[w:w + 1, :] = s
        s = sum_ref[2, w:w + 1, :] * s + sum_ref[3, w:w + 1, :]


def _lru_out_kernel(x_ref, mod_ref, y_ref, s0_ref, pf_ref, pb_ref, hf_ref, hb_ref,
                    wout_ref, gffn_ref, w1_ref, w3_ref, w2_ref, gfin_ref, o_ref):
    mod = mod_ref[...]
    rows = [slice(j * OUT_TM, (j + 1) * OUT_TM) for j in range(OUT_SUBTILES)]
    by_col = (OUT_TM // GRID_W, GRID_W, LRU_BLOCK)
    x1 = []
    for rs in rows:
        yl = None
        for c in range(LRU_BLOCKS):
            cs = slice(c * LRU_BLOCK, (c + 1) * LRU_BLOCK)
            hs = (s0_ref[rs, cs].reshape(by_col) + pf_ref[rs, cs].reshape(by_col) * hf_ref[:, cs]
                  + pb_ref[rs, cs].reshape(by_col) * hb_ref[:, cs]).reshape(OUT_TM, LRU_BLOCK)
            ym = (hs * y_ref[rs, cs].astype(F32)).astype(BF16)
            part = _dot(ym, wout_ref[cs, :])
            yl = part if yl is None else yl + part
        x1.append(x_ref[rs] + mod[2:3] * yl)
    x2 = _ffn_stages(x1, mod, gffn_ref[...], w1_ref, w3_ref, w2_ref)
    for rs, t in zip(rows, x2):
        o_ref[rs] = _rms(t, gfin_ref[...])


def _const(shape, index=None):
    nd = len(shape)
    index = (0,) * nd if index is None else index
    return pl.BlockSpec(shape, lambda *_: index, pipeline_mode=pl.Buffered(1))


def _params(n_axes=1):
    return pltpu.CompilerParams(dimension_semantics=("arbitrary",) * n_axes,
                                vmem_limit_bytes=V7X_VMEM_LIMIT)


def kernel(x, c, ctx, c_ctx, norm_mix_g, norm_ffn_g, w_mod, b_mod, gla_w_in, gla_gate_w1, gla_gate_w2, gla_gate_b, gla_head_norm_g, gla_w_out, lru_w_in, lru_conv_w, lru_conv_b, lru_gate_a_w, lru_gate_a_b, lru_gate_x_w, lru_gate_x_b, lru_lambda, lru_w_out, ffn_w1, ffn_w3, ffn_w2, final_norm_g):
    assert x.shape == (1, SEQ, D) and ctx.shape == (1, CTX, D) and CTX == TM == ROWS
    x2d = x.reshape(SEQ, D)
    ctx2d = ctx.reshape(CTX, D)
    nseq = (NT + SUBTILES) * TM

    m = _modulation(jnp.stack([c[0], c_ctx], axis=1), w_mod, b_mod)
    mods = m[:, :2].reshape(2, 2, 6, D)
    mods = jnp.pad(mods, ((0, 0), (0, 0), (0, SUB - 6), (0, 0)))

    def mod_spec(layer):
        return pl.BlockSpec((None, None, SUB, D), lambda i: (layer, jnp.where(i == CTX_STEP, 1, 0), 0, 0))

    vec = lambda t: t.reshape(1, -1)
    tile = lambda w, f: pl.BlockSpec((TM, w), f)

    def ffn_specs(layer):
        return [_const((1, D)), _const((None, D, FFN), (layer, 0, 0)),
                _const((None, D, FFN), (layer, 0, 0)), _const((None, FFN, D), (layer, 0, 0))]

    gw1 = jnp.concatenate([gla_gate_w1[0, 0], gla_gate_w1[0, 1]], axis=1)
    gw1 = jnp.pad(gw1, ((0, 0), (0, LANES - 2 * RANK))).astype(BF16)
    gw2 = jnp.zeros((LANES, 2 * DK), F32)
    gw2 = gw2.at[:RANK, :DK].set(gla_gate_w2[0, 0]).at[RANK:2 * RANK, DK:].set(gla_gate_w2[0, 1])
    gw2 = gw2.astype(BF16)
    gb = gla_gate_b[0].reshape(1, 2 * DK)

    pair = lambda w: pl.BlockSpec((SUBTILES * TM, w), lambda i: (i, 0))
    lat_pair = pl.BlockSpec((SUBTILES * TM, D), lambda i: (jnp.minimum(i, CTX_STEP - 1), 0))
    q, k, v, g, la = pl.pallas_call(
        _gla_proj_kernel,
        grid=(CTX_STEP + 1,),
        in_specs=[lat_pair, _const((CTX, D)), mod_spec(0), _const((1, D)),
                  _const((D, 2 * DK + 2 * DV)), _const((D, LANES)), _const((LANES, 2 * DK)),
                  _const((1, 2 * DK))],
        out_specs=[pair(DK), pair(DK), pair(DV), pair(DV), pair(2 * DK)],
        out_shape=[jax.ShapeDtypeStruct((nseq, DK), BF16), jax.ShapeDtypeStruct((nseq, DK), BF16),
                   jax.ShapeDtypeStruct((nseq, DV), BF16), jax.ShapeDtypeStruct((nseq, DV), BF16),
                   jax.ShapeDtypeStruct((nseq, 2 * DK), F32)],
        scratch_shapes=[pltpu.VMEM((D, 2 * DK + 2 * DV), BF16)],
        compiler_params=_params(),
        name="gla_proj",
    )(x2d, ctx2d, mods, vec(norm_mix_g[0]), gla_w_in[0], gw1, gw2, gb)

    seq_f = lambda i: (jnp.where(i == 0, NT, i - 1), 0)
    seq_b = lambda i: (jnp.where(i == 0, NT, NT - i), 0)
    streamed = [ffn_w1.reshape(2 * D, FFN), ffn_w3.reshape(2 * D, FFN), ffn_w2.reshape(2 * FFN, D),
                gla_w_out[0], lru_w_in[0], lru_w_out[0]]
    assert len(streamed) == N_STREAMED_WEIGHTS

    def slab_spec(w):
        nblk = NT if w.shape[0] % (NT * BF16_SUBLANES) == 0 else NT // 2
        rows = w.shape[0] // nblk
        assert rows * nblk == w.shape[0] and rows % BF16_SUBLANES == 0
        return pl.BlockSpec((rows, w.shape[1]), lambda i: (jnp.minimum(i, nblk - 1), 0))

    slabs = [slab_spec(w) for w in streamed]
    la_b = lambda i: (seq_b(i)[0], 1)
    o_f, o_b, w1b, w3b, w2b, gwob, lwinb, lwob = pl.pallas_call(
        _gla_scan_kernel,
        grid=(NT + 1,),
        in_specs=[tile(DK, seq_f), tile(DK, seq_f), tile(DV, seq_f), tile(DK, seq_f),
                  tile(DK, seq_b), tile(DK, seq_b), tile(DV, seq_b), tile(DK, la_b)] + slabs,
        out_specs=[tile(DV, seq_f), tile(DV, seq_b)] + slabs,
        out_shape=[jax.ShapeDtypeStruct(((NT + 1) * TM, DV), BF16)] * 2
                  + [jax.ShapeDtypeStruct(w.shape, BF16) for w in streamed],
        scratch_shapes=[pltpu.VMEM((HEADS, HK, HV), F32)] * 2,
        compiler_params=_params(),
        name="gla_scan",
    )(q, k, v, la, q, k, v, la, *streamed)
    w1b, w3b, w2b = w1b.reshape(2, D, FFN), w3b.reshape(2, D, FFN), w2b.reshape(2, FFN, D)

    sub = lambda w, j: pl.BlockSpec((TM, w), lambda i: (jnp.minimum(SUBTILES * i + j, NT), 0))
    x_mid, y_br, u_all = pl.pallas_call(
        _gla_out_kernel,
        grid=(CTX_STEP + 1,),
        in_specs=[lat_pair, _const((CTX, D)), mod_spec(0), mod_spec(1),
                  sub(DV, 0), sub(DV, 1), sub(DV, 0), sub(DV, 1), sub(DV, 0), sub(DV, 1),
                  _const((1, HV)), _const((DV, D))] + ffn_specs(0) + [_const((1, D)), _const((D, 2 * D))],
        out_specs=[pair(D), pair(D), pair(D)],
        out_shape=[jax.ShapeDtypeStruct((nseq, D), F32), jax.ShapeDtypeStruct((nseq, D), BF16),
                   jax.ShapeDtypeStruct((nseq, D), F32)],
        compiler_params=_params(),
        name="gla_out",
    )(x2d, ctx2d, mods, mods, o_f, o_f, o_b, o_b, g, g,
      vec(gla_head_norm_g[0]), gwob, vec(norm_ffn_g[0]), w1b, w3b, w2b,
      vec(norm_mix_g[1]), lwinb)

    u3 = u_all.reshape(nseq // GRID_W, GRID_W, D)
    uctx3 = u_all[NT * TM:(NT + 1) * TM].reshape(SUB, CTX_SEG, D).transpose(1, 0, 2)
    gate_w = lambda d: jnp.concatenate([lru_gate_a_w[0, d], lru_gate_x_w[0, d]], axis=-1).astype(BF16)
    gate_b = lambda d: 0.5 * jnp.stack([lru_gate_a_b[0, d], lru_gate_x_b[0, d]])
    chan = lambda rows: pl.BlockSpec((rows, LRU_BLOCK), lambda cb, wg: (0, cb))
    gate_spec = pl.BlockSpec((None, LRU_BLOCK, 2 * LRU_BLOCK), lambda cb, wg: (cb, 0, 0))
    lru_weights = (lru_conv_w[0], vec(lru_conv_b[0]), gate_w(0), gate_w(1), gate_b(0), gate_b(1),
                   lru_lambda[0])

    def column_scans(u3d, rows, blk_rows, ngroups):
        blk = (rows, SUB, LRU_BLOCK)
        here = pl.BlockSpec(blk, lambda cb, wg: (0, wg, cb))
        return pl.pallas_call(
            functools.partial(_lru_scan_kernel, rows=rows, blk_rows=blk_rows, ngroups=ngroups),
            grid=(LRU_BLOCKS, ngroups),
            in_specs=[here,
                      pl.BlockSpec((2, SUB, LRU_BLOCK), lambda cb, wg: (rows // 2 - 1, jnp.maximum(wg - 1, 0), cb)),
                      pl.BlockSpec((2, SUB, LRU_BLOCK), lambda cb, wg: (0, jnp.minimum(wg + 1, ngroups - 1), cb)),
                      chan(CONV_W), chan(1), gate_spec, gate_spec, chan(2), chan(2), chan(2)],
            out_specs=[here, here, here, pl.BlockSpec((4, SUB, LRU_BLOCK), lambda cb, wg: (0, wg, cb))],
            out_shape=[jax.ShapeDtypeStruct((rows, ngroups * SUB, D), F32)] * 3
                      + [jax.ShapeDtypeStruct((4, ngroups * SUB, D), F32)],
            scratch_shapes=[pltpu.VMEM((rows + CONV_W - 1, SUB, LRU_BLOCK), F32)],
            compiler_params=_params(2),
            name=f"lru_scan_{ngroups * SUB}x{rows}",
        )(u3d, u3d, u3d, *lru_weights)

    ctx_sums = column_scans(uctx3, CTX_SEG, CTX_SEG, 1)[3]
    s0, pf, pb, sums = column_scans(u3, ROWS, SCAN_ROWS, NGRP)

    h_f, h_b = pl.pallas_call(
        _lru_chain_kernel,
        out_shape=[jax.ShapeDtypeStruct((GRID_W, D), F32)] * 2,
        name="lru_chain",
    )(ctx_sums, sums)

    lat = lambda i: (i, 0)
    big = lambda f: pl.BlockSpec((OUT_SUBTILES * OUT_TM, D), f)
    out = pl.pallas_call(
        _lru_out_kernel,
        grid=(SEQ // (OUT_SUBTILES * OUT_TM),),
        in_specs=[big(lat), _const((None, None, SUB, D), (1, 0, 0, 0)),
                  big(lat), big(lat), big(lat), big(lat),
                  _const((GRID_W, D)), _const((GRID_W, D)), _const((D, D))] + ffn_specs(1) + [_const((1, D))],
        out_specs=big(lat),
        out_shape=jax.ShapeDtypeStruct((SEQ, D), F32),
        compiler_params=_params(),
        name="lru_out",
    )(x_mid, mods, y_br, s0.reshape(SEQ, D), pf.reshape(SEQ, D), pb.reshape(SEQ, D),
      h_f, h_b,
      lwob, vec(norm_ffn_g[1]), w1b, w3b, w2b, vec(final_norm_g))
    return out.reshape(1, SEQ, D)
```

```python
import functools

import jax
import jax.numpy as jnp
from jax import lax
from jax.experimental import pallas as pl
from jax.experimental.pallas import tpu as pltpu

D = 1024
SEQ = 16384
GRID_W = 64
ROWS = SEQ // GRID_W
CTX = 256
TM = 256
NT = SEQ // TM
HEADS = 4
DK = 512
DV = 1024
HK = DK // HEADS
HV = DV // HEADS
RANK = 16
TAU = 16.0
CHUNK = 64
NCH = TM // CHUNK
LRU_BLOCKS = 4
LRU_BLOCK = D // LRU_BLOCKS
LRU_C = 8.0
FFN = 2816
EPS = 1e-6
SUB = 8
LANES = 128
BF16_SUBLANES = 16
N_STREAMED_WEIGHTS = 6
CONV_W = 4
CONV_LEFT = 2
NGRP = GRID_W // SUB
SCAN_ROWS = 32
CTX_SEG = CTX // SUB
SUBTILES = 2
CTX_STEP = NT // SUBTILES
OUT_TM = 256
OUT_SUBTILES = 2

F32 = jnp.float32
BF16 = jnp.bfloat16

V7X_VMEM_LIMIT = 56 * 1024 * 1024
LOG2E = 1.4426950408889634
TINY = 1e-30


def _dot(a, b):
    return jnp.dot(a, b, preferred_element_type=F32)


def _dot_nt(a, b):
    return lax.dot_general(a, b, (((1,), (1,)), ((), ())), preferred_element_type=F32)


def _dot_tn(a, b):
    return lax.dot_general(a, b, (((0,), (0,)), ((), ())), preferred_element_type=F32)


def _rms(x, g):
    ms = jnp.mean(x * x, axis=-1, keepdims=True)
    return x * lax.rsqrt(ms + EPS) * g


def _rms_mod(x, g, scale, shift):
    return _rms(x, g * (1.0 + scale)) + shift


def _silu(x):
    return x * jax.nn.sigmoid(x)


def _split_bf16(x):
    hi = x.astype(BF16)
    lo = (x - hi.astype(F32)).astype(BF16)
    return hi, lo


def _mod_kernel(c_ref, w_ref, b_ref, o_ref):
    s = _silu(c_ref[...])
    w = w_ref[...]
    b = b_ref[...]
    o_ref[...] = jnp.zeros_like(o_ref)
    for r in range(2):
        o_ref[r:r + 1, :] = jnp.sum(w * s[:, r:r + 1], axis=0, keepdims=True) + b


def _modulation(ccols, w_mod, b_mod):
    depth = w_mod.shape[0]
    tn = 1536
    return pl.pallas_call(
        _mod_kernel,
        grid=(depth, 6 * D // tn),
        in_specs=[
            pl.BlockSpec((D, 2), lambda l, n: (0, 0)),
            pl.BlockSpec((None, D, tn), lambda l, n: (l, 0, n)),
            pl.BlockSpec((None, 1, tn), lambda l, n: (l, 0, n)),
        ],
        out_specs=pl.BlockSpec((None, SUB, tn), lambda l, n: (l, 0, n)),
        out_shape=jax.ShapeDtypeStruct((depth, SUB, 6 * D), F32),
        compiler_params=pltpu.CompilerParams(
            dimension_semantics=("arbitrary", "arbitrary"),
            vmem_limit_bytes=V7X_VMEM_LIMIT),
        name="mod",
    )(ccols, w_mod, b_mod.reshape(depth, 1, 6 * D))


def _gla_tiles(jobs):
    row = lax.broadcasted_iota(jnp.int32, (TM, TM), 0)
    col = lax.broadcasted_iota(jnp.int32, (TM, TM), 1)
    shift = CHUNK.bit_length() - 1
    same = jnp.right_shift(row, shift) == jnp.right_shift(col, shift)
    prep = []
    for q, k, v, la, s_ref, o_ref, rev in jobs:
        tri = same & ((col >= row) if rev else (col <= row))
        tri_b = tri.astype(BF16)
        la_hi, la_lo = _split_bf16(la)
        b = _dot(tri_b, la_hi) + _dot(tri_b, la_lo)
        prep.append((tri, b))
    work = []
    for (q, k, v, la, s_ref, o_ref, rev), (tri, b) in zip(jobs, prep):
        ends = [c * CHUNK if rev else (c + 1) * CHUNK - 1 for c in range(NCH)]
        bl = jnp.concatenate([jnp.broadcast_to(b[r:r + 1], (CHUNK, DK)) for r in ends], axis=0)
        q_dec = (q * jnp.exp(b)).astype(BF16)
        k_inv = (k * jnp.exp(-b)).astype(BF16)
        k_end = (k * jnp.exp(bl - b)).astype(BF16)
        totals = jnp.concatenate([b[r:r + 1] for r in ends] + [jnp.zeros((SUB - NCH, DK), F32)], axis=0)
        decay_t = jnp.exp(totals).T
        order = list(range(NCH - 1, -1, -1) if rev else range(NCH))
        for h in range(HEADS):
            ks = slice(h * HK, (h + 1) * HK)
            vs = slice(h * HV, (h + 1) * HV)
            work.append(dict(tri=tri, qd=q_dec[:, ks], ki=k_inv[:, ks], ke=k_end[:, ks], vh=v[:, vs],
                             dec=decay_t[ks], order=order, s_ref=s_ref, h=h, o_ref=o_ref, vs=vs))
    for w in work:
        w["att"] = jnp.where(w["tri"], _dot_nt(w["qd"], w["ki"]), 0.0).astype(BF16)
    for w in work:
        w["o_intra"] = _dot(w["att"], w["vh"])
        w["st"] = w["s_ref"][w["h"]]
    for idx in range(NCH):
        for w in work:
            c = w["order"][idx]
            rs = slice(c * CHUNK, (c + 1) * CHUNK)
            o_inter = _dot(w["qd"][rs], w["st"].astype(BF16))
            w["o_ref"][rs, w["vs"]] = (w["o_intra"][rs] + o_inter).astype(w["o_ref"].dtype)
            u = _dot_tn(w["ke"][rs], w["vh"][rs])
            w["st"] = w["dec"][:, c:c + 1] * w["st"] + u
    for w in work:
        w["s_ref"][w["h"]] = w["st"]


def _proj_store(rs, z, zg, hn_row, q_ref, k_ref, v_ref, g_ref, la_ref):
    q_ref[rs] = (z[:, :DK] * (HK ** -0.5)).astype(BF16)
    k_ref[rs] = z[:, DK:2 * DK].astype(BF16)
    v_ref[rs] = z[:, 2 * DK:2 * DK + DV].astype(BF16)
    g_ref[rs] = (_silu(z[:, 2 * DK + DV:]) * hn_row).astype(BF16)
    la_ref[rs] = (jnp.minimum(zg, 0.0) - jnp.log(1.0 + jnp.exp(-jnp.abs(zg)))) * (1.0 / TAU)


def _gla_proj_kernel(x_ref, ctx_ref, mod_ref, gmix_ref, winf_ref, gw1_ref, gw2_ref, gb_ref, hn_ref,
                     q_ref, k_ref, v_ref, g_ref, la_ref, win_ref):
    @pl.when(pl.program_id(0) == 0)
    def _():
        win_ref[...] = winf_ref[...].astype(BF16)

    rows = [slice(j * TM, (j + 1) * TM) for j in range(SUBTILES)]

    def run(read_x):
        mod = mod_ref[...]
        hn_row = hn_ref[...]
        outs = (q_ref, k_ref, v_ref, g_ref, la_ref)
        a = [_rms_mod(read_x(rs), gmix_ref[...], mod[1:2], mod[0:1]).astype(BF16) for rs in rows]
        pending = None
        for rs, at in zip(rows, a):
            z = _dot(at, win_ref[...])
            t = _dot(at, gw1_ref[...]).astype(BF16)
            zg = _dot(t, gw2_ref[...]) + gb_ref[...]
            if pending is not None:
                _proj_store(*pending, hn_row, *outs)
            pending = (rs, z, zg)
        _proj_store(*pending, hn_row, *outs)

    is_ctx = pl.program_id(0) == CTX_STEP
    pl.when(is_ctx)(lambda: run(lambda rs: ctx_ref[...]))
    pl.when(jnp.logical_not(is_ctx))(lambda: run(lambda rs: x_ref[rs]))


def _gla_scan_kernel(qf_ref, kf_ref, vf_ref, laf_ref, qb_ref, kb_ref, vb_ref, lab_ref, *rest):
    nw = N_STREAMED_WEIGHTS
    w_in, (of_ref, ob_ref), w_out, (sf_ref, sb_ref) = (
        rest[:nw], rest[nw:nw + 2], rest[nw + 2:2 * nw + 2], rest[2 * nw + 2:])

    @pl.when(pl.program_id(0) == 0)
    def _():
        sf_ref[...] = jnp.zeros_like(sf_ref)
        sb_ref[...] = jnp.zeros_like(sb_ref)

    _gla_tiles([
        (qf_ref[...].astype(F32), kf_ref[...].astype(F32), vf_ref[...], laf_ref[...], sf_ref, of_ref, False),
        (qb_ref[...].astype(F32), kb_ref[...].astype(F32), vb_ref[...], lab_ref[...], sb_ref, ob_ref, True)])
    for src, dst in zip(w_in, w_out):
        dst[...] = src[...].astype(BF16)


def _ffn_stages(x1, mod, gffn, w1_ref, w3_ref, w2_ref):
    f = [_rms_mod(t, gffn, mod[4:5], mod[3:4]).astype(BF16) for t in x1]
    p = [(_silu(_dot(t, w1_ref[...])) * _dot(t, w3_ref[...])).astype(BF16) for t in f]
    return [t + mod[5:6] * _dot(pt, w2_ref[...]) for t, pt in zip(x1, p)]


def _gla_out_kernel(x_ref, ctx_ref, mod0_ref, mod1_ref, of0_ref, of1_ref, ob0_ref, ob1_ref,
                    g0_ref, g1_ref,
                    wout_ref, gffn_ref, w1_ref, w3_ref, w2_ref, gmix1_ref, lwin_ref,
                    xo_ref, y_ref, u_ref):
    subtiles = ((of0_ref, ob0_ref, g0_ref), (of1_ref, ob1_ref, g1_ref))
    rows = [slice(j * TM, (j + 1) * TM) for j in range(SUBTILES)]
    is_ctx = pl.program_id(0) == CTX_STEP

    @pl.when(is_ctx)
    def _():
        _gla_out_body(lambda j: ctx_ref[...], rows, subtiles, mod0_ref, mod1_ref, wout_ref, gffn_ref,
                      w1_ref, w3_ref, w2_ref, gmix1_ref, lwin_ref, xo_ref, y_ref, u_ref)

    @pl.when(jnp.logical_not(is_ctx))
    def _():
        _gla_out_body(lambda j: x_ref[rows[j]], rows, subtiles, mod0_ref, mod1_ref, wout_ref, gffn_ref,
                      w1_ref, w3_ref, w2_ref, gmix1_ref, lwin_ref, xo_ref, y_ref, u_ref)


def _gla_out_body(read_x, rows, subtiles, mod0_ref, mod1_ref, wout_ref, gffn_ref, w1_ref, w3_ref, w2_ref,
                  gmix1_ref, lwin_ref, xo_ref, y_ref, u_ref):
    mod0 = mod0_ref[...]
    mod1 = mod1_ref[...]
    gffn = gffn_ref[...]
    v = [dict() for _ in subtiles]

    def gate(j):
        of_ref, ob_ref, g_ref = subtiles[j]
        y = None
        for h in range(HEADS):
            hs = slice(h * HV, (h + 1) * HV)
            o = of_ref[:, hs].astype(F32) + ob_ref[:, hs].astype(F32)
            gated = (_rms(o, g_ref[:, hs].astype(F32))).astype(BF16)
            part = _dot(gated, wout_ref[hs, :])
            y = part if y is None else y + part
        v[j]["y"] = y

    def mix_out(j):
        v[j]["x1"] = read_x(j) + mod0[2:3] * v[j]["y"]
        v[j]["f"] = _rms_mod(v[j]["x1"], gffn, mod0[4:5], mod0[3:4]).astype(BF16)

    def ffn_up(j):
        v[j]["h1"] = _dot(v[j]["f"], w1_ref[...])
        v[j]["h3"] = _dot(v[j]["f"], w3_ref[...])

    def ffn_act(j):
        v[j]["p"] = (_silu(v[j]["h1"]) * v[j]["h3"]).astype(BF16)

    def ffn_down(j):
        x2 = v[j]["x1"] + mod0[5:6] * _dot(v[j]["p"], w2_ref[...])
        xo_ref[rows[j]] = x2
        v[j]["a"] = _rms_mod(x2, gmix1_ref[...], mod1[1:2], mod1[0:1]).astype(BF16)

    def lru_proj(j):
        v[j]["z"] = _dot(v[j]["a"], lwin_ref[...])

    def lru_store(j):
        y_ref[rows[j]] = jax.nn.gelu(v[j]["z"][:, :D]).astype(BF16)
        u_ref[rows[j]] = v[j]["z"][:, D:]

    for stage in (gate, mix_out, ffn_up, ffn_act, ffn_down, lru_proj, lru_store):
        for j in range(SUBTILES):
            stage(j)


def _lru_coeffs(ub, uch, wg_ref, bg_ref, c2, rows):
    zz = _dot(ub, wg_ref[...])
    bg = bg_ref[...]
    ta = jnp.tanh(zz[:, :LRU_BLOCK] + bg[0:1])
    tx = jnp.tanh(zz[:, LRU_BLOCK:] + bg[1:2])
    a = jnp.exp2(c2 * ta + c2)
    v = 1.0 - a * a
    s = v * lax.rsqrt(jnp.maximum(v, TINY))
    b = s * ((tx + 1.0) * uch)
    shape = (rows, SUB, LRU_BLOCK)
    return a.reshape(shape), b.reshape(shape)


def _lru_scan_kernel(u_ref, hp_ref, hn_ref, cw_ref, cb_ref, wg0_ref, wg1_ref,
                     bg0_ref, bg1_ref, lam_ref,
                     s0_ref, pf_ref, pb_ref, sum_ref,
                     e_ref, *, rows, blk_rows, ngroups):
    wg = pl.program_id(1)
    u = u_ref[...]
    prev_ok = wg >= 1
    next_ok = wg <= ngroups - 2
    sub = lax.broadcasted_iota(jnp.int32, (SUB, LRU_BLOCK), 0)
    hp = hp_ref[...]
    hn = hn_ref[...]

    def from_prev_column(cur, halo):
        edge = jnp.where(prev_ok, pltpu.roll(halo, 1, 0), 0.0)
        return jnp.where(sub == 0, edge, pltpu.roll(cur, 1, 0))

    e_ref[CONV_LEFT:CONV_LEFT + rows] = u
    e_ref[0] = from_prev_column(u[rows - 2], hp[0])
    e_ref[1] = from_prev_column(u[rows - 1], hp[1])
    edge = jnp.where(next_ok, pltpu.roll(hn[0], SUB - 1, 0), 0.0)
    e_ref[CONV_LEFT + rows] = jnp.where(sub == SUB - 1, edge, pltpu.roll(u[0], SUB - 1, 0))
    cw = 0.5 * cw_ref[...]
    cb = 0.5 * cb_ref[...]
    lam = lam_ref[...]
    c2 = (-0.5 * LRU_C * LOG2E) * jax.nn.softplus(-lam)
    nblk = rows // blk_rows
    ab = [[None] * nblk, [None] * nblk]
    for k in range(nblk):
        for d, blk in ((0, k), (1, nblk - 1 - k)):
            r0 = blk * blk_rows
            uch = cb + cw[0:1] * e_ref[r0:r0 + blk_rows]
            for j in range(1, CONV_W):
                uch = uch + cw[j:j + 1] * e_ref[r0 + j:r0 + j + blk_rows]
            uch = uch.reshape(blk_rows * SUB, LRU_BLOCK)
            ab[d][blk] = _lru_coeffs(uch.astype(BF16), uch, (wg0_ref, wg1_ref)[d], (bg0_ref, bg1_ref)[d],
                                     c2[d:d + 1], blk_rows)

    h = jnp.zeros((SUB, LRU_BLOCK), F32)
    p = jnp.ones((SUB, LRU_BLOCK), F32)
    for r in range(rows):
        a, b = ab[0][r // blk_rows]
        h = a[r % blk_rows] * h + b[r % blk_rows]
        p = a[r % blk_rows] * p
        s0_ref[r] = h
        pf_ref[r] = p
    sum_ref[0] = p
    sum_ref[1] = h
    h = jnp.zeros((SUB, LRU_BLOCK), F32)
    p = jnp.ones((SUB, LRU_BLOCK), F32)
    for r in range(rows - 1, -1, -1):
        a, b = ab[1][r // blk_rows]
        h = a[r % blk_rows] * h + b[r % blk_rows]
        p = a[r % blk_rows] * p
        s0_ref[r] = s0_ref[r] + h
        pb_ref[r] = p
    sum_ref[2] = p
    sum_ref[3] = h


def _lru_chain_kernel(csum_ref, sum_ref, hf_ref, hb_ref):
    s = jnp.zeros((1, D), F32)
    for w in range(SUB):
        s = csum_ref[0, w:w + 1, :] * s + csum_ref[1, w:w + 1, :]
    for w in range(GRID_W):
        hf_ref[w:w + 1, :] = s
        s = sum_ref[0, w:w + 1, :] * s + sum_ref[1, w:w + 1, :]
    s = jnp.zeros((1, D), F32)
    for w in range(SUB - 1, -1, -1):
        s = csum_ref[2, w:w + 1, :] * s + csum_ref[3, w:w + 1, :]
    for w in range(GRID_W - 1, -1, -1):
        hb_ref[w:w + 1, :] = s
        s = sum_ref[2, w:w + 1, :] * s + sum_ref[3, w:w + 1, :]


def _lru_out_kernel(x_ref, mod_ref, y_ref, s0_ref, pf_ref, pb_ref, hf_ref, hb_ref,
                    wout_ref, gffn_ref, w1_ref, w3_ref, w2_ref, gfin_ref, o_ref):
    mod = mod_ref[...]
    rows = [slice(j * OUT_TM, (j + 1) * OUT_TM) for j in range(OUT_SUBTILES)]
    by_col = (OUT_TM // GRID_W, GRID_W, LRU_BLOCK)
    x1 = []
    for rs in rows:
        yl = None
        for c in range(LRU_BLOCKS):
            cs = slice(c * LRU_BLOCK, (c + 1) * LRU_BLOCK)
            hs = (s0_ref[rs, cs].reshape(by_col) + pf_ref[rs, cs].reshape(by_col) * hf_ref[:, cs]
                  + pb_ref[rs, cs].reshape(by_col) * hb_ref[:, cs]).reshape(OUT_TM, LRU_BLOCK)
            ym = (hs * y_ref[rs, cs].astype(F32)).astype(BF16)
            part = _dot(ym, wout_ref[cs, :])
            yl = part if yl is None else yl + part
        x1.append(x_ref[rs] + mod[2:3] * yl)
    x2 = _ffn_stages(x1, mod, gffn_ref[...], w1_ref, w3_ref, w2_ref)
    for rs, t in zip(rows, x2):
        o_ref[rs] = _rms(t, gfin_ref[...])


def _const(shape, index=None):
    nd = len(shape)
    index = (0,) * nd if index is None else index
    return pl.BlockSpec(shape, lambda *_: index, pipeline_mode=pl.Buffered(1))


def _params(n_axes=1):
    return pltpu.CompilerParams(dimension_semantics=("arbitrary",) * n_axes,
                                vmem_limit_bytes=V7X_VMEM_LIMIT)


def kernel(x, c, ctx, c_ctx, norm_mix_g, norm_ffn_g, w_mod, b_mod, gla_w_in, gla_gate_w1, gla_gate_w2, gla_gate_b, gla_head_norm_g, gla_w_out, lru_w_in, lru_conv_w, lru_conv_b, lru_gate_a_w, lru_gate_a_b, lru_gate_x_w, lru_gate_x_b, lru_lambda, lru_w_out, ffn_w1, ffn_w3, ffn_w2, final_norm_g):
    assert x.shape == (1, SEQ, D) and ctx.shape == (1, CTX, D) and CTX == TM == ROWS
    x2d = x.reshape(SEQ, D)
    ctx2d = ctx.reshape(CTX, D)
    nseq = (NT + SUBTILES) * TM

    m = _modulation(jnp.stack([c[0], c_ctx], axis=1), w_mod, b_mod)
    mods = m[:, :2].reshape(2, 2, 6, D)
    mods = jnp.pad(mods, ((0, 0), (0, 0), (0, SUB - 6), (0, 0)))

    def mod_spec(layer):
        return pl.BlockSpec((None, None, SUB, D), lambda i: (layer, jnp.where(i == CTX_STEP, 1, 0), 0, 0))

    vec = lambda t: t.reshape(1, -1)
    tile = lambda w, f: pl.BlockSpec((TM, w), f)

    def ffn_specs(layer):
        return [_const((1, D)), _const((None, D, FFN), (layer, 0, 0)),
                _const((None, D, FFN), (layer, 0, 0)), _const((None, FFN, D), (layer, 0, 0))]

    gw1 = jnp.concatenate([gla_gate_w1[0, 0], gla_gate_w1[0, 1]], axis=1)
    gw1 = jnp.pad(gw1, ((0, 0), (0, LANES - 2 * RANK))).astype(BF16)
    gw2 = jnp.zeros((LANES, 2 * DK), F32)
    gw2 = gw2.at[:RANK, :DK].set(gla_gate_w2[0, 0]).at[RANK:2 * RANK, DK:].set(gla_gate_w2[0, 1])
    gw2 = gw2.astype(BF16)
    gb = gla_gate_b[0].reshape(1, 2 * DK)

    pair = lambda w: pl.BlockSpec((SUBTILES * TM, w), lambda i: (i, 0))
    lat_pair = pl.BlockSpec((SUBTILES * TM, D), lambda i: (jnp.minimum(i, CTX_STEP - 1), 0))
    q, k, v, g, la = pl.pallas_call(
        _gla_proj_kernel,
        grid=(CTX_STEP + 1,),
        in_specs=[lat_pair, _const((CTX, D)), mod_spec(0), _const((1, D)),
                  _const((D, 2 * DK + 2 * DV)), _const((D, LANES)), _const((LANES, 2 * DK)),
                  _const((1, 2 * DK)), _const((1, DV))],
        out_specs=[pair(DK), pair(DK), pair(DV), pair(DV), pair(2 * DK)],
        out_shape=[jax.ShapeDtypeStruct((nseq, DK), BF16), jax.ShapeDtypeStruct((nseq, DK), BF16),
                   jax.ShapeDtypeStruct((nseq, DV), BF16), jax.ShapeDtypeStruct((nseq, DV), BF16),
                   jax.ShapeDtypeStruct((nseq, 2 * DK), F32)],
        scratch_shapes=[pltpu.VMEM((D, 2 * DK + 2 * DV), BF16)],
        compiler_params=_params(),
        name="gla_proj",
    )(x2d, ctx2d, mods, vec(norm_mix_g[0]), gla_w_in[0], gw1, gw2, gb,
      jnp.tile(vec(gla_head_norm_g[0]), (1, HEADS)))

    seq_f = lambda i: (jnp.where(i == 0, NT, i - 1), 0)
    seq_b = lambda i: (jnp.where(i == 0, NT, NT - i), 0)
    streamed = [ffn_w1.reshape(2 * D, FFN), ffn_w3.reshape(2 * D, FFN), ffn_w2.reshape(2 * FFN, D),
                gla_w_out[0], lru_w_in[0], lru_w_out[0]]
    assert len(streamed) == N_STREAMED_WEIGHTS

    def slab_spec(w):
        nblk = NT if w.shape[0] % (NT * BF16_SUBLANES) == 0 else NT // 2
        rows = w.shape[0] // nblk
        assert rows * nblk == w.shape[0] and rows % BF16_SUBLANES == 0
        return pl.BlockSpec((rows, w.shape[1]), lambda i: (jnp.minimum(i, nblk - 1), 0))

    slabs = [slab_spec(w) for w in streamed]
    la_b = lambda i: (seq_b(i)[0], 1)
    o_f, o_b, w1b, w3b, w2b, gwob, lwinb, lwob = pl.pallas_call(
        _gla_scan_kernel,
        grid=(NT + 1,),
        in_specs=[tile(DK, seq_f), tile(DK, seq_f), tile(DV, seq_f), tile(DK, seq_f),
                  tile(DK, seq_b), tile(DK, seq_b), tile(DV, seq_b), tile(DK, la_b)] + slabs,
        out_specs=[tile(DV, seq_f), tile(DV, seq_b)] + slabs,
        out_shape=[jax.ShapeDtypeStruct(((NT + 1) * TM, DV), BF16)] * 2
                  + [jax.ShapeDtypeStruct(w.shape, BF16) for w in streamed],
        scratch_shapes=[pltpu.VMEM((HEADS, HK, HV), F32)] * 2,
        compiler_params=_params(),
        name="gla_scan",
    )(q, k, v, la, q, k, v, la, *streamed)
    w1b, w3b, w2b = w1b.reshape(2, D, FFN), w3b.reshape(2, D, FFN), w2b.reshape(2, FFN, D)

    sub = lambda w, j: pl.BlockSpec((TM, w), lambda i: (jnp.minimum(SUBTILES * i + j, NT), 0))
    x_mid, y_br, u_all = pl.pallas_call(
        _gla_out_kernel,
        grid=(CTX_STEP + 1,),
        in_specs=[lat_pair, _const((CTX, D)), mod_spec(0), mod_spec(1),
                  sub(DV, 0), sub(DV, 1), sub(DV, 0), sub(DV, 1), sub(DV, 0), sub(DV, 1),
                  _const((DV, D))] + ffn_specs(0) + [_const((1, D)), _const((D, 2 * D))],
        out_specs=[pair(D), pair(D), pair(D)],
        out_shape=[jax.ShapeDtypeStruct((nseq, D), F32), jax.ShapeDtypeStruct((nseq, D), BF16),
                   jax.ShapeDtypeStruct((nseq, D), F32)],
        compiler_params=_params(),
        name="gla_out",
    )(x2d, ctx2d, mods, mods, o_f, o_f, o_b, o_b, g, g,
      gwob, vec(norm_ffn_g[0]), w1b, w3b, w2b,
      vec(norm_mix_g[1]), lwinb)

    u3 = u_all.reshape(nseq // GRID_W, GRID_W, D)
    uctx3 = u_all[NT * TM:(NT + 1) * TM].reshape(SUB, CTX_SEG, D).transpose(1, 0, 2)
    gate_w = lambda d: jnp.concatenate([lru_gate_a_w[0, d], lru_gate_x_w[0, d]], axis=-1).astype(BF16)
    gate_b = lambda d: 0.5 * jnp.stack([lru_gate_a_b[0, d], lru_gate_x_b[0, d]])
    chan = lambda rows: pl.BlockSpec((rows, LRU_BLOCK), lambda cb, wg: (0, cb))
    gate_spec = pl.BlockSpec((None, LRU_BLOCK, 2 * LRU_BLOCK), lambda cb, wg: (cb, 0, 0))
    lru_weights = (lru_conv_w[0], vec(lru_conv_b[0]), gate_w(0), gate_w(1), gate_b(0), gate_b(1),
                   lru_lambda[0])

    def column_scans(u3d, rows, blk_rows, ngroups):
        blk = (rows, SUB, LRU_BLOCK)
        here = pl.BlockSpec(blk, lambda cb, wg: (0, wg, cb))
        return pl.pallas_call(
            functools.partial(_lru_scan_kernel, rows=rows, blk_rows=blk_rows, ngroups=ngroups),
            grid=(LRU_BLOCKS, ngroups),
            in_specs=[here,
                      pl.BlockSpec((2, SUB, LRU_BLOCK), lambda cb, wg: (rows // 2 - 1, jnp.maximum(wg - 1, 0), cb)),
                      pl.BlockSpec((2, SUB, LRU_BLOCK), lambda cb, wg: (0, jnp.minimum(wg + 1, ngroups - 1), cb)),
                      chan(CONV_W), chan(1), gate_spec, gate_spec, chan(2), chan(2), chan(2)],
            out_specs=[here, here, here, pl.BlockSpec((4, SUB, LRU_BLOCK), lambda cb, wg: (0, wg, cb))],
            out_shape=[jax.ShapeDtypeStruct((rows, ngroups * SUB, D), F32)] * 3
                      + [jax.ShapeDtypeStruct((4, ngroups * SUB, D), F32)],
            scratch_shapes=[pltpu.VMEM((rows + CONV_W - 1, SUB, LRU_BLOCK), F32)],
            compiler_params=_params(2),
            name=f"lru_scan_{ngroups * SUB}x{rows}",
        )(u3d, u3d, u3d, *lru_weights)

    ctx_sums = column_scans(uctx3, CTX_SEG, CTX_SEG, 1)[3]
    s0, pf, pb, sums = column_scans(u3, ROWS, SCAN_ROWS, NGRP)

    h_f, h_b = pl.pallas_call(
        _lru_chain_kernel,
        out_shape=[jax.ShapeDtypeStruct((GRID_W, D), F32)] * 2,
        name="lru_chain",
    )(ctx_sums, sums)

    lat = lambda i: (i, 0)
    big = lambda f: pl.BlockSpec((OUT_SUBTILES * OUT_TM, D), f)
    out = pl.pallas_call(
        _lru_out_kernel,
        grid=(SEQ // (OUT_SUBTILES * OUT_TM),),
        in_specs=[big(lat), _const((None, None, SUB, D), (1, 0, 0, 0)),
                  big(lat), big(lat), big(lat), big(lat),
                  _const((GRID_W, D)), _const((GRID_W, D)), _const((D, D))] + ffn_specs(1) + [_const((1, D))],
        out_specs=big(lat),
        out_shape=jax.ShapeDtypeStruct((SEQ, D), F32),
        compiler_params=_params(),
        name="lru_out",
    )(x_mid, mods, y_br, s0.reshape(SEQ, D), pf.reshape(SEQ, D), pb.reshape(SEQ, D),
      h_f, h_b,
      lwob, vec(norm_ffn_g[1]), w1b, w3b, w2b, vec(final_norm_g))
    return out.reshape(1, SEQ, D)
```

```python
import functools

import jax
import jax.numpy as jnp
from jax import lax
from jax.experimental import pallas as pl
from jax.experimental.pallas import tpu as pltpu

D = 1024
SEQ = 16384
GRID_W = 64
ROWS = SEQ // GRID_W
CTX = 256
TM = 256
NT = SEQ // TM
HEADS = 4
DK = 512
DV = 1024
HK = DK // HEADS
HV = DV // HEADS
RANK = 16
TAU = 16.0
CHUNK = 64
NCH = TM // CHUNK
LRU_BLOCKS = 4
LRU_BLOCK = D // LRU_BLOCKS
LRU_C = 8.0
FFN = 2816
EPS = 1e-6
SUB = 8
LANES = 128
BF16_SUBLANES = 16
N_STREAMED_WEIGHTS = 6
CONV_W = 4
CONV_LEFT = 2
NGRP = GRID_W // SUB
SCAN_ROWS = 32
CTX_SEG = CTX // SUB
SUBTILES = 2
CTX_STEP = NT // SUBTILES
OUT_TM = 256
OUT_SUBTILES = 2

F32 = jnp.float32
BF16 = jnp.bfloat16

V7X_VMEM_LIMIT = 56 * 1024 * 1024
LOG2E = 1.4426950408889634
TINY = 1e-30


def _dot(a, b):
    return jnp.dot(a, b, preferred_element_type=F32)


def _dot_nt(a, b):
    return lax.dot_general(a, b, (((1,), (1,)), ((), ())), preferred_element_type=F32)


def _dot_tn(a, b):
    return lax.dot_general(a, b, (((0,), (0,)), ((), ())), preferred_element_type=F32)


def _rms(x, g):
    ms = jnp.mean(x * x, axis=-1, keepdims=True)
    return x * lax.rsqrt(ms + EPS) * g


def _rms_mod(x, g, scale, shift):
    return _rms(x, g * (1.0 + scale)) + shift


def _silu(x):
    return x * jax.nn.sigmoid(x)


def _split_bf16(x):
    hi = x.astype(BF16)
    lo = (x - hi.astype(F32)).astype(BF16)
    return hi, lo


def _mod_kernel(c_ref, w_ref, b_ref, o_ref):
    s = _silu(c_ref[...])
    w = w_ref[...]
    b = b_ref[...]
    o_ref[...] = jnp.zeros_like(o_ref)
    for r in range(2):
        o_ref[r:r + 1, :] = jnp.sum(w * s[:, r:r + 1], axis=0, keepdims=True) + b


def _modulation(ccols, w_mod, b_mod):
    depth = w_mod.shape[0]
    tn = 1536
    return pl.pallas_call(
        _mod_kernel,
        grid=(depth, 6 * D // tn),
        in_specs=[
            pl.BlockSpec((D, 2), lambda l, n: (0, 0)),
            pl.BlockSpec((None, D, tn), lambda l, n: (l, 0, n)),
            pl.BlockSpec((None, 1, tn), lambda l, n: (l, 0, n)),
        ],
        out_specs=pl.BlockSpec((None, SUB, tn), lambda l, n: (l, 0, n)),
        out_shape=jax.ShapeDtypeStruct((depth, SUB, 6 * D), F32),
        compiler_params=pltpu.CompilerParams(
            dimension_semantics=("arbitrary", "arbitrary"),
            vmem_limit_bytes=V7X_VMEM_LIMIT),
        name="mod",
    )(ccols, w_mod, b_mod.reshape(depth, 1, 6 * D))


def _gla_tiles(jobs):
    row = lax.broadcasted_iota(jnp.int32, (TM, TM), 0)
    col = lax.broadcasted_iota(jnp.int32, (TM, TM), 1)
    shift = CHUNK.bit_length() - 1
    same = jnp.right_shift(row, shift) == jnp.right_shift(col, shift)
    prep = []
    for q, k, v, la, s_ref, o_ref, rev in jobs:
        tri = same & ((col >= row) if rev else (col <= row))
        tri_b = tri.astype(BF16)
        la_hi, la_lo = _split_bf16(la)
        b = _dot(tri_b, la_hi) + _dot(tri_b, la_lo)
        prep.append((tri, b))
    work = []
    for (q, k, v, la, s_ref, o_ref, rev), (tri, b) in zip(jobs, prep):
        ends = [c * CHUNK if rev else (c + 1) * CHUNK - 1 for c in range(NCH)]
        bl = jnp.concatenate([jnp.broadcast_to(b[r:r + 1], (CHUNK, DK)) for r in ends], axis=0)
        q_dec = (q * jnp.exp(b)).astype(BF16)
        k_inv = (k * jnp.exp(-b)).astype(BF16)
        k_end = (k * jnp.exp(bl - b)).astype(BF16)
        totals = jnp.concatenate([b[r:r + 1] for r in ends] + [jnp.zeros((SUB - NCH, DK), F32)], axis=0)
        decay_t = jnp.exp(totals).T
        order = list(range(NCH - 1, -1, -1) if rev else range(NCH))
        for h in range(HEADS):
            ks = slice(h * HK, (h + 1) * HK)
            vs = slice(h * HV, (h + 1) * HV)
            work.append(dict(tri=tri, qd=q_dec[:, ks], ki=k_inv[:, ks], ke=k_end[:, ks], vh=v[:, vs],
                             dec=decay_t[ks], order=order, s_ref=s_ref, h=h, o_ref=o_ref, vs=vs))
    for w in work:
        w["att"] = jnp.where(w["tri"], _dot_nt(w["qd"], w["ki"]), 0.0).astype(BF16)
    for w in work:
        w["o_intra"] = _dot(w["att"], w["vh"])
        w["st"] = w["s_ref"][w["h"]]
    for idx in range(NCH):
        for w in work:
            c = w["order"][idx]
            rs = slice(c * CHUNK, (c + 1) * CHUNK)
            o_inter = _dot(w["qd"][rs], w["st"].astype(BF16))
            w["o_ref"][rs, w["vs"]] = (w["o_intra"][rs] + o_inter).astype(w["o_ref"].dtype)
            u = _dot_tn(w["ke"][rs], w["vh"][rs])
            w["st"] = w["dec"][:, c:c + 1] * w["st"] + u
    for w in work:
        w["s_ref"][w["h"]] = w["st"]


def _proj_store(rs, z, zg, hn_row, q_ref, k_ref, v_ref, g_ref, la_ref):
    q_ref[rs] = (z[:, :DK] * (HK ** -0.5)).astype(BF16)
    k_ref[rs] = z[:, DK:2 * DK].astype(BF16)
    v_ref[rs] = z[:, 2 * DK:2 * DK + DV].astype(BF16)
    g_ref[rs] = (_silu(z[:, 2 * DK + DV:]) * hn_row).astype(BF16)
    la_ref[rs] = (jnp.minimum(zg, 0.0) - jnp.log(1.0 + jnp.exp(-jnp.abs(zg)))) * (1.0 / TAU)


def _gla_proj_kernel(x_ref, ctx_ref, mod_ref, gmix_ref, winf_ref, gw1_ref, gw2_ref, gb_ref, hn_ref,
                     q_ref, k_ref, v_ref, g_ref, la_ref, win_ref):
    @pl.when(pl.program_id(0) == 0)
    def _():
        win_ref[...] = winf_ref[...].astype(BF16)

    rows = [slice(j * TM, (j + 1) * TM) for j in range(SUBTILES)]

    def run(read_x):
        mod = mod_ref[...]
        hn_row = hn_ref[...]
        outs = (q_ref, k_ref, v_ref, g_ref, la_ref)
        a = [_rms_mod(read_x(rs), gmix_ref[...], mod[1:2], mod[0:1]).astype(BF16) for rs in rows]
        pending = None
        for rs, at in zip(rows, a):
            z = _dot(at, win_ref[...])
            t = _dot(at, gw1_ref[...]).astype(BF16)
            zg = _dot(t, gw2_ref[...]) + gb_ref[...]
            if pending is not None:
                _proj_store(*pending, hn_row, *outs)
            pending = (rs, z, zg)
        _proj_store(*pending, hn_row, *outs)

    is_ctx = pl.program_id(0) == CTX_STEP
    pl.when(is_ctx)(lambda: run(lambda rs: ctx_ref[...]))
    pl.when(jnp.logical_not(is_ctx))(lambda: run(lambda rs: x_ref[rs]))


def _gla_scan_kernel(qf_ref, kf_ref, vf_ref, laf_ref, qb_ref, kb_ref, vb_ref, lab_ref, *rest):
    nw = N_STREAMED_WEIGHTS
    w_in, (of_ref, ob_ref), w_out, (sf_ref, sb_ref) = (
        rest[:nw], rest[nw:nw + 2], rest[nw + 2:2 * nw + 2], rest[2 * nw + 2:])

    @pl.when(pl.program_id(0) == 0)
    def _():
        sf_ref[...] = jnp.zeros_like(sf_ref)
        sb_ref[...] = jnp.zeros_like(sb_ref)

    _gla_tiles([
        (qf_ref[...].astype(F32), kf_ref[...].astype(F32), vf_ref[...], laf_ref[...], sf_ref, of_ref, False),
        (qb_ref[...].astype(F32), kb_ref[...].astype(F32), vb_ref[...], lab_ref[...], sb_ref, ob_ref, True)])
    for src, dst in zip(w_in, w_out):
        dst[...] = src[...].astype(BF16)


def _ffn_stages(x1, mod, gffn, w1_ref, w3_ref, w2_ref):
    f = [_rms_mod(t, gffn, mod[4:5], mod[3:4]).astype(BF16) for t in x1]
    p = [(_silu(_dot(t, w1_ref[...])) * _dot(t, w3_ref[...])).astype(BF16) for t in f]
    return [t + mod[5:6] * _dot(pt, w2_ref[...]) for t, pt in zip(x1, p)]


def _gla_out_kernel(x_ref, ctx_ref, mod0_ref, mod1_ref, of0_ref, of1_ref, ob0_ref, ob1_ref,
                    g0_ref, g1_ref,
                    wout_ref, gffn_ref, w1_ref, w3_ref, w2_ref, gmix1_ref, lwin_ref,
                    xo_ref, y_ref, u_ref):
    subtiles = ((of0_ref, ob0_ref, g0_ref), (of1_ref, ob1_ref, g1_ref))
    rows = [slice(j * TM, (j + 1) * TM) for j in range(SUBTILES)]
    is_ctx = pl.program_id(0) == CTX_STEP
    read_x = lambda j: jnp.where(is_ctx, ctx_ref[...], x_ref[rows[j]])
    mod0 = mod0_ref[...]
    mod1 = mod1_ref[...]
    gffn = gffn_ref[...]
    v = [dict() for _ in subtiles]

    def gate(j):
        of_ref, ob_ref, g_ref = subtiles[j]
        y = None
        for h in range(HEADS):
            hs = slice(h * HV, (h + 1) * HV)
            o = of_ref[:, hs].astype(F32) + ob_ref[:, hs].astype(F32)
            gated = (_rms(o, g_ref[:, hs].astype(F32))).astype(BF16)
            part = _dot(gated, wout_ref[hs, :])
            y = part if y is None else y + part
        v[j]["y"] = y

    def mix_out(j):
        v[j]["x1"] = read_x(j) + mod0[2:3] * v[j]["y"]
        v[j]["f"] = _rms_mod(v[j]["x1"], gffn, mod0[4:5], mod0[3:4]).astype(BF16)

    def ffn_up(j):
        v[j]["h1"] = _dot(v[j]["f"], w1_ref[...])
        v[j]["h3"] = _dot(v[j]["f"], w3_ref[...])

    def ffn_act(j):
        v[j]["p"] = (_silu(v[j]["h1"]) * v[j]["h3"]).astype(BF16)

    def ffn_down(j):
        x2 = v[j]["x1"] + mod0[5:6] * _dot(v[j]["p"], w2_ref[...])
        xo_ref[rows[j]] = x2
        v[j]["a"] = _rms_mod(x2, gmix1_ref[...], mod1[1:2], mod1[0:1]).astype(BF16)

    def lru_proj(j):
        v[j]["z"] = _dot(v[j]["a"], lwin_ref[...])

    def lru_store(j):
        y_ref[rows[j]] = jax.nn.gelu(v[j]["z"][:, :D]).astype(BF16)
        u_ref[rows[j]] = v[j]["z"][:, D:]

    for stage in (gate, mix_out, ffn_up, ffn_act, ffn_down, lru_proj, lru_store):
        for j in range(SUBTILES):
            stage(j)


def _lru_coeffs(ub, uch, wg_ref, bg_ref, c2, rows):
    zz = _dot(ub, wg_ref[...])
    bg = bg_ref[...]
    ta = jnp.tanh(zz[:, :LRU_BLOCK] + bg[0:1])
    tx = jnp.tanh(zz[:, LRU_BLOCK:] + bg[1:2])
    a = jnp.exp2(c2 * ta + c2)
    v = 1.0 - a * a
    s = v * lax.rsqrt(jnp.maximum(v, TINY))
    b = s * ((tx + 1.0) * uch)
    shape = (rows, SUB, LRU_BLOCK)
    return a.reshape(shape), b.reshape(shape)


def _lru_scan_kernel(u_ref, hp_ref, hn_ref, cw_ref, cb_ref, wg0_ref, wg1_ref,
                     bg0_ref, bg1_ref, lam_ref,
                     s0_ref, pf_ref, pb_ref, sum_ref,
                     e_ref, *, rows, blk_rows, ngroups):
    wg = pl.program_id(1)
    u = u_ref[...]
    prev_ok = wg >= 1
    next_ok = wg <= ngroups - 2
    sub = lax.broadcasted_iota(jnp.int32, (SUB, LRU_BLOCK), 0)
    hp = hp_ref[...]
    hn = hn_ref[...]

    def from_prev_column(cur, halo):
        edge = jnp.where(prev_ok, pltpu.roll(halo, 1, 0), 0.0)
        return jnp.where(sub == 0, edge, pltpu.roll(cur, 1, 0))

    e_ref[CONV_LEFT:CONV_LEFT + rows] = u
    e_ref[0] = from_prev_column(u[rows - 2], hp[0])
    e_ref[1] = from_prev_column(u[rows - 1], hp[1])
    edge = jnp.where(next_ok, pltpu.roll(hn[0], SUB - 1, 0), 0.0)
    e_ref[CONV_LEFT + rows] = jnp.where(sub == SUB - 1, edge, pltpu.roll(u[0], SUB - 1, 0))
    cw = 0.5 * cw_ref[...]
    cb = 0.5 * cb_ref[...]
    lam = lam_ref[...]
    c2 = (-0.5 * LRU_C * LOG2E) * jax.nn.softplus(-lam)
    nblk = rows // blk_rows
    ab = [[None] * nblk, [None] * nblk]
    for k in range(nblk):
        for d, blk in ((0, k), (1, nblk - 1 - k)):
            r0 = blk * blk_rows
            uch = cb + cw[0:1] * e_ref[r0:r0 + blk_rows]
            for j in range(1, CONV_W):
                uch = uch + cw[j:j + 1] * e_ref[r0 + j:r0 + j + blk_rows]
            uch = uch.reshape(blk_rows * SUB, LRU_BLOCK)
            ab[d][blk] = _lru_coeffs(uch.astype(BF16), uch, (wg0_ref, wg1_ref)[d], (bg0_ref, bg1_ref)[d],
                                     c2[d:d + 1], blk_rows)

    h = jnp.zeros((SUB, LRU_BLOCK), F32)
    p = jnp.ones((SUB, LRU_BLOCK), F32)
    for r in range(rows):
        a, b = ab[0][r // blk_rows]
        h = a[r % blk_rows] * h + b[r % blk_rows]
        p = a[r % blk_rows] * p
        s0_ref[r] = h
        pf_ref[r] = p
    sum_ref[0] = p
    sum_ref[1] = h
    h = jnp.zeros((SUB, LRU_BLOCK), F32)
    p = jnp.ones((SUB, LRU_BLOCK), F32)
    for r in range(rows - 1, -1, -1):
        a, b = ab[1][r // blk_rows]
        h = a[r % blk_rows] * h + b[r % blk_rows]
        p = a[r % blk_rows] * p
        s0_ref[r] = s0_ref[r] + h
        pb_ref[r] = p
    sum_ref[2] = p
    sum_ref[3] = h


def _lru_chain_kernel(csum_ref, sum_ref, hf_ref, hb_ref):
    s = jnp.zeros((1, D), F32)
    for w in range(SUB):
        s = csum_ref[0, w:w + 1, :] * s + csum_ref[1, w:w + 1, :]
    for w in range(GRID_W):
        hf_ref[w:w + 1, :] = s
        s = sum_ref[0, w:w + 1, :] * s + sum_ref[1, w:w + 1, :]
    s = jnp.zeros((1, D), F32)
    for w in range(SUB - 1, -1, -1):
        s = csum_ref[2, w:w + 1, :] * s + csum_ref[3, w:w + 1, :]
    for w in range(GRID_W - 1, -1, -1):
        hb_ref[w:w + 1, :] = s
        s = sum_ref[2, w:w + 1, :] * s + sum_ref[3, w:w + 1, :]


def _lru_out_kernel(x_ref, mod_ref, y_ref, s0_ref, pf_ref, pb_ref, hf_ref, hb_ref,
                    wout_ref, gffn_ref, w1_ref, w3_ref, w2_ref, gfin_ref, o_ref):
    mod = mod_ref[...]
    rows = [slice(j * OUT_TM, (j + 1) * OUT_TM) for j in range(OUT_SUBTILES)]
    by_col = (OUT_TM // GRID_W, GRID_W, LRU_BLOCK)
    x1 = []
    for rs in rows:
        yl = None
        for c in range(LRU_BLOCKS):
            cs = slice(c * LRU_BLOCK, (c + 1) * LRU_BLOCK)
            hs = (s0_ref[rs, cs].reshape(by_col) + pf_ref[rs, cs].reshape(by_col) * hf_ref[:, cs]
                  + pb_ref[rs, cs].reshape(by_col) * hb_ref[:, cs]).reshape(OUT_TM, LRU_BLOCK)
            ym = (hs * y_ref[rs, cs].astype(F32)).astype(BF16)
            part = _dot(ym, wout_ref[cs, :])
            yl = part if yl is None else yl + part
        x1.append(x_ref[rs] + mod[2:3] * yl)
    x2 = _ffn_stages(x1, mod, gffn_ref[...], w1_ref, w3_ref, w2_ref)
    for rs, t in zip(rows, x2):
        o_ref[rs] = _rms(t, gfin_ref[...])


def _const(shape, index=None):
    nd = len(shape)
    index = (0,) * nd if index is None else index
    return pl.BlockSpec(shape, lambda *_: index, pipeline_mode=pl.Buffered(1))


def _params(n_axes=1):
    return pltpu.CompilerParams(dimension_semantics=("arbitrary",) * n_axes,
                                vmem_limit_bytes=V7X_VMEM_LIMIT)


def kernel(x, c, ctx, c_ctx, norm_mix_g, norm_ffn_g, w_mod, b_mod, gla_w_in, gla_gate_w1, gla_gate_w2, gla_gate_b, gla_head_norm_g, gla_w_out, lru_w_in, lru_conv_w, lru_conv_b, lru_gate_a_w, lru_gate_a_b, lru_gate_x_w, lru_gate_x_b, lru_lambda, lru_w_out, ffn_w1, ffn_w3, ffn_w2, final_norm_g):
    assert x.shape == (1, SEQ, D) and ctx.shape == (1, CTX, D) and CTX == TM == ROWS
    x2d = x.reshape(SEQ, D)
    ctx2d = ctx.reshape(CTX, D)
    nseq = (NT + SUBTILES) * TM

    m = _modulation(jnp.stack([c[0], c_ctx], axis=1), w_mod, b_mod)
    mods = m[:, :2].reshape(2, 2, 6, D)
    mods = jnp.pad(mods, ((0, 0), (0, 0), (0, SUB - 6), (0, 0)))

    def mod_spec(layer):
        return pl.BlockSpec((None, None, SUB, D), lambda i: (layer, jnp.where(i == CTX_STEP, 1, 0), 0, 0))

    vec = lambda t: t.reshape(1, -1)
    tile = lambda w, f: pl.BlockSpec((TM, w), f)

    def ffn_specs(layer):
        return [_const((1, D)), _const((None, D, FFN), (layer, 0, 0)),
                _const((None, D, FFN), (layer, 0, 0)), _const((None, FFN, D), (layer, 0, 0))]

    gw1 = jnp.concatenate([gla_gate_w1[0, 0], gla_gate_w1[0, 1]], axis=1)
    gw1 = jnp.pad(gw1, ((0, 0), (0, LANES - 2 * RANK))).astype(BF16)
    gw2 = jnp.zeros((LANES, 2 * DK), F32)
    gw2 = gw2.at[:RANK, :DK].set(gla_gate_w2[0, 0]).at[RANK:2 * RANK, DK:].set(gla_gate_w2[0, 1])
    gw2 = gw2.astype(BF16)
    gb = gla_gate_b[0].reshape(1, 2 * DK)

    pair = lambda w: pl.BlockSpec((SUBTILES * TM, w), lambda i: (i, 0))
    lat_pair = pl.BlockSpec((SUBTILES * TM, D), lambda i: (jnp.minimum(i, CTX_STEP - 1), 0))
    q, k, v, g, la = pl.pallas_call(
        _gla_proj_kernel,
        grid=(CTX_STEP + 1,),
        in_specs=[lat_pair, _const((CTX, D)), mod_spec(0), _const((1, D)),
                  _const((D, 2 * DK + 2 * DV)), _const((D, LANES)), _const((LANES, 2 * DK)),
                  _const((1, 2 * DK)), _const((1, DV))],
        out_specs=[pair(DK), pair(DK), pair(DV), pair(DV), pair(2 * DK)],
        out_shape=[jax.ShapeDtypeStruct((nseq, DK), BF16), jax.ShapeDtypeStruct((nseq, DK), BF16),
                   jax.ShapeDtypeStruct((nseq, DV), BF16), jax.ShapeDtypeStruct((nseq, DV), BF16),
                   jax.ShapeDtypeStruct((nseq, 2 * DK), F32)],
        scratch_shapes=[pltpu.VMEM((D, 2 * DK + 2 * DV), BF16)],
        compiler_params=_params(),
        name="gla_proj",
    )(x2d, ctx2d, mods, vec(norm_mix_g[0]), gla_w_in[0], gw1, gw2, gb,
      jnp.tile(vec(gla_head_norm_g[0]), (1, HEADS)))

    seq_f = lambda i: (jnp.where(i == 0, NT, i - 1), 0)
    seq_b = lambda i: (jnp.where(i == 0, NT, NT - i), 0)
    streamed = [ffn_w1.reshape(2 * D, FFN), ffn_w3.reshape(2 * D, FFN), ffn_w2.reshape(2 * FFN, D),
                gla_w_out[0], lru_w_in[0], lru_w_out[0]]
    assert len(streamed) == N_STREAMED_WEIGHTS

    def slab_spec(w):
        nblk = NT if w.shape[0] % (NT * BF16_SUBLANES) == 0 else NT // 2
        rows = w.shape[0] // nblk
        assert rows * nblk == w.shape[0] and rows % BF16_SUBLANES == 0
        return pl.BlockSpec((rows, w.shape[1]), lambda i: (jnp.minimum(i, nblk - 1), 0))

    slabs = [slab_spec(w) for w in streamed]
    la_b = lambda i: (seq_b(i)[0], 1)
    o_f, o_b, w1b, w3b, w2b, gwob, lwinb, lwob = pl.pallas_call(
        _gla_scan_kernel,
        grid=(NT + 1,),
        in_specs=[tile(DK, seq_f), tile(DK, seq_f), tile(DV, seq_f), tile(DK, seq_f),
                  tile(DK, seq_b), tile(DK, seq_b), tile(DV, seq_b), tile(DK, la_b)] + slabs,
        out_specs=[tile(DV, seq_f), tile(DV, seq_b)] + slabs,
        out_shape=[jax.ShapeDtypeStruct(((NT + 1) * TM, DV), BF16)] * 2
                  + [jax.ShapeDtypeStruct(w.shape, BF16) for w in streamed],
        scratch_shapes=[pltpu.VMEM((HEADS, HK, HV), F32)] * 2,
        compiler_params=_params(),
        name="gla_scan",
    )(q, k, v, la, q, k, v, la, *streamed)
    w1b, w3b, w2b = w1b.reshape(2, D, FFN), w3b.reshape(2, D, FFN), w2b.reshape(2, FFN, D)

    sub = lambda w, j: pl.BlockSpec((TM, w), lambda i: (jnp.minimum(SUBTILES * i + j, NT), 0))
    x_mid, y_br, u_all = pl.pallas_call(
        _gla_out_kernel,
        grid=(CTX_STEP + 1,),
        in_specs=[lat_pair, _const((CTX, D)), mod_spec(0), mod_spec(1),
                  sub(DV, 0), sub(DV, 1), sub(DV, 0), sub(DV, 1), sub(DV, 0), sub(DV, 1),
                  _const((DV, D))] + ffn_specs(0) + [_const((1, D)), _const((D, 2 * D))],
        out_specs=[pair(D), pair(D), pair(D)],
        out_shape=[jax.ShapeDtypeStruct((nseq, D), F32), jax.ShapeDtypeStruct((nseq, D), BF16),
                   jax.ShapeDtypeStruct((nseq, D), F32)],
        compiler_params=_params(),
        name="gla_out",
    )(x2d, ctx2d, mods, mods, o_f, o_f, o_b, o_b, g, g,
      gwob, vec(norm_ffn_g[0]), w1b, w3b, w2b,
      vec(norm_mix_g[1]), lwinb)

    u3 = u_all.reshape(nseq // GRID_W, GRID_W, D)
    uctx3 = u_all[NT * TM:(NT + 1) * TM].reshape(SUB, CTX_SEG, D).transpose(1, 0, 2)
    gate_w = lambda d: jnp.concatenate([lru_gate_a_w[0, d], lru_gate_x_w[0, d]], axis=-1).astype(BF16)
    gate_b = lambda d: 0.5 * jnp.stack([lru_gate_a_b[0, d], lru_gate_x_b[0, d]])
    chan = lambda rows: pl.BlockSpec((rows, LRU_BLOCK), lambda cb, wg: (0, cb))
    gate_spec = pl.BlockSpec((None, LRU_BLOCK, 2 * LRU_BLOCK), lambda cb, wg: (cb, 0, 0))
    lru_weights = (lru_conv_w[0], vec(lru_conv_b[0]), gate_w(0), gate_w(1), gate_b(0), gate_b(1),
                   lru_lambda[0])

    def column_scans(u3d, rows, blk_rows, ngroups):
        blk = (rows, SUB, LRU_BLOCK)
        here = pl.BlockSpec(blk, lambda cb, wg: (0, wg, cb))
        return pl.pallas_call(
            functools.partial(_lru_scan_kernel, rows=rows, blk_rows=blk_rows, ngroups=ngroups),
            grid=(LRU_BLOCKS, ngroups),
            in_specs=[here,
                      pl.BlockSpec((2, SUB, LRU_BLOCK), lambda cb, wg: (rows // 2 - 1, jnp.maximum(wg - 1, 0), cb)),
                      pl.BlockSpec((2, SUB, LRU_BLOCK), lambda cb, wg: (0, jnp.minimum(wg + 1, ngroups - 1), cb)),
                      chan(CONV_W), chan(1), gate_spec, gate_spec, chan(2), chan(2), chan(2)],
            out_specs=[here, here, here, pl.BlockSpec((4, SUB, LRU_BLOCK), lambda cb, wg: (0, wg, cb))],
            out_shape=[jax.ShapeDtypeStruct((rows, ngroups * SUB, D), F32)] * 3
                      + [jax.ShapeDtypeStruct((4, ngroups * SUB, D), F32)],
            scratch_shapes=[pltpu.VMEM((rows + CONV_W - 1, SUB, LRU_BLOCK), F32)],
            compiler_params=_params(2),
            name=f"lru_scan_{ngroups * SUB}x{rows}",
        )(u3d, u3d, u3d, *lru_weights)

    ctx_sums = column_scans(uctx3, CTX_SEG, CTX_SEG, 1)[3]
    s0, pf, pb, sums = column_scans(u3, ROWS, SCAN_ROWS, NGRP)

    h_f, h_b = pl.pallas_call(
        _lru_chain_kernel,
        out_shape=[jax.ShapeDtypeStruct((GRID_W, D), F32)] * 2,
        name="lru_chain",
    )(ctx_sums, sums)

    lat = lambda i: (i, 0)
    big = lambda f: pl.BlockSpec((OUT_SUBTILES * OUT_TM, D), f)
    out = pl.pallas_call(
        _lru_out_kernel,
        grid=(SEQ // (OUT_SUBTILES * OUT_TM),),
        in_specs=[big(lat), _const((None, None, SUB, D), (1, 0, 0, 0)),
                  big(lat), big(lat), big(lat), big(lat),
                  _const((GRID_W, D)), _const((GRID_W, D)), _const((D, D))] + ffn_specs(1) + [_const((1, D))],
        out_specs=big(lat),
        out_shape=jax.ShapeDtypeStruct((SEQ, D), F32),
        compiler_params=_params(),
        name="lru_out",
    )(x_mid, mods, y_br, s0.reshape(SEQ, D), pf.reshape(SEQ, D), pb.reshape(SEQ, D),
      h_f, h_b,
      lwob, vec(norm_ffn_g[1]), w1b, w3b, w2b, vec(final_norm_g))
    return out.reshape(1, SEQ, D)
```

```python
import functools

import jax
import jax.numpy as jnp
from jax import lax
from jax.experimental import pallas as pl
from jax.experimental.pallas import tpu as pltpu

D = 1024
SEQ = 16384
GRID_W = 64
ROWS = SEQ // GRID_W
CTX = 256
TM = 256
NT = SEQ // TM
HEADS = 4
DK = 512
DV = 1024
HK = DK // HEADS
HV = DV // HEADS
RANK = 16
TAU = 16.0
CHUNK = 64
NCH = TM // CHUNK
LRU_BLOCKS = 4
LRU_BLOCK = D // LRU_BLOCKS
LRU_C = 8.0
FFN = 2816
EPS = 1e-6
SUB = 8
LANES = 128
BF16_SUBLANES = 16
N_STREAMED_WEIGHTS = 6
CONV_W = 4
CONV_LEFT = 2
NGRP = GRID_W // SUB
SCAN_ROWS = 32
FFN_CHUNK = 256
CTX_SEG = CTX // SUB
SUBTILES = 2
CTX_STEP = NT // SUBTILES
OUT_TM = 256
OUT_SUBTILES = 2

F32 = jnp.float32
BF16 = jnp.bfloat16

V7X_VMEM_LIMIT = 56 * 1024 * 1024
LOG2E = 1.4426950408889634
TINY = 1e-30


def _dot(a, b):
    return jnp.dot(a, b, preferred_element_type=F32)


def _dot_nt(a, b):
    return lax.dot_general(a, b, (((1,), (1,)), ((), ())), preferred_element_type=F32)


def _dot_tn(a, b):
    return lax.dot_general(a, b, (((0,), (0,)), ((), ())), preferred_element_type=F32)


def _rms(x, g):
    ms = jnp.mean(x * x, axis=-1, keepdims=True)
    return x * lax.rsqrt(ms + EPS) * g


def _rms_mod(x, g, scale, shift):
    return _rms(x, g * (1.0 + scale)) + shift


def _silu(x):
    return x * jax.nn.sigmoid(x)


def _split_bf16(x):
    hi = x.astype(BF16)
    lo = (x - hi.astype(F32)).astype(BF16)
    return hi, lo


def _mod_kernel(c_ref, w_ref, b_ref, o_ref):
    s = _silu(c_ref[...])
    w = w_ref[...]
    b = b_ref[...]
    o_ref[...] = jnp.zeros_like(o_ref)
    for r in range(2):
        o_ref[r:r + 1, :] = jnp.sum(w * s[:, r:r + 1], axis=0, keepdims=True) + b


def _modulation(ccols, w_mod, b_mod):
    depth = w_mod.shape[0]
    tn = 1536
    return pl.pallas_call(
        _mod_kernel,
        grid=(depth, 6 * D // tn),
        in_specs=[
            pl.BlockSpec((D, 2), lambda l, n: (0, 0)),
            pl.BlockSpec((None, D, tn), lambda l, n: (l, 0, n)),
            pl.BlockSpec((None, 1, tn), lambda l, n: (l, 0, n)),
        ],
        out_specs=pl.BlockSpec((None, SUB, tn), lambda l, n: (l, 0, n)),
        out_shape=jax.ShapeDtypeStruct((depth, SUB, 6 * D), F32),
        compiler_params=pltpu.CompilerParams(
            dimension_semantics=("arbitrary", "arbitrary"),
            vmem_limit_bytes=V7X_VMEM_LIMIT),
        name="mod",
    )(ccols, w_mod, b_mod.reshape(depth, 1, 6 * D))


def _gla_tiles(jobs):
    row = lax.broadcasted_iota(jnp.int32, (TM, TM), 0)
    col = lax.broadcasted_iota(jnp.int32, (TM, TM), 1)
    shift = CHUNK.bit_length() - 1
    same = jnp.right_shift(row, shift) == jnp.right_shift(col, shift)
    prep = []
    for q, k, v, la, s_ref, o_ref, rev in jobs:
        tri = same & ((col >= row) if rev else (col <= row))
        tri_b = tri.astype(BF16)
        la_hi, la_lo = _split_bf16(la)
        b = _dot(tri_b, la_hi) + _dot(tri_b, la_lo)
        prep.append((tri, b))
    work = []
    for (q, k, v, la, s_ref, o_ref, rev), (tri, b) in zip(jobs, prep):
        ends = [c * CHUNK if rev else (c + 1) * CHUNK - 1 for c in range(NCH)]
        bl = jnp.concatenate([jnp.broadcast_to(b[r:r + 1], (CHUNK, DK)) for r in ends], axis=0)
        q_dec = (q * jnp.exp(b)).astype(BF16)
        k_inv = (k * jnp.exp(-b)).astype(BF16)
        k_end = (k * jnp.exp(bl - b)).astype(BF16)
        totals = jnp.concatenate([b[r:r + 1] for r in ends] + [jnp.zeros((SUB - NCH, DK), F32)], axis=0)
        decay_t = jnp.exp(totals).T
        order = list(range(NCH - 1, -1, -1) if rev else range(NCH))
        for h in range(HEADS):
            ks = slice(h * HK, (h + 1) * HK)
            vs = slice(h * HV, (h + 1) * HV)
            work.append(dict(tri=tri, qd=q_dec[:, ks], ki=k_inv[:, ks], ke=k_end[:, ks], vh=v[:, vs],
                             dec=decay_t[ks], order=order, s_ref=s_ref, h=h, o_ref=o_ref, vs=vs))
    for w in work:
        w["att"] = jnp.where(w["tri"], _dot_nt(w["qd"], w["ki"]), 0.0).astype(BF16)
    for w in work:
        w["o_intra"] = _dot(w["att"], w["vh"])
        w["st"] = w["s_ref"][w["h"]]
    for idx in range(NCH):
        for w in work:
            c = w["order"][idx]
            rs = slice(c * CHUNK, (c + 1) * CHUNK)
            o_inter = _dot(w["qd"][rs], w["st"].astype(BF16))
            w["o_ref"][rs, w["vs"]] = (w["o_intra"][rs] + o_inter).astype(w["o_ref"].dtype)
            u = _dot_tn(w["ke"][rs], w["vh"][rs])
            w["st"] = w["dec"][:, c:c + 1] * w["st"] + u
    for w in work:
        w["s_ref"][w["h"]] = w["st"]


def _proj_store(rs, z, zg, q_ref, k_ref, v_ref, g_ref, la_ref):
    q_ref[rs] = (z[:, :DK] * (HK ** -0.5)).astype(BF16)
    k_ref[rs] = z[:, DK:2 * DK].astype(BF16)
    v_ref[rs] = z[:, 2 * DK:2 * DK + DV].astype(BF16)
    g_ref[rs] = _silu(z[:, 2 * DK + DV:]).astype(BF16)
    la_ref[rs] = (jnp.minimum(zg, 0.0) - jnp.log(1.0 + jnp.exp(-jnp.abs(zg)))) * (1.0 / TAU)


def _gla_proj_kernel(x_ref, ctx_ref, mod_ref, gmix_ref, winf_ref, gw1_ref, gw2_ref, gb_ref,
                     q_ref, k_ref, v_ref, g_ref, la_ref, win_ref):
    @pl.when(pl.program_id(0) == 0)
    def _():
        win_ref[...] = winf_ref[...].astype(BF16)

    is_ctx = pl.program_id(0) == CTX_STEP
    mod = mod_ref[...]
    rows = [slice(j * TM, (j + 1) * TM) for j in range(SUBTILES)]
    a = [_rms_mod(jnp.where(is_ctx, ctx_ref[...], x_ref[rs]), gmix_ref[...], mod[1:2], mod[0:1]).astype(BF16)
         for rs in rows]
    pending = None
    for rs, at in zip(rows, a):
        z = _dot(at, win_ref[...])
        t = _dot(at, gw1_ref[...]).astype(BF16)
        zg = _dot(t, gw2_ref[...]) + gb_ref[...]
        if pending is not None:
            _proj_store(*pending, q_ref, k_ref, v_ref, g_ref, la_ref)
        pending = (rs, z, zg)
    _proj_store(*pending, q_ref, k_ref, v_ref, g_ref, la_ref)


def _gla_scan_kernel(qf_ref, kf_ref, vf_ref, laf_ref, qb_ref, kb_ref, vb_ref, lab_ref, *rest):
    nw = N_STREAMED_WEIGHTS
    w_in, (of_ref, ob_ref), w_out, (sf_ref, sb_ref) = (
        rest[:nw], rest[nw:nw + 2], rest[nw + 2:2 * nw + 2], rest[2 * nw + 2:])

    @pl.when(pl.program_id(0) == 0)
    def _():
        sf_ref[...] = jnp.zeros_like(sf_ref)
        sb_ref[...] = jnp.zeros_like(sb_ref)

    _gla_tiles([
        (qf_ref[...].astype(F32), kf_ref[...].astype(F32), vf_ref[...], laf_ref[...], sf_ref, of_ref, False),
        (qb_ref[...].astype(F32), kb_ref[...].astype(F32), vb_ref[...], lab_ref[...], sb_ref, ob_ref, True)])
    for src, dst in zip(w_in, w_out):
        dst[...] = src[...].astype(BF16)


def _ffn_stages(x1, mod, gffn, w1_ref, w3_ref, w2_ref):
    f = [_rms_mod(t, gffn, mod[4:5], mod[3:4]).astype(BF16) for t in x1]
    p = [jnp.concatenate(
        [(_silu(_dot(t, w1_ref[:, c:c + FFN_CHUNK])) * _dot(t, w3_ref[:, c:c + FFN_CHUNK])).astype(BF16)
         for c in range(0, FFN, FFN_CHUNK)], axis=1) for t in f]
    return [t + mod[5:6] * _dot(pt, w2_ref[...]) for t, pt in zip(x1, p)]


def _gla_out_kernel(x_ref, ctx_ref, mod0_ref, mod1_ref, of0_ref, of1_ref, ob0_ref, ob1_ref,
                    g0_ref, g1_ref,
                    hn_ref, wout_ref, gffn_ref, w1_ref, w3_ref, w2_ref, gmix1_ref, lwin_ref,
                    xo_ref, y_ref, u_ref):
    is_ctx = pl.program_id(0) == CTX_STEP
    hn = hn_ref[...]
    mod0 = mod0_ref[...]
    mod1 = mod1_ref[...]
    subtiles = ((of0_ref, ob0_ref, g0_ref), (of1_ref, ob1_ref, g1_ref))
    rows = [slice(j * TM, (j + 1) * TM) for j in range(SUBTILES)]
    gffn = gffn_ref[...]
    v = [dict() for _ in subtiles]

    def gate(j):
        of_ref, ob_ref, g_ref = subtiles[j]
        y = None
        for h in range(HEADS):
            hs = slice(h * HV, (h + 1) * HV)
            o = of_ref[:, hs].astype(F32) + ob_ref[:, hs].astype(F32)
            gated = (_rms(o, hn) * g_ref[:, hs].astype(F32)).astype(BF16)
            part = _dot(gated, wout_ref[hs, :])
            y = part if y is None else y + part
        v[j]["y"] = y

    def mix_out(j):
        xt = jnp.where(is_ctx, ctx_ref[...], x_ref[rows[j]])
        v[j]["x1"] = xt + mod0[2:3] * v[j]["y"]
        v[j]["f"] = _rms_mod(v[j]["x1"], gffn, mod0[4:5], mod0[3:4]).astype(BF16)

    def ffn_up(j):
        f = v[j]["f"]
        v[j]["p"] = jnp.concatenate(
            [(_silu(_dot(f, w1_ref[:, c:c + FFN_CHUNK])) * _dot(f, w3_ref[:, c:c + FFN_CHUNK])).astype(BF16)
             for c in range(0, FFN, FFN_CHUNK)], axis=1)

    def ffn_down(j):
        x2 = v[j]["x1"] + mod0[5:6] * _dot(v[j]["p"], w2_ref[...])
        xo_ref[rows[j]] = x2
        v[j]["a"] = _rms_mod(x2, gmix1_ref[...], mod1[1:2], mod1[0:1]).astype(BF16)

    def lru_proj(j):
        v[j]["z"] = _dot(v[j]["a"], lwin_ref[...])

    def lru_store(j):
        y_ref[rows[j]] = jax.nn.gelu(v[j]["z"][:, :D]).astype(BF16)
        u_ref[rows[j]] = v[j]["z"][:, D:]

    for stage in (gate, mix_out, ffn_up, ffn_down, lru_proj, lru_store):
        for j in range(SUBTILES):
            stage(j)


def _lru_coeffs(ub, uch, wg_ref, bg_ref, c2, rows):
    zz = _dot(ub, wg_ref[...])
    bg = bg_ref[...]
    ta = jnp.tanh(zz[:, :LRU_BLOCK] + bg[0:1])
    tx = jnp.tanh(zz[:, LRU_BLOCK:] + bg[1:2])
    a = jnp.exp2(c2 * ta + c2)
    v = 1.0 - a * a
    s = v * lax.rsqrt(jnp.maximum(v, TINY))
    b = s * ((tx + 1.0) * uch)
    shape = (rows, SUB, LRU_BLOCK)
    return a.reshape(shape), b.reshape(shape)


def _lru_scan_kernel(u_ref, hp_ref, hn_ref, cw_ref, cb_ref, wg0_ref, wg1_ref,
                     bg0_ref, bg1_ref, lam_ref,
                     s0_ref, pf_ref, pb_ref, sum_ref,
                     e_ref, *, rows, blk_rows, ngroups):
    wg = pl.program_id(1)
    u = u_ref[...]
    prev_ok = wg >= 1
    next_ok = wg <= ngroups - 2
    sub = lax.broadcasted_iota(jnp.int32, (SUB, LRU_BLOCK), 0)
    hp = hp_ref[...]
    hn = hn_ref[...]

    def from_prev_column(cur, halo):
        edge = jnp.where(prev_ok, pltpu.roll(halo, 1, 0), 0.0)
        return jnp.where(sub == 0, edge, pltpu.roll(cur, 1, 0))

    e_ref[CONV_LEFT:CONV_LEFT + rows] = u
    e_ref[0] = from_prev_column(u[rows - 2], hp[0])
    e_ref[1] = from_prev_column(u[rows - 1], hp[1])
    edge = jnp.where(next_ok, pltpu.roll(hn[0], SUB - 1, 0), 0.0)
    e_ref[CONV_LEFT + rows] = jnp.where(sub == SUB - 1, edge, pltpu.roll(u[0], SUB - 1, 0))
    cw = 0.5 * cw_ref[...]
    cb = 0.5 * cb_ref[...]
    lam = lam_ref[...]
    c2 = (-0.5 * LRU_C * LOG2E) * jax.nn.softplus(-lam)
    nblk = rows // blk_rows
    ab = [[None] * nblk, [None] * nblk]
    for k in range(nblk):
        for d, blk in ((0, k), (1, nblk - 1 - k)):
            r0 = blk * blk_rows
            uch = cb + cw[0:1] * e_ref[r0:r0 + blk_rows]
            for j in range(1, CONV_W):
                uch = uch + cw[j:j + 1] * e_ref[r0 + j:r0 + j + blk_rows]
            uch = uch.reshape(blk_rows * SUB, LRU_BLOCK)
            ab[d][blk] = _lru_coeffs(uch.astype(BF16), uch, (wg0_ref, wg1_ref)[d], (bg0_ref, bg1_ref)[d],
                                     c2[d:d + 1], blk_rows)

    h = jnp.zeros((SUB, LRU_BLOCK), F32)
    p = jnp.ones((SUB, LRU_BLOCK), F32)
    for r in range(rows):
        a, b = ab[0][r // blk_rows]
        h = a[r % blk_rows] * h + b[r % blk_rows]
        p = a[r % blk_rows] * p
        s0_ref[r] = h
        pf_ref[r] = p
    sum_ref[0] = p
    sum_ref[1] = h
    h = jnp.zeros((SUB, LRU_BLOCK), F32)
    p = jnp.ones((SUB, LRU_BLOCK), F32)
    for r in range(rows - 1, -1, -1):
        a, b = ab[1][r // blk_rows]
        h = a[r % blk_rows] * h + b[r % blk_rows]
        p = a[r % blk_rows] * p
        s0_ref[r] = s0_ref[r] + h
        pb_ref[r] = p
    sum_ref[2] = p
    sum_ref[3] = h


def _lru_chain(csum_ref, sum_ref, hf_ref, hb_ref):
    s = jnp.zeros((1, D), F32)
    for w in range(SUB):
        s = csum_ref[0, w:w + 1, :] * s + csum_ref[1, w:w + 1, :]
    for w in range(GRID_W):
        hf_ref[w:w + 1, :] = s
        s = sum_ref[0, w:w + 1, :] * s + sum_ref[1, w:w + 1, :]
    s = jnp.zeros((1, D), F32)
    for w in range(SUB - 1, -1, -1):
        s = csum_ref[2, w:w + 1, :] * s + csum_ref[3, w:w + 1, :]
    for w in range(GRID_W - 1, -1, -1):
        hb_ref[w:w + 1, :] = s
        s = sum_ref[2, w:w + 1, :] * s + sum_ref[3, w:w + 1, :]


def _lru_out_kernel(x_ref, mod_ref, y_ref, s0_ref, pf_ref, pb_ref, csum_ref, sum_ref,
                    wout_ref, gffn_ref, w1_ref, w3_ref, w2_ref, gfin_ref, o_ref, hf_ref, hb_ref):
    @pl.when(pl.program_id(0) == 0)
    def _():
        _lru_chain(csum_ref, sum_ref, hf_ref, hb_ref)

    mod = mod_ref[...]
    rows = [slice(j * OUT_TM, (j + 1) * OUT_TM) for j in range(OUT_SUBTILES)]
    by_col = (OUT_TM // GRID_W, GRID_W, LRU_BLOCK)
    x1 = []
    for rs in rows:
        yl = None
        for c in range(LRU_BLOCKS):
            cs = slice(c * LRU_BLOCK, (c + 1) * LRU_BLOCK)
            hs = (s0_ref[rs, cs].reshape(by_col) + pf_ref[rs, cs].reshape(by_col) * hf_ref[:, cs]
                  + pb_ref[rs, cs].reshape(by_col) * hb_ref[:, cs]).reshape(OUT_TM, LRU_BLOCK)
            ym = (hs * y_ref[rs, cs].astype(F32)).astype(BF16)
            part = _dot(ym, wout_ref[cs, :])
            yl = part if yl is None else yl + part
        x1.append(x_ref[rs] + mod[2:3] * yl)
    x2 = _ffn_stages(x1, mod, gffn_ref[...], w1_ref, w3_ref, w2_ref)
    for rs, t in zip(rows, x2):
        o_ref[rs] = _rms(t, gfin_ref[...])


def _const(shape, index=None):
    nd = len(shape)
    index = (0,) * nd if index is None else index
    return pl.BlockSpec(shape, lambda *_: index, pipeline_mode=pl.Buffered(1))


def _params(n_axes=1):
    return pltpu.CompilerParams(dimension_semantics=("arbitrary",) * n_axes,
                                vmem_limit_bytes=V7X_VMEM_LIMIT)


def kernel(x, c, ctx, c_ctx, norm_mix_g, norm_ffn_g, w_mod, b_mod, gla_w_in, gla_gate_w1, gla_gate_w2, gla_gate_b, gla_head_norm_g, gla_w_out, lru_w_in, lru_conv_w, lru_conv_b, lru_gate_a_w, lru_gate_a_b, lru_gate_x_w, lru_gate_x_b, lru_lambda, lru_w_out, ffn_w1, ffn_w3, ffn_w2, final_norm_g):
    assert x.shape == (1, SEQ, D) and ctx.shape == (1, CTX, D) and CTX == TM == ROWS
    x2d = x.reshape(SEQ, D)
    ctx2d = ctx.reshape(CTX, D)
    nseq = (NT + SUBTILES) * TM

    m = _modulation(jnp.stack([c[0], c_ctx], axis=1), w_mod, b_mod)
    mods = m[:, :2].reshape(2, 2, 6, D)
    mods = jnp.pad(mods, ((0, 0), (0, 0), (0, SUB - 6), (0, 0)))

    def mod_spec(layer):
        return pl.BlockSpec((None, None, SUB, D), lambda i: (layer, jnp.where(i == CTX_STEP, 1, 0), 0, 0))

    vec = lambda t: t.reshape(1, -1)
    tile = lambda w, f: pl.BlockSpec((TM, w), f)

    def ffn_specs(layer):
        return [_const((1, D)), _const((None, D, FFN), (layer, 0, 0)),
                _const((None, D, FFN), (layer, 0, 0)), _const((None, FFN, D), (layer, 0, 0))]

    gw1 = jnp.concatenate([gla_gate_w1[0, 0], gla_gate_w1[0, 1]], axis=1)
    gw1 = jnp.pad(gw1, ((0, 0), (0, LANES - 2 * RANK))).astype(BF16)
    gw2 = jnp.zeros((LANES, 2 * DK), F32)
    gw2 = gw2.at[:RANK, :DK].set(gla_gate_w2[0, 0]).at[RANK:2 * RANK, DK:].set(gla_gate_w2[0, 1])
    gw2 = gw2.astype(BF16)
    gb = gla_gate_b[0].reshape(1, 2 * DK)

    pair = lambda w: pl.BlockSpec((SUBTILES * TM, w), lambda i: (i, 0))
    lat_pair = pl.BlockSpec((SUBTILES * TM, D), lambda i: (jnp.minimum(i, CTX_STEP - 1), 0))
    q, k, v, g, la = pl.pallas_call(
        _gla_proj_kernel,
        grid=(CTX_STEP + 1,),
        in_specs=[lat_pair, _const((CTX, D)), mod_spec(0), _const((1, D)),
                  _const((D, 2 * DK + 2 * DV)), _const((D, LANES)), _const((LANES, 2 * DK)),
                  _const((1, 2 * DK))],
        out_specs=[pair(DK), pair(DK), pair(DV), pair(DV), pair(2 * DK)],
        out_shape=[jax.ShapeDtypeStruct((nseq, DK), BF16), jax.ShapeDtypeStruct((nseq, DK), BF16),
                   jax.ShapeDtypeStruct((nseq, DV), BF16), jax.ShapeDtypeStruct((nseq, DV), BF16),
                   jax.ShapeDtypeStruct((nseq, 2 * DK), F32)],
        scratch_shapes=[pltpu.VMEM((D, 2 * DK + 2 * DV), BF16)],
        compiler_params=_params(),
        name="gla_proj",
    )(x2d, ctx2d, mods, vec(norm_mix_g[0]), gla_w_in[0], gw1, gw2, gb)

    seq_f = lambda i: (jnp.where(i == 0, NT, i - 1), 0)
    seq_b = lambda i: (jnp.where(i == 0, NT, NT - i), 0)
    streamed = [ffn_w1.reshape(2 * D, FFN), ffn_w3.reshape(2 * D, FFN), ffn_w2.reshape(2 * FFN, D),
                gla_w_out[0], lru_w_in[0], lru_w_out[0]]
    assert len(streamed) == N_STREAMED_WEIGHTS

    def slab_spec(w):
        nblk = NT if w.shape[0] % (NT * BF16_SUBLANES) == 0 else NT // 2
        rows = w.shape[0] // nblk
        assert rows * nblk == w.shape[0] and rows % BF16_SUBLANES == 0
        return pl.BlockSpec((rows, w.shape[1]), lambda i: (jnp.minimum(i, nblk - 1), 0))

    slabs = [slab_spec(w) for w in streamed]
    la_b = lambda i: (seq_b(i)[0], 1)
    o_f, o_b, w1b, w3b, w2b, gwob, lwinb, lwob = pl.pallas_call(
        _gla_scan_kernel,
        grid=(NT + 1,),
        in_specs=[tile(DK, seq_f), tile(DK, seq_f), tile(DV, seq_f), tile(DK, seq_f),
                  tile(DK, seq_b), tile(DK, seq_b), tile(DV, seq_b), tile(DK, la_b)] + slabs,
        out_specs=[tile(DV, seq_f), tile(DV, seq_b)] + slabs,
        out_shape=[jax.ShapeDtypeStruct(((NT + 1) * TM, DV), BF16)] * 2
                  + [jax.ShapeDtypeStruct(w.shape, BF16) for w in streamed],
        scratch_shapes=[pltpu.VMEM((HEADS, HK, HV), F32)] * 2,
        compiler_params=_params(),
        name="gla_scan",
    )(q, k, v, la, q, k, v, la, *streamed)
    w1b, w3b, w2b = w1b.reshape(2, D, FFN), w3b.reshape(2, D, FFN), w2b.reshape(2, FFN, D)

    sub = lambda w, j: pl.BlockSpec((TM, w), lambda i: (jnp.minimum(SUBTILES * i + j, NT), 0))
    x_mid, y_br, u_all = pl.pallas_call(
        _gla_out_kernel,
        grid=(CTX_STEP + 1,),
        in_specs=[lat_pair, _const((CTX, D)), mod_spec(0), mod_spec(1),
                  sub(DV, 0), sub(DV, 1), sub(DV, 0), sub(DV, 1), sub(DV, 0), sub(DV, 1),
                  _const((1, HV)), _const((DV, D))] + ffn_specs(0) + [_const((1, D)), _const((D, 2 * D))],
        out_specs=[pair(D), pair(D), pair(D)],
        out_shape=[jax.ShapeDtypeStruct((nseq, D), F32), jax.ShapeDtypeStruct((nseq, D), BF16),
                   jax.ShapeDtypeStruct((nseq, D), F32)],
        compiler_params=_params(),
        name="gla_out",
    )(x2d, ctx2d, mods, mods, o_f, o_f, o_b, o_b, g, g,
      vec(gla_head_norm_g[0]), gwob, vec(norm_ffn_g[0]), w1b, w3b, w2b,
      vec(norm_mix_g[1]), lwinb)

    u3 = u_all.reshape(nseq // GRID_W, GRID_W, D)
    uctx3 = u_all[NT * TM:(NT + 1) * TM].reshape(SUB, CTX_SEG, D).transpose(1, 0, 2)
    gate_w = lambda d: jnp.concatenate([lru_gate_a_w[0, d], lru_gate_x_w[0, d]], axis=-1).astype(BF16)
    gate_b = lambda d: 0.5 * jnp.stack([lru_gate_a_b[0, d], lru_gate_x_b[0, d]])
    chan = lambda rows: pl.BlockSpec((rows, LRU_BLOCK), lambda cb, wg: (0, cb))
    gate_spec = pl.BlockSpec((None, LRU_BLOCK, 2 * LRU_BLOCK), lambda cb, wg: (cb, 0, 0))
    lru_weights = (lru_conv_w[0], vec(lru_conv_b[0]), gate_w(0), gate_w(1), gate_b(0), gate_b(1),
                   lru_lambda[0])

    def column_scans(u3d, rows, blk_rows, ngroups):
        blk = (rows, SUB, LRU_BLOCK)
        here = pl.BlockSpec(blk, lambda cb, wg: (0, wg, cb))
        return pl.pallas_call(
            functools.partial(_lru_scan_kernel, rows=rows, blk_rows=blk_rows, ngroups=ngroups),
            grid=(LRU_BLOCKS, ngroups),
            in_specs=[here,
                      pl.BlockSpec((2, SUB, LRU_BLOCK), lambda cb, wg: (rows // 2 - 1, jnp.maximum(wg - 1, 0), cb)),
                      pl.BlockSpec((2, SUB, LRU_BLOCK), lambda cb, wg: (0, jnp.minimum(wg + 1, ngroups - 1), cb)),
                      chan(CONV_W), chan(1), gate_spec, gate_spec, chan(2), chan(2), chan(2)],
            out_specs=[here, here, here, pl.BlockSpec((4, SUB, LRU_BLOCK), lambda cb, wg: (0, wg, cb))],
            out_shape=[jax.ShapeDtypeStruct((rows, ngroups * SUB, D), F32)] * 3
                      + [jax.ShapeDtypeStruct((4, ngroups * SUB, D), F32)],
            scratch_shapes=[pltpu.VMEM((rows + CONV_W - 1, SUB, LRU_BLOCK), F32)],
            compiler_params=_params(2),
            name=f"lru_scan_{ngroups * SUB}x{rows}",
        )(u3d, u3d, u3d, *lru_weights)

    ctx_sums = column_scans(uctx3, CTX_SEG, CTX_SEG, 1)[3]
    s0, pf, pb, sums = column_scans(u3, ROWS, SCAN_ROWS, NGRP)

    lat = lambda i: (i, 0)
    big = lambda f: pl.BlockSpec((OUT_SUBTILES * OUT_TM, D), f)
    out = pl.pallas_call(
        _lru_out_kernel,
        grid=(SEQ // (OUT_SUBTILES * OUT_TM),),
        in_specs=[big(lat), _const((None, None, SUB, D), (1, 0, 0, 0)),
                  big(lat), big(lat), big(lat), big(lat),
                  _const((4, SUB, D)), _const((4, GRID_W, D)), _const((D, D))] + ffn_specs(1) + [_const((1, D))],
        out_specs=big(lat),
        out_shape=jax.ShapeDtypeStruct((SEQ, D), F32),
        scratch_shapes=[pltpu.VMEM((GRID_W, D), F32)] * 2,
        compiler_params=_params(),
        name="lru_out",
    )(x_mid, mods, y_br, s0.reshape(SEQ, D), pf.reshape(SEQ, D), pb.reshape(SEQ, D),
      ctx_sums, sums,
      lwob, vec(norm_ffn_g[1]), w1b, w3b, w2b, vec(final_norm_g))
    return out.reshape(1, SEQ, D)
```

```python
import functools

import jax
import jax.numpy as jnp
from jax import lax
from jax.experimental import pallas as pl
from jax.experimental.pallas import tpu as pltpu

D = 1024
SEQ = 16384
GRID_W = 64
ROWS = SEQ // GRID_W
CTX = 256
TM = 256
NT = SEQ // TM
HEADS = 4
DK = 512
DV = 1024
HK = DK // HEADS
HV = DV // HEADS
RANK = 16
TAU = 16.0
CHUNK = 64
NCH = TM // CHUNK
LRU_BLOCKS = 4
LRU_BLOCK = D // LRU_BLOCKS
LRU_C = 8.0
FFN = 2816
EPS = 1e-6
SUB = 8
LANES = 128
BF16_SUBLANES = 16
N_STREAMED_WEIGHTS = 6
CONV_W = 4
CONV_LEFT = 2
NGRP = GRID_W // SUB
SCAN_ROWS = 32
FFN_CHUNK = 256
CTX_SEG = CTX // SUB
SUBTILES = 2
CTX_STEP = NT // SUBTILES
OUT_TM = 256
OUT_SUBTILES = 2

F32 = jnp.float32
BF16 = jnp.bfloat16

V7X_VMEM_LIMIT = 56 * 1024 * 1024
LOG2E = 1.4426950408889634
TINY = 1e-30


def _dot(a, b):
    return jnp.dot(a, b, preferred_element_type=F32)


def _dot_nt(a, b):
    return lax.dot_general(a, b, (((1,), (1,)), ((), ())), preferred_element_type=F32)


def _dot_tn(a, b):
    return lax.dot_general(a, b, (((0,), (0,)), ((), ())), preferred_element_type=F32)


def _rms(x, g):
    ms = jnp.mean(x * x, axis=-1, keepdims=True)
    return x * lax.rsqrt(ms + EPS) * g


def _rms_mod(x, g, scale, shift):
    return _rms(x, g * (1.0 + scale)) + shift


def _silu(x):
    return x * jax.nn.sigmoid(x)


def _mod_rows(m_ref, row):
    r = m_ref[pl.ds(row, 1), :]
    return jnp.concatenate([r[:, j * D:(j + 1) * D] for j in range(6)], axis=0)


def _split_bf16(x):
    hi = x.astype(BF16)
    lo = (x - hi.astype(F32)).astype(BF16)
    return hi, lo


def _mod_kernel(c_ref, w_ref, b_ref, o_ref):
    s = _silu(c_ref[...])
    w = w_ref[...]
    b = b_ref[...]
    o_ref[...] = jnp.zeros_like(o_ref)
    for r in range(2):
        o_ref[r:r + 1, :] = jnp.sum(w * s[:, r:r + 1], axis=0, keepdims=True) + b


def _modulation(ccols, w_mod, b_mod):
    depth = w_mod.shape[0]
    tn = 1536
    return pl.pallas_call(
        _mod_kernel,
        grid=(depth, 6 * D // tn),
        in_specs=[
            pl.BlockSpec((D, 2), lambda l, n: (0, 0)),
            pl.BlockSpec((None, D, tn), lambda l, n: (l, 0, n)),
            pl.BlockSpec((None, 1, tn), lambda l, n: (l, 0, n)),
        ],
        out_specs=pl.BlockSpec((None, SUB, tn), lambda l, n: (l, 0, n)),
        out_shape=jax.ShapeDtypeStruct((depth, SUB, 6 * D), F32),
        compiler_params=pltpu.CompilerParams(
            dimension_semantics=("arbitrary", "arbitrary"),
            vmem_limit_bytes=V7X_VMEM_LIMIT),
        name="mod",
    )(ccols, w_mod, b_mod.reshape(depth, 1, 6 * D))


def _gla_tiles(jobs):
    row = lax.broadcasted_iota(jnp.int32, (TM, TM), 0)
    col = lax.broadcasted_iota(jnp.int32, (TM, TM), 1)
    shift = CHUNK.bit_length() - 1
    same = jnp.right_shift(row, shift) == jnp.right_shift(col, shift)
    prep = []
    for q, k, v, la, s_ref, o_ref, rev in jobs:
        tri = same & ((col >= row) if rev else (col <= row))
        tri_b = tri.astype(BF16)
        la_hi, la_lo = _split_bf16(la)
        b = _dot(tri_b, la_hi) + _dot(tri_b, la_lo)
        prep.append((tri, b))
    work = []
    for (q, k, v, la, s_ref, o_ref, rev), (tri, b) in zip(jobs, prep):
        ends = [c * CHUNK if rev else (c + 1) * CHUNK - 1 for c in range(NCH)]
        bl = jnp.concatenate([jnp.broadcast_to(b[r:r + 1], (CHUNK, DK)) for r in ends], axis=0)
        q_dec = (q * jnp.exp(b)).astype(BF16)
        k_inv = (k * jnp.exp(-b)).astype(BF16)
        k_end = (k * jnp.exp(bl - b)).astype(BF16)
        totals = jnp.concatenate([b[r:r + 1] for r in ends] + [jnp.zeros((SUB - NCH, DK), F32)], axis=0)
        decay_t = jnp.exp(totals).T
        order = list(range(NCH - 1, -1, -1) if rev else range(NCH))
        for h in range(HEADS):
            ks = slice(h * HK, (h + 1) * HK)
            vs = slice(h * HV, (h + 1) * HV)
            work.append(dict(tri=tri, qd=q_dec[:, ks], ki=k_inv[:, ks], ke=k_end[:, ks], vh=v[:, vs],
                             dec=decay_t[ks], order=order, s_ref=s_ref, h=h, o_ref=o_ref, vs=vs))
    for w in work:
        w["att"] = jnp.where(w["tri"], _dot_nt(w["qd"], w["ki"]), 0.0).astype(BF16)
    for w in work:
        w["o_intra"] = _dot(w["att"], w["vh"])
        w["st"] = w["s_ref"][w["h"]]
    for idx in range(NCH):
        for w in work:
            c = w["order"][idx]
            rs = slice(c * CHUNK, (c + 1) * CHUNK)
            o_inter = _dot(w["qd"][rs], w["st"].astype(BF16))
            w["o_ref"][rs, w["vs"]] = (w["o_intra"][rs] + o_inter).astype(w["o_ref"].dtype)
            u = _dot_tn(w["ke"][rs], w["vh"][rs])
            w["st"] = w["dec"][:, c:c + 1] * w["st"] + u
    for w in work:
        w["s_ref"][w["h"]] = w["st"]


def _proj_store(rs, z, zg, q_ref, k_ref, v_ref, g_ref, la_ref):
    q_ref[rs] = (z[:, :DK] * (HK ** -0.5)).astype(BF16)
    k_ref[rs] = z[:, DK:2 * DK].astype(BF16)
    v_ref[rs] = z[:, 2 * DK:2 * DK + DV].astype(BF16)
    g_ref[rs] = _silu(z[:, 2 * DK + DV:]).astype(BF16)
    la_ref[rs] = (jnp.minimum(zg, 0.0) - jnp.log(1.0 + jnp.exp(-jnp.abs(zg)))) * (1.0 / TAU)


def _gla_proj_kernel(x_ref, ctx_ref, mod_ref, gmix_ref, winf_ref, gw1_ref, gw2_ref, gb_ref,
                     q_ref, k_ref, v_ref, g_ref, la_ref, win_ref):
    @pl.when(pl.program_id(0) == 0)
    def _():
        win_ref[...] = winf_ref[...].astype(BF16)

    is_ctx = pl.program_id(0) == CTX_STEP
    mod = _mod_rows(mod_ref, is_ctx.astype(jnp.int32))
    rows = [slice(j * TM, (j + 1) * TM) for j in range(SUBTILES)]
    a = [_rms_mod(jnp.where(is_ctx, ctx_ref[...], x_ref[rs]), gmix_ref[...], mod[1:2], mod[0:1]).astype(BF16)
         for rs in rows]
    pending = None
    for rs, at in zip(rows, a):
        z = _dot(at, win_ref[...])
        t = _dot(at, gw1_ref[...]).astype(BF16)
        zg = _dot(t, gw2_ref[...]) + gb_ref[...]
        if pending is not None:
            _proj_store(*pending, q_ref, k_ref, v_ref, g_ref, la_ref)
        pending = (rs, z, zg)
    _proj_store(*pending, q_ref, k_ref, v_ref, g_ref, la_ref)


def _gla_scan_kernel(qf_ref, kf_ref, vf_ref, laf_ref, qb_ref, kb_ref, vb_ref, lab_ref, *rest):
    nw = N_STREAMED_WEIGHTS
    w_in, (of_ref, ob_ref), w_out, (sf_ref, sb_ref) = (
        rest[:nw], rest[nw:nw + 2], rest[nw + 2:2 * nw + 2], rest[2 * nw + 2:])

    @pl.when(pl.program_id(0) == 0)
    def _():
        sf_ref[...] = jnp.zeros_like(sf_ref)
        sb_ref[...] = jnp.zeros_like(sb_ref)

    _gla_tiles([
        (qf_ref[...].astype(F32), kf_ref[...].astype(F32), vf_ref[...], laf_ref[...], sf_ref, of_ref, False),
        (qb_ref[...].astype(F32), kb_ref[...].astype(F32), vb_ref[...], lab_ref[...], sb_ref, ob_ref, True)])
    for src, dst in zip(w_in, w_out):
        dst[...] = src[...].astype(BF16)


def _ffn_stages(x1, mod, gffn, w1_ref, w3_ref, w2_ref):
    f = [_rms_mod(t, gffn, mod[4:5], mod[3:4]).astype(BF16) for t in x1]
    p = [jnp.concatenate(
        [(_silu(_dot(t, w1_ref[:, c:c + FFN_CHUNK])) * _dot(t, w3_ref[:, c:c + FFN_CHUNK])).astype(BF16)
         for c in range(0, FFN, FFN_CHUNK)], axis=1) for t in f]
    return [t + mod[5:6] * _dot(pt, w2_ref[...]) for t, pt in zip(x1, p)]


def _gla_out_kernel(x_ref, ctx_ref, mod0_ref, mod1_ref, of0_ref, of1_ref, ob0_ref, ob1_ref,
                    g0_ref, g1_ref,
                    hn_ref, wout_ref, gffn_ref, w1_ref, w3_ref, w2_ref, gmix1_ref, lwin_ref,
                    xo_ref, y_ref, u_ref):
    is_ctx = pl.program_id(0) == CTX_STEP
    hn = hn_ref[...]
    mod0 = _mod_rows(mod0_ref, is_ctx.astype(jnp.int32))
    mod1 = _mod_rows(mod1_ref, is_ctx.astype(jnp.int32))
    subtiles = ((of0_ref, ob0_ref, g0_ref), (of1_ref, ob1_ref, g1_ref))
    rows = [slice(j * TM, (j + 1) * TM) for j in range(SUBTILES)]
    gffn = gffn_ref[...]
    v = [dict() for _ in subtiles]

    def gate(j):
        of_ref, ob_ref, g_ref = subtiles[j]
        y = None
        for h in range(HEADS):
            hs = slice(h * HV, (h + 1) * HV)
            o = of_ref[:, hs].astype(F32) + ob_ref[:, hs].astype(F32)
            gated = (_rms(o, hn) * g_ref[:, hs].astype(F32)).astype(BF16)
            part = _dot(gated, wout_ref[hs, :])
            y = part if y is None else y + part
        v[j]["y"] = y

    def mix_out(j):
        xt = jnp.where(is_ctx, ctx_ref[...], x_ref[rows[j]])
        v[j]["x1"] = xt + mod0[2:3] * v[j]["y"]
        v[j]["f"] = _rms_mod(v[j]["x1"], gffn, mod0[4:5], mod0[3:4]).astype(BF16)

    def ffn_up(j):
        f = v[j]["f"]
        v[j]["p"] = jnp.concatenate(
            [(_silu(_dot(f, w1_ref[:, c:c + FFN_CHUNK])) * _dot(f, w3_ref[:, c:c + FFN_CHUNK])).astype(BF16)
             for c in range(0, FFN, FFN_CHUNK)], axis=1)

    def ffn_down(j):
        x2 = v[j]["x1"] + mod0[5:6] * _dot(v[j]["p"], w2_ref[...])
        xo_ref[rows[j]] = x2
        v[j]["a"] = _rms_mod(x2, gmix1_ref[...], mod1[1:2], mod1[0:1]).astype(BF16)

    def lru_proj(j):
        v[j]["z"] = _dot(v[j]["a"], lwin_ref[...])

    def lru_store(j):
        y_ref[rows[j]] = jax.nn.gelu(v[j]["z"][:, :D]).astype(BF16)
        u_ref[rows[j]] = v[j]["z"][:, D:]

    for stage in (gate, mix_out, ffn_up, ffn_down, lru_proj, lru_store):
        for j in range(SUBTILES):
            stage(j)


def _lru_coeffs(ub, uch, wg_ref, bg_ref, c2, rows):
    zz = _dot(ub, wg_ref[...])
    bg = bg_ref[...]
    ta = jnp.tanh(zz[:, :LRU_BLOCK] + bg[0:1])
    tx = jnp.tanh(zz[:, LRU_BLOCK:] + bg[1:2])
    a = jnp.exp2(c2 * ta + c2)
    v = 1.0 - a * a
    s = v * lax.rsqrt(jnp.maximum(v, TINY))
    b = s * ((tx + 1.0) * uch)
    shape = (rows, SUB, LRU_BLOCK)
    return a.reshape(shape), b.reshape(shape)


def _lru_scan_kernel(u_ref, hp_ref, hn_ref, cw_ref, cb_ref, wg0_ref, wg1_ref,
                     bg0_ref, bg1_ref, lam_ref,
                     s0_ref, pf_ref, pb_ref, sum_ref,
                     e_ref, *, rows, blk_rows, ngroups):
    wg = pl.program_id(1)
    u = u_ref[...]
    prev_ok = wg >= 1
    next_ok = wg <= ngroups - 2
    sub = lax.broadcasted_iota(jnp.int32, (SUB, LRU_BLOCK), 0)
    hp = hp_ref[...]
    hn = hn_ref[...]

    def from_prev_column(cur, halo):
        edge = jnp.where(prev_ok, pltpu.roll(halo, 1, 0), 0.0)
        return jnp.where(sub == 0, edge, pltpu.roll(cur, 1, 0))

    e_ref[CONV_LEFT:CONV_LEFT + rows] = u
    e_ref[0] = from_prev_column(u[rows - 2], hp[0])
    e_ref[1] = from_prev_column(u[rows - 1], hp[1])
    edge = jnp.where(next_ok, pltpu.roll(hn[0], SUB - 1, 0), 0.0)
    e_ref[CONV_LEFT + rows] = jnp.where(sub == SUB - 1, edge, pltpu.roll(u[0], SUB - 1, 0))
    cw = 0.5 * cw_ref[...]
    cb = 0.5 * cb_ref[...]
    lam = lam_ref[...]
    c2 = (-0.5 * LRU_C * LOG2E) * jax.nn.softplus(-lam)
    nblk = rows // blk_rows
    ab = [[None] * nblk, [None] * nblk]
    for k in range(nblk):
        for d, blk in ((0, k), (1, nblk - 1 - k)):
            r0 = blk * blk_rows
            uch = cb + cw[0:1] * e_ref[r0:r0 + blk_rows]
            for j in range(1, CONV_W):
                uch = uch + cw[j:j + 1] * e_ref[r0 + j:r0 + j + blk_rows]
            uch = uch.reshape(blk_rows * SUB, LRU_BLOCK)
            ab[d][blk] = _lru_coeffs(uch.astype(BF16), uch, (wg0_ref, wg1_ref)[d], (bg0_ref, bg1_ref)[d],
                                     c2[d:d + 1], blk_rows)

    h = jnp.zeros((SUB, LRU_BLOCK), F32)
    p = jnp.ones((SUB, LRU_BLOCK), F32)
    for r in range(rows):
        a, b = ab[0][r // blk_rows]
        h = a[r % blk_rows] * h + b[r % blk_rows]
        p = a[r % blk_rows] * p
        s0_ref[r] = h
        pf_ref[r] = p
    sum_ref[0] = p
    sum_ref[1] = h
    h = jnp.zeros((SUB, LRU_BLOCK), F32)
    p = jnp.ones((SUB, LRU_BLOCK), F32)
    for r in range(rows - 1, -1, -1):
        a, b = ab[1][r // blk_rows]
        h = a[r % blk_rows] * h + b[r % blk_rows]
        p = a[r % blk_rows] * p
        s0_ref[r] = s0_ref[r] + h
        pb_ref[r] = p
    sum_ref[2] = p
    sum_ref[3] = h


def _lru_chain(csum_ref, sum_ref, hf_ref, hb_ref):
    s = jnp.zeros((1, D), F32)
    for w in range(SUB):
        s = csum_ref[0, w:w + 1, :] * s + csum_ref[1, w:w + 1, :]
    for w in range(GRID_W):
        hf_ref[w:w + 1, :] = s
        s = sum_ref[0, w:w + 1, :] * s + sum_ref[1, w:w + 1, :]
    s = jnp.zeros((1, D), F32)
    for w in range(SUB - 1, -1, -1):
        s = csum_ref[2, w:w + 1, :] * s + csum_ref[3, w:w + 1, :]
    for w in range(GRID_W - 1, -1, -1):
        hb_ref[w:w + 1, :] = s
        s = sum_ref[2, w:w + 1, :] * s + sum_ref[3, w:w + 1, :]


def _lru_out_kernel(x_ref, mod_ref, y_ref, s0_ref, pf_ref, pb_ref, csum_ref, sum_ref,
                    wout_ref, gffn_ref, w1_ref, w3_ref, w2_ref, gfin_ref, o_ref, hf_ref, hb_ref):
    @pl.when(pl.program_id(0) == 0)
    def _():
        _lru_chain(csum_ref, sum_ref, hf_ref, hb_ref)

    mod = _mod_rows(mod_ref, 0)
    rows = [slice(j * OUT_TM, (j + 1) * OUT_TM) for j in range(OUT_SUBTILES)]
    by_col = (OUT_TM // GRID_W, GRID_W, LRU_BLOCK)
    x1 = []
    for rs in rows:
        yl = None
        for c in range(LRU_BLOCKS):
            cs = slice(c * LRU_BLOCK, (c + 1) * LRU_BLOCK)
            hs = (s0_ref[rs, cs].reshape(by_col) + pf_ref[rs, cs].reshape(by_col) * hf_ref[:, cs]
                  + pb_ref[rs, cs].reshape(by_col) * hb_ref[:, cs]).reshape(OUT_TM, LRU_BLOCK)
            ym = (hs * y_ref[rs, cs].astype(F32)).astype(BF16)
            part = _dot(ym, wout_ref[cs, :])
            yl = part if yl is None else yl + part
        x1.append(x_ref[rs] + mod[2:3] * yl)
    x2 = _ffn_stages(x1, mod, gffn_ref[...], w1_ref, w3_ref, w2_ref)
    for rs, t in zip(rows, x2):
        o_ref[rs] = _rms(t, gfin_ref[...])


def _const(shape, index=None):
    nd = len(shape)
    index = (0,) * nd if index is None else index
    return pl.BlockSpec(shape, lambda *_: index, pipeline_mode=pl.Buffered(1))


def _params(n_axes=1):
    return pltpu.CompilerParams(dimension_semantics=("arbitrary",) * n_axes,
                                vmem_limit_bytes=V7X_VMEM_LIMIT)


def kernel(x, c, ctx, c_ctx, norm_mix_g, norm_ffn_g, w_mod, b_mod, gla_w_in, gla_gate_w1, gla_gate_w2, gla_gate_b, gla_head_norm_g, gla_w_out, lru_w_in, lru_conv_w, lru_conv_b, lru_gate_a_w, lru_gate_a_b, lru_gate_x_w, lru_gate_x_b, lru_lambda, lru_w_out, ffn_w1, ffn_w3, ffn_w2, final_norm_g):
    assert x.shape == (1, SEQ, D) and ctx.shape == (1, CTX, D) and CTX == TM == ROWS
    x2d = x.reshape(SEQ, D)
    ctx2d = ctx.reshape(CTX, D)
    nseq = (NT + SUBTILES) * TM

    mods = _modulation(jnp.stack([c[0], c_ctx], axis=1), w_mod, b_mod)

    def mod_spec(layer):
        return _const((None, SUB, 6 * D), (layer, 0, 0))

    vec = lambda t: t.reshape(1, -1)
    tile = lambda w, f: pl.BlockSpec((TM, w), f)

    def ffn_specs(layer):
        return [_const((1, D)), _const((None, D, FFN), (layer, 0, 0)),
                _const((None, D, FFN), (layer, 0, 0)), _const((None, FFN, D), (layer, 0, 0))]

    gw1 = jnp.concatenate([gla_gate_w1[0, 0], gla_gate_w1[0, 1]], axis=1)
    gw1 = jnp.pad(gw1, ((0, 0), (0, LANES - 2 * RANK))).astype(BF16)
    gw2 = jnp.zeros((LANES, 2 * DK), F32)
    gw2 = gw2.at[:RANK, :DK].set(gla_gate_w2[0, 0]).at[RANK:2 * RANK, DK:].set(gla_gate_w2[0, 1])
    gw2 = gw2.astype(BF16)
    gb = gla_gate_b[0].reshape(1, 2 * DK)

    pair = lambda w: pl.BlockSpec((SUBTILES * TM, w), lambda i: (i, 0))
    lat_pair = pl.BlockSpec((SUBTILES * TM, D), lambda i: (jnp.minimum(i, CTX_STEP - 1), 0))
    q, k, v, g, la = pl.pallas_call(
        _gla_proj_kernel,
        grid=(CTX_STEP + 1,),
        in_specs=[lat_pair, _const((CTX, D)), mod_spec(0), _const((1, D)),
                  _const((D, 2 * DK + 2 * DV)), _const((D, LANES)), _const((LANES, 2 * DK)),
                  _const((1, 2 * DK))],
        out_specs=[pair(DK), pair(DK), pair(DV), pair(DV), pair(2 * DK)],
        out_shape=[jax.ShapeDtypeStruct((nseq, DK), BF16), jax.ShapeDtypeStruct((nseq, DK), BF16),
                   jax.ShapeDtypeStruct((nseq, DV), BF16), jax.ShapeDtypeStruct((nseq, DV), BF16),
                   jax.ShapeDtypeStruct((nseq, 2 * DK), F32)],
        scratch_shapes=[pltpu.VMEM((D, 2 * DK + 2 * DV), BF16)],
        compiler_params=_params(),
        name="gla_proj",
    )(x2d, ctx2d, mods, vec(norm_mix_g[0]), gla_w_in[0], gw1, gw2, gb)

    seq_f = lambda i: (jnp.where(i == 0, NT, i - 1), 0)
    seq_b = lambda i: (jnp.where(i == 0, NT, NT - i), 0)
    streamed = [ffn_w1.reshape(2 * D, FFN), ffn_w3.reshape(2 * D, FFN), ffn_w2.reshape(2 * FFN, D),
                gla_w_out[0], lru_w_in[0], lru_w_out[0]]
    assert len(streamed) == N_STREAMED_WEIGHTS

    def slab_spec(w):
        nblk = NT if w.shape[0] % (NT * BF16_SUBLANES) == 0 else NT // 2
        rows = w.shape[0] // nblk
        assert rows * nblk == w.shape[0] and rows % BF16_SUBLANES == 0
        return pl.BlockSpec((rows, w.shape[1]), lambda i: (jnp.minimum(i, nblk - 1), 0))

    slabs = [slab_spec(w) for w in streamed]
    la_b = lambda i: (seq_b(i)[0], 1)
    o_f, o_b, w1b, w3b, w2b, gwob, lwinb, lwob = pl.pallas_call(
        _gla_scan_kernel,
        grid=(NT + 1,),
        in_specs=[tile(DK, seq_f), tile(DK, seq_f), tile(DV, seq_f), tile(DK, seq_f),
                  tile(DK, seq_b), tile(DK, seq_b), tile(DV, seq_b), tile(DK, la_b)] + slabs,
        out_specs=[tile(DV, seq_f), tile(DV, seq_b)] + slabs,
        out_shape=[jax.ShapeDtypeStruct(((NT + 1) * TM, DV), BF16)] * 2
                  + [jax.ShapeDtypeStruct(w.shape, BF16) for w in streamed],
        scratch_shapes=[pltpu.VMEM((HEADS, HK, HV), F32)] * 2,
        compiler_params=_params(),
        name="gla_scan",
    )(q, k, v, la, q, k, v, la, *streamed)
    w1b, w3b, w2b = w1b.reshape(2, D, FFN), w3b.reshape(2, D, FFN), w2b.reshape(2, FFN, D)

    sub = lambda w, j: pl.BlockSpec((TM, w), lambda i: (jnp.minimum(SUBTILES * i + j, NT), 0))
    x_mid, y_br, u_all = pl.pallas_call(
        _gla_out_kernel,
        grid=(CTX_STEP + 1,),
        in_specs=[lat_pair, _const((CTX, D)), mod_spec(0), mod_spec(1),
                  sub(DV, 0), sub(DV, 1), sub(DV, 0), sub(DV, 1), sub(DV, 0), sub(DV, 1),
                  _const((1, HV)), _const((DV, D))] + ffn_specs(0) + [_const((1, D)), _const((D, 2 * D))],
        out_specs=[pair(D), pair(D), pair(D)],
        out_shape=[jax.ShapeDtypeStruct((nseq, D), F32), jax.ShapeDtypeStruct((nseq, D), BF16),
                   jax.ShapeDtypeStruct((nseq, D), F32)],
        compiler_params=_params(),
        name="gla_out",
    )(x2d, ctx2d, mods, mods, o_f, o_f, o_b, o_b, g, g,
      vec(gla_head_norm_g[0]), gwob, vec(norm_ffn_g[0]), w1b, w3b, w2b,
      vec(norm_mix_g[1]), lwinb)

    u3 = u_all.reshape(nseq // GRID_W, GRID_W, D)
    uctx3 = u_all[NT * TM:(NT + 1) * TM].reshape(SUB, CTX_SEG, D).transpose(1, 0, 2)
    gate_w = lambda d: jnp.concatenate([lru_gate_a_w[0, d], lru_gate_x_w[0, d]], axis=-1).astype(BF16)
    gate_b = lambda d: 0.5 * jnp.stack([lru_gate_a_b[0, d], lru_gate_x_b[0, d]])
    chan = lambda rows: pl.BlockSpec((rows, LRU_BLOCK), lambda cb, wg: (0, cb))
    gate_spec = pl.BlockSpec((None, LRU_BLOCK, 2 * LRU_BLOCK), lambda cb, wg: (cb, 0, 0))
    lru_weights = (lru_conv_w[0], vec(lru_conv_b[0]), gate_w(0), gate_w(1), gate_b(0), gate_b(1),
                   lru_lambda[0])

    def column_scans(u3d, rows, blk_rows, ngroups):
        blk = (rows, SUB, LRU_BLOCK)
        here = pl.BlockSpec(blk, lambda cb, wg: (0, wg, cb))
        return pl.pallas_call(
            functools.partial(_lru_scan_kernel, rows=rows, blk_rows=blk_rows, ngroups=ngroups),
            grid=(LRU_BLOCKS, ngroups),
            in_specs=[here,
                      pl.BlockSpec((2, SUB, LRU_BLOCK), lambda cb, wg: (rows // 2 - 1, jnp.maximum(wg - 1, 0), cb)),
                      pl.BlockSpec((2, SUB, LRU_BLOCK), lambda cb, wg: (0, jnp.minimum(wg + 1, ngroups - 1), cb)),
                      chan(CONV_W), chan(1), gate_spec, gate_spec, chan(2), chan(2), chan(2)],
            out_specs=[here, here, here, pl.BlockSpec((4, SUB, LRU_BLOCK), lambda cb, wg: (0, wg, cb))],
            out_shape=[jax.ShapeDtypeStruct((rows, ngroups * SUB, D), F32)] * 3
                      + [jax.ShapeDtypeStruct((4, ngroups * SUB, D), F32)],
            scratch_shapes=[pltpu.VMEM((rows + CONV_W - 1, SUB, LRU_BLOCK), F32)],
            compiler_params=_params(2),
            name=f"lru_scan_{ngroups * SUB}x{rows}",
        )(u3d, u3d, u3d, *lru_weights)

    ctx_sums = column_scans(uctx3, CTX_SEG, CTX_SEG, 1)[3]
    s0, pf, pb, sums = column_scans(u3, ROWS, SCAN_ROWS, NGRP)

    lat = lambda i: (i, 0)
    big = lambda f: pl.BlockSpec((OUT_SUBTILES * OUT_TM, D), f)
    out = pl.pallas_call(
        _lru_out_kernel,
        grid=(SEQ // (OUT_SUBTILES * OUT_TM),),
        in_specs=[big(lat), mod_spec(1),
                  big(lat), big(lat), big(lat), big(lat),
                  _const((4, SUB, D)), _const((4, GRID_W, D)), _const((D, D))] + ffn_specs(1) + [_const((1, D))],
        out_specs=big(lat),
        out_shape=jax.ShapeDtypeStruct((SEQ, D), F32),
        scratch_shapes=[pltpu.VMEM((GRID_W, D), F32)] * 2,
        compiler_params=_params(),
        name="lru_out",
    )(x_mid, mods, y_br, s0.reshape(SEQ, D), pf.reshape(SEQ, D), pb.reshape(SEQ, D),
      ctx_sums, sums,
      lwob, vec(norm_ffn_g[1]), w1b, w3b, w2b, vec(final_norm_g))
    return out.reshape(1, SEQ, D)
```
